```python
import math
import jax, jax.numpy as jnp
from jax import lax
import numpy as np

D_MODEL = 1024
BATCH = 8
SEQ = 4096
DEPTH = 2
DEC_BATCH = 32
DEC_SEQ = 32
PAST_LEN = 1024

CHUNK = 64
Q_BLOCK = 128
SSD_HEADS = 16
SSD_HEADDIM = 64
SSD_INNER = SSD_HEADS * SSD_HEADDIM
SSD_GROUPS = 2
HEADS_PER_GROUP = SSD_HEADS // SSD_GROUPS
SSD_STATE = 128
CONV_WIDTH = 4
CONV_DIM = SSD_INNER + 2 * SSD_GROUPS * SSD_STATE
MLA_HEADS = 8
Q_LORA = 384
KV_LORA = 256
NOPE_DIM = 128
ROPE_DIM = 64
V_DIM = 128
ROPE_THETA = 10000.0
MLA_INNER = MLA_HEADS * V_DIM
MIX_WIDTH = SSD_INNER + MLA_INNER
IN_SPLITS = (SSD_INNER, CONV_DIM, SSD_HEADS, Q_LORA, KV_LORA, ROPE_DIM)
IN_DIM = SSD_INNER + CONV_DIM + SSD_HEADS + Q_LORA + KV_LORA + ROPE_DIM
D_FF = 2816
N_EXPERTS = 8
TOP_K = 2
EXPERT_FF = 3584
EPS = 1e-6

kernel_name = 'hybrid_ssd_mla_stream_step'


def rmsnorm(x, w):
    xf = x.astype(jnp.float32)
    y = xf * lax.rsqrt(jnp.mean(xf * xf, axis=-1, keepdims=True) + EPS)
    return (y * w.astype(jnp.float32)).astype(x.dtype)


def split_last(x, sizes):
    idx = np.cumsum(sizes)[:-1].tolist()
    return jnp.split(x, idx, axis=-1)


def rope(x, pos):
    half = x.shape[-1] // 2
    inv_freq = ROPE_THETA ** (-jnp.arange(half, dtype=jnp.float32) / half)
    ang = pos.astype(jnp.float32)[:, None] * inv_freq[None, :]
    cos = jnp.cos(ang)[:, None, :]
    sin = jnp.sin(ang)[:, None, :]
    x1 = x[..., :half].astype(jnp.float32)
    x2 = x[..., half:].astype(jnp.float32)
    out = jnp.concatenate([x1 * cos - x2 * sin, x1 * sin + x2 * cos], axis=-1)
    return out.astype(x.dtype)


def chunk_causal_attention(q, k, v, q_pos, k_pos):
    bsz, sq, nh, dk = q.shape
    dv = v.shape[-1]
    scale = dk ** -0.5
    k_chunk = k_pos // CHUNK

    def block(args):
        qb, qpb = args
        s = jnp.einsum('bqhd,bkhd->bhqk', qb, k, preferred_element_type=jnp.float32) * scale
        mask = k_chunk[None, :] <= (qpb[:, None] // CHUNK)
        s = jnp.where(mask, s, -jnp.inf)
        p = jax.nn.softmax(s, axis=-1).astype(v.dtype)
        return jnp.einsum('bhqk,bkhd->bqhd', p, v)

    if sq > Q_BLOCK and sq % Q_BLOCK == 0:
        nb = sq // Q_BLOCK
        qs = jnp.moveaxis(q.reshape(bsz, nb, Q_BLOCK, nh, dk), 1, 0)
        ps = q_pos.reshape(nb, Q_BLOCK)
        out = lax.map(block, (qs, ps))
        return jnp.moveaxis(out, 0, 1).reshape(bsz, sq, nh, dv)
    return block((q, q_pos))


def ssd_scan(xdt, da, b, c, h0):
    bsz, length = xdt.shape[:2]
    t = CHUNK if length % CHUNK == 0 else length
    nc = length // t
    xdt = xdt.reshape(bsz, nc, t, SSD_GROUPS, HEADS_PER_GROUP, SSD_HEADDIM)
    da = da.reshape(bsz, nc, t, SSD_GROUPS, HEADS_PER_GROUP)
    b = b.reshape(bsz, nc, t, SSD_GROUPS, SSD_STATE)
    c = c.reshape(bsz, nc, t, SSD_GROUPS, SSD_STATE)
    cs = jnp.cumsum(da, axis=2)
    causal = jnp.tril(jnp.ones((t, t), dtype=bool))[:, :, None, None]
    seg = cs[:, :, :, None] - cs[:, :, None, :]
    decay = jnp.exp(jnp.where(causal, seg, -jnp.inf))
    cb = jnp.einsum('bctgn,bcsgn->bctsg', c, b)
    y_diag = jnp.einsum('bctsgr,bcsgrp->bctgrp', cb[..., None] * decay, xdt)
    to_end = jnp.exp(cs[:, :, -1:] - cs)
    local = jnp.einsum('bctgn,bctgrp->bcgrpn', b, xdt * to_end[..., None])
    chunk_decay = jnp.exp(cs[:, :, -1])

    def carry_step(h, inp):
        loc, dec = inp
        return h * dec[..., None, None] + loc, h

    h_final, h_in = lax.scan(carry_step, h0,
                             (jnp.moveaxis(local, 1, 0), jnp.moveaxis(chunk_decay, 1, 0)))
    h_in = jnp.moveaxis(h_in, 0, 1)
    y_off = jnp.einsum('bctgn,bcgrpn->bctgrp', c, h_in) * jnp.exp(cs)[..., None]
    y = (y_diag + y_off).reshape(bsz, length, SSD_GROUPS, HEADS_PER_GROUP, SSD_HEADDIM)
    return y, h_final


def ssd_mixer(z, xbc, dt_raw, conv_buf, ssm_state, conv_w, conv_b, dt_bias, a_log, d_skip, norm_w):
    bsz, length = xbc.shape[:2]
    xpad = jnp.concatenate([conv_buf.astype(xbc.dtype), xbc], axis=1)
    new_conv = xpad[:, xpad.shape[1] - (CONV_WIDTH - 1):]
    conv = conv_b
    for k in range(CONV_WIDTH):
        conv = conv + xpad[:, k:k + length] * conv_w[k]
    xbc = jax.nn.silu(conv)
    xs, bs, cmat = split_last(xbc, (SSD_INNER, SSD_GROUPS * SSD_STATE, SSD_GROUPS * SSD_STATE))
    xs = xs.reshape(bsz, length, SSD_GROUPS, HEADS_PER_GROUP, SSD_HEADDIM).astype(jnp.float32)
    bs = bs.reshape(bsz, length, SSD_GROUPS, SSD_STATE).astype(jnp.float32)
    cmat = cmat.reshape(bsz, length, SSD_GROUPS, SSD_STATE).astype(jnp.float32)
    dt = jax.nn.softplus(dt_raw.astype(jnp.float32) + dt_bias.astype(jnp.float32))
    dt = dt.reshape(bsz, length, SSD_GROUPS, HEADS_PER_GROUP)
    a = -jnp.exp(a_log.astype(jnp.float32)).reshape(SSD_GROUPS, HEADS_PER_GROUP)
    h0 = ssm_state.astype(jnp.float32).reshape(bsz, SSD_GROUPS, HEADS_PER_GROUP, SSD_HEADDIM, SSD_STATE)
    y, h_final = ssd_scan(xs * dt[..., None], dt * a, bs, cmat, h0)
    y = y + xs * d_skip.astype(jnp.float32).reshape(SSD_GROUPS, HEADS_PER_GROUP, 1)
    y = y.reshape(bsz, length, SSD_INNER).astype(z.dtype)
    y = rmsnorm(y * jax.nn.silu(z), norm_w)
    new_ssm = h_final.reshape(bsz, SSD_HEADS, SSD_HEADDIM, SSD_STATE).astype(ssm_state.dtype)
    return y, new_conv, new_ssm


def mla_mixer(cq, ckv_raw, kr_raw, pos, past_latent, past_krope, q_norm_w, w_q_up, kv_norm_w, w_kv_up):
    bsz, length = cq.shape[:2]
    q = (rmsnorm(cq, q_norm_w) @ w_q_up).reshape(bsz, length, MLA_HEADS, NOPE_DIM + ROPE_DIM)
    q = jnp.concatenate([q[..., :NOPE_DIM], rope(q[..., NOPE_DIM:], pos)], axis=-1)
    c_kv = rmsnorm(ckv_raw, kv_norm_w)
    k_rope = rope(kr_raw[:, :, None, :], pos)[:, :, 0]
    if past_latent is None:
        lat_all, kr_all, k_pos = c_kv, k_rope, pos
    else:
        past = past_latent.shape[1]
        lat_all = jnp.concatenate([past_latent.astype(c_kv.dtype), c_kv], axis=1)
        kr_all = jnp.concatenate([past_krope.astype(k_rope.dtype), k_rope], axis=1)
        k_pos = jnp.arange(past + length, dtype=jnp.int32)
    n_keys = lat_all.shape[1]
    kv = (lat_all @ w_kv_up).reshape(bsz, n_keys, MLA_HEADS, NOPE_DIM + V_DIM)
    k = jnp.concatenate([kv[..., :NOPE_DIM],
                         jnp.broadcast_to(kr_all[:, :, None, :], (bsz, n_keys, MLA_HEADS, ROPE_DIM))], axis=-1)
    v = kv[..., NOPE_DIM:]
    o = chunk_causal_attention(q, k, v, pos, k_pos)
    return o.reshape(bsz, length, MLA_INNER), c_kv, k_rope


def mixing_sublayer(x, pos, conv_buf, ssm_state, past_latent, past_krope,
                    norm_w, w_in, conv_w, conv_b, dt_bias, a_log, d_skip, ssd_norm_w,
                    q_norm_w, w_q_up, kv_norm_w, w_kv_up, w_out):
    h = rmsnorm(x, norm_w)
    z, xbc, dt_raw, cq, ckv, kr = split_last(h @ w_in, IN_SPLITS)
    y_ssd, new_conv, new_ssm = ssd_mixer(z, xbc, dt_raw, conv_buf, ssm_state, conv_w, conv_b,
                                         dt_bias, a_log, d_skip, ssd_norm_w)
    y_mla, c_kv, k_rope = mla_mixer(cq, ckv, kr, pos, past_latent, past_krope,
                                    q_norm_w, w_q_up, kv_norm_w, w_kv_up)
    y = jnp.concatenate([y_ssd, y_mla], axis=-1) @ w_out
    return x + y, c_kv, k_rope, new_conv, new_ssm


def swiglu(h, w_gate, w_up, w_down):
    return (jax.nn.silu(h @ w_gate) * (h @ w_up)) @ w_down


def moe_swiglu(h, w_router, w_gate, w_up, w_down):
    logits = jnp.einsum('bsd,de->bse', h, w_router, preferred_element_type=jnp.float32)
    top_val, top_idx = lax.top_k(logits, TOP_K)
    gates = jax.nn.softmax(top_val, axis=-1)
    combine = jnp.einsum('bske,bsk->bse', jax.nn.one_hot(top_idx, N_EXPERTS, dtype=jnp.float32),
                         gates).astype(h.dtype)
    out = jnp.zeros_like(h)
    for e in range(N_EXPERTS):
        out = out + combine[..., e:e + 1] * swiglu(h, w_gate[e], w_up[e], w_down[e])
    return out


def setup_inputs(seed: int = 0) -> dict:
    key = jax.random.key(seed)
    ks = iter(jax.random.split(key, 48))
    f32 = jnp.float32

    def nrm(shape, scale):
        return jax.random.normal(next(ks), shape, f32) * scale

    def gain(shape):
        return 1.0 + nrm(shape, 0.02)

    n_dense = (DEPTH + 1) // 2
    n_moe = DEPTH // 2
    dt0 = jnp.exp(jax.random.uniform(next(ks), (DEPTH, SSD_HEADS), f32,
                                     minval=math.log(1e-3), maxval=math.log(1e-1)))
    dt_bias = dt0 + jnp.log(-jnp.expm1(-dt0))
    a_log = jnp.log(jax.random.uniform(next(ks), (DEPTH, SSD_HEADS), f32, minval=1.0, maxval=16.0))
    return {
        'x_prompt': nrm((BATCH, SEQ, D_MODEL), 1.0),
        'x_sample': nrm((DEC_BATCH, DEC_SEQ, D_MODEL), 1.0),
        'cache_kv_latent': nrm((DEPTH, DEC_BATCH, PAST_LEN, KV_LORA), 1.0),
        'cache_k_rope': nrm((DEPTH, DEC_BATCH, PAST_LEN, ROPE_DIM), 1.0),
        'state_ssm': nrm((DEPTH, DEC_BATCH, SSD_HEADS, SSD_HEADDIM, SSD_STATE), 0.1),
        'state_conv': nrm((DEPTH, DEC_BATCH, CONV_WIDTH - 1, CONV_DIM), 1.0),
        'norm_mix': gain((DEPTH, D_MODEL)),
        'w_in': nrm((DEPTH, D_MODEL, IN_DIM), D_MODEL ** -0.5),
        'conv_w': nrm((DEPTH, CONV_WIDTH, CONV_DIM), CONV_WIDTH ** -0.5),
        'conv_b': nrm((DEPTH, CONV_DIM), 0.02),
        'dt_bias': dt_bias,
        'a_log': a_log,
        'd_skip': gain((DEPTH, SSD_HEADS)),
        'ssd_norm': gain((DEPTH, SSD_INNER)),
        'q_norm': gain((DEPTH, Q_LORA)),
        'w_q_up': nrm((DEPTH, Q_LORA, MLA_HEADS * (NOPE_DIM + ROPE_DIM)), Q_LORA ** -0.5),
        'kv_norm': gain((DEPTH, KV_LORA)),
        'w_kv_up': nrm((DEPTH, KV_LORA, MLA_HEADS * (NOPE_DIM + V_DIM)), KV_LORA ** -0.5),
        'w_out': nrm((DEPTH, MIX_WIDTH, D_MODEL), MIX_WIDTH ** -0.5),
        'norm_ffn': gain((DEPTH, D_MODEL)),
        'ffn_w_gate': nrm((n_dense, D_MODEL, D_FF), D_MODEL ** -0.5),
        'ffn_w_up': nrm((n_dense, D_MODEL, D_FF), D_MODEL ** -0.5),
        'ffn_w_down': nrm((n_dense, D_FF, D_MODEL), D_FF ** -0.5),
        'moe_router': nrm((n_moe, D_MODEL, N_EXPERTS), D_MODEL ** -0.5),
        'moe_w_gate': nrm((n_moe, N_EXPERTS, D_MODEL, EXPERT_FF), D_MODEL ** -0.5),
        'moe_w_up': nrm((n_moe, N_EXPERTS, D_MODEL, EXPERT_FF), D_MODEL ** -0.5),
        'moe_w_down': nrm((n_moe, N_EXPERTS, EXPERT_FF, D_MODEL), EXPERT_FF ** -0.5),
        'final_norm': gain((D_MODEL,)),
    }


def reference(x_prompt, x_sample, cache_kv_latent, cache_k_rope, state_ssm, state_conv,
              norm_mix, w_in, conv_w, conv_b, dt_bias, a_log, d_skip, ssd_norm,
              q_norm, w_q_up, kv_norm, w_kv_up, w_out, norm_ffn,
              ffn_w_gate, ffn_w_up, ffn_w_down,
              moe_router, moe_w_gate, moe_w_up, moe_w_down, final_norm):
    bp, sp = x_prompt.shape[:2]
    past = cache_kv_latent.shape[2]
    pos_p = jnp.arange(sp, dtype=jnp.int32)
    pos_s = past + jnp.arange(x_sample.shape[1], dtype=jnp.int32)
    hp, hs = x_prompt, x_sample
    lat_p, kr_p, ssm_p, conv_p = [], [], [], []
    lat_s, kr_s, ssm_s, conv_s = [], [], [], []
    for layer in range(DEPTH):
        mix = (norm_mix[layer], w_in[layer], conv_w[layer], conv_b[layer], dt_bias[layer],
               a_log[layer], d_skip[layer], ssd_norm[layer], q_norm[layer], w_q_up[layer],
               kv_norm[layer], w_kv_up[layer], w_out[layer])
        zero_conv = jnp.zeros((bp, CONV_WIDTH - 1, CONV_DIM), hp.dtype)
        zero_ssm = jnp.zeros((bp, SSD_HEADS, SSD_HEADDIM, SSD_STATE), hp.dtype)
        hp, lat, kr, cv, sm = mixing_sublayer(hp, pos_p, zero_conv, zero_ssm, None, None, *mix)
        lat_p.append(lat); kr_p.append(kr); conv_p.append(cv); ssm_p.append(sm)
        hs, lat, kr, cv, sm = mixing_sublayer(hs, pos_s, state_conv[layer], state_ssm[layer],
                                              cache_kv_latent[layer], cache_k_rope[layer], *mix)
        lat_s.append(lat); kr_s.append(kr); conv_s.append(cv); ssm_s.append(sm)
        if layer % 2 == 0:
            i = layer // 2
            hp = hp + swiglu(rmsnorm(hp, norm_ffn[layer]), ffn_w_gate[i], ffn_w_up[i], ffn_w_down[i])
            hs = hs + swiglu(rmsnorm(hs, norm_ffn[layer]), ffn_w_gate[i], ffn_w_up[i], ffn_w_down[i])
        else:
            i = layer // 2
            hp = hp + moe_swiglu(rmsnorm(hp, norm_ffn[layer]), moe_router[i], moe_w_gate[i],
                                 moe_w_up[i], moe_w_down[i])
            hs = hs + moe_swiglu(rmsnorm(hs, norm_ffn[layer]), moe_router[i], moe_w_gate[i],
                                 moe_w_up[i], moe_w_down[i])
    y_prompt = rmsnorm(hp, final_norm)
    y_sample = rmsnorm(hs, final_norm)
    return (y_prompt, y_sample,
            jnp.stack(lat_p), jnp.stack(kr_p), jnp.stack(ssm_p), jnp.stack(conv_p),
            jnp.stack(lat_s), jnp.stack(kr_s), jnp.stack(ssm_s), jnp.stack(conv_s))
```

```python
import functools
import math

import jax
import jax.numpy as jnp
import numpy as np
from jax import lax
from jax.experimental import pallas as pl
from jax.experimental.pallas import tpu as pltpu

F32 = jnp.float32
BF16 = jnp.bfloat16
I32 = jnp.int32
HI = lax.Precision.HIGHEST

D_MODEL = 1024
CHUNK = 64
LOG2_CHUNK = 6
SSD_HEADS = 16
SSD_HEADDIM = 64
SSD_INNER = SSD_HEADS * SSD_HEADDIM
SSD_GROUPS = 2
HEADS_PER_GROUP = SSD_HEADS // SSD_GROUPS
SSD_STATE = 128
CONV_WIDTH = 4
CONV_DIM = SSD_INNER + 2 * SSD_GROUPS * SSD_STATE
MLA_HEADS = 8
Q_LORA = 384
KV_LORA = 256
NOPE_DIM = 128
ROPE_DIM = 64
V_DIM = 128
QK_DIM = NOPE_DIM + ROPE_DIM
ROPE_THETA = 10000.0
MLA_INNER = MLA_HEADS * V_DIM
N_EXPERTS = 8
TOP_K = 2
EPS = 1e-6

LANES = 128
SUBLANES = 8
MXU_DIM = 256
VMEM_LIMIT = 56 * 1024 * 1024

OFF_Z = 0
OFF_XBC = OFF_Z + SSD_INNER
OFF_CQ = OFF_XBC + CONV_DIM
OFF_CKV = OFF_CQ + Q_LORA
OFF_KR = OFF_CKV + KV_LORA
OFF_KRS = OFF_KR + ROPE_DIM
OFF_DT = OFF_KRS + ROPE_DIM
IN_COLS = OFF_DT + LANES
Q_HEAD_COLS = NOPE_DIM + 2 * ROPE_DIM

TOKEN_TILE = 512
ATT_TILE = 256
MOE_TILE = 256
DISPATCH_TILE = 512
COMBINE_TILE = 256


def _params(n_axes):
    return pltpu.CompilerParams(dimension_semantics=("arbitrary",) * n_axes,
                                vmem_limit_bytes=VMEM_LIMIT)


def _const_spec(shape):
    zeros = (0,) * len(shape)
    return pl.BlockSpec(shape, lambda *_: zeros)


def _rms(x, w):
    return x * lax.rsqrt(jnp.mean(x * x, axis=-1, keepdims=True) + EPS) * w


def _silu(x):
    return x * (1.0 / (1.0 + jnp.exp(-x)))


def _dot(a, b):
    return jnp.dot(a, b, preferred_element_type=F32)


def _dot_nt(a, b):
    return lax.dot_general(a, b, (((1,), (1,)), ((), ())), preferred_element_type=F32)


def _dot_tn(a, b):
    return lax.dot_general(a, b, (((0,), (0,)), ((), ())), preferred_element_type=F32)


def _dot_hi(a, b):
    return jnp.dot(a, b, preferred_element_type=F32, precision=HI)


ROW_TILE = (D_MODEL // LANES, LANES)


def _row_tile_spec(tm):
    return pl.BlockSpec((tm,) + ROW_TILE, lambda i, *_: (i, 0, 0))


def _store_row_tiles(ref, x):
    for s in range(ROW_TILE[0]):
        ref[:, s, :] = x[:, s * LANES:(s + 1) * LANES]


def _load_row_tiles(ref, *lead):
    return jnp.concatenate([ref[lead + (slice(None), s, slice(None))] for s in range(ROW_TILE[0])], axis=1)


def _inproj_body(x_ref, nw_ref, win_ref, qnw_ref, wq_ref, kvnw_ref, wkv_ref, cos_ref, sin_ref, dtb_ref,
                 z_ref, xbc_ref, dt_ref, ckv_ref, kr_ref, q_ref, k_ref, v_ref):
    h = _rms(x_ref[...], nw_ref[...]).astype(BF16)
    proj = _dot(h, win_ref[...])
    z_ref[...] = proj[:, OFF_Z:OFF_XBC]
    xbc_ref[...] = proj[:, OFF_XBC:OFF_CQ]
    pre = proj[:, OFF_DT:IN_COLS] + dtb_ref[...]
    dt_ref[...] = jnp.maximum(pre, 0.0) + jnp.log(1.0 + jnp.exp(-jnp.abs(pre)))
    cos = cos_ref[...]
    sin = sin_ref[...]
    k_rope = proj[:, OFF_KR:OFF_KRS] * cos + proj[:, OFF_KRS:OFF_DT] * sin
    kr_ref[...] = k_rope
    c_kv = _rms(proj[:, OFF_CKV:OFF_KR], kvnw_ref[...])
    ckv_ref[...] = c_kv
    qn = _rms(proj[:, OFF_CQ:OFF_CKV], qnw_ref[...]).astype(BF16)
    q_all = _dot(qn, wq_ref[...])
    kv_all = _dot(c_kv.astype(BF16), wkv_ref[...])
    k_rope_b = k_rope.astype(BF16)
    for hd in range(MLA_HEADS):
        qb = q_all[:, hd * Q_HEAD_COLS:(hd + 1) * Q_HEAD_COLS]
        q_rope = qb[:, NOPE_DIM:QK_DIM] * cos + qb[:, QK_DIM:Q_HEAD_COLS] * sin
        q_ref[hd] = jnp.concatenate([qb[:, :NOPE_DIM], q_rope], axis=1).astype(BF16)
        kvb = kv_all[:, hd * (NOPE_DIM + V_DIM):(hd + 1) * (NOPE_DIM + V_DIM)]
        k_ref[hd] = jnp.concatenate([kvb[:, :NOPE_DIM].astype(BF16), k_rope_b], axis=1)
        v_ref[hd] = kvb[:, NOPE_DIM:].astype(BF16)


def _inproj(x, lw, cos_tab, sin_tab, tm):
    n = x.shape[0]
    nt = n // tm
    tab_blocks = cos_tab.shape[0] // tm
    row = lambda w: pl.BlockSpec((tm, w), lambda i: (i, 0))
    head = lambda w: pl.BlockSpec((MLA_HEADS, tm, w), lambda i: (0, i, 0))
    tab = pl.BlockSpec((tm, ROPE_DIM), lambda i: (i % tab_blocks, 0))
    return pl.pallas_call(
        _inproj_body,
        grid=(nt,),
        in_specs=[row(D_MODEL), _const_spec((1, D_MODEL)), _const_spec((D_MODEL, IN_COLS)),
                  _const_spec((1, Q_LORA)), _const_spec((Q_LORA, MLA_HEADS * Q_HEAD_COLS)),
                  _const_spec((1, KV_LORA)), _const_spec((KV_LORA, MLA_HEADS * (NOPE_DIM + V_DIM))),
                  tab, tab, _const_spec((1, LANES))],
        out_specs=[row(SSD_INNER), row(CONV_DIM), row(LANES), row(KV_LORA), row(ROPE_DIM),
                   head(QK_DIM), head(QK_DIM), head(V_DIM)],
        out_shape=[jax.ShapeDtypeStruct((n, SSD_INNER), F32), jax.ShapeDtypeStruct((n, CONV_DIM), F32),
                   jax.ShapeDtypeStruct((n, LANES), F32), jax.ShapeDtypeStruct((n, KV_LORA), F32),
                   jax.ShapeDtypeStruct((n, ROPE_DIM), F32),
                   jax.ShapeDtypeStruct((MLA_HEADS, n, QK_DIM), BF16),
                   jax.ShapeDtypeStruct((MLA_HEADS, n, QK_DIM), BF16),
                   jax.ShapeDtypeStruct((MLA_HEADS, n, V_DIM), BF16)],
        compiler_params=_params(1),
        name="inproj",
    )(x, lw["norm_mix"], lw["w_in"], lw["q_norm"], lw["w_q"], lw["kv_norm"], lw["w_kv"],
      cos_tab, sin_tab, lw["dt_bias"])


def _pastkv_body(lat_ref, kr_ref, wkv_ref, k_ref, v_ref):
    kv_all = _dot(lat_ref[...].astype(BF16), wkv_ref[...])
    k_rope_b = kr_ref[...].astype(BF16)
    for hd in range(MLA_HEADS):
        kvb = kv_all[:, hd * (NOPE_DIM + V_DIM):(hd + 1) * (NOPE_DIM + V_DIM)]
        k_ref[hd] = jnp.concatenate([kvb[:, :NOPE_DIM].astype(BF16), k_rope_b], axis=1)
        v_ref[hd] = kvb[:, NOPE_DIM:].astype(BF16)


def _pastkv(lat, kr, w_kv, tm):
    n = lat.shape[0]
    row = lambda w: pl.BlockSpec((tm, w), lambda i: (i, 0))
    head = lambda w: pl.BlockSpec((MLA_HEADS, tm, w), lambda i: (0, i, 0))
    return pl.pallas_call(
        _pastkv_body,
        grid=(n // tm,),
        in_specs=[row(KV_LORA), row(ROPE_DIM), _const_spec(w_kv.shape)],
        out_specs=[head(QK_DIM), head(V_DIM)],
        out_shape=[jax.ShapeDtypeStruct((MLA_HEADS, n, QK_DIM), BF16),
                   jax.ShapeDtypeStruct((MLA_HEADS, n, V_DIM), BF16)],
        compiler_params=_params(1),
        name="pastkv",
    )(lat, kr, w_kv)


def _ssd_body(t, has_init, *refs):
    if has_init:
        (xbc_ref, dt_ref, z_ref, cw_ref, cb_ref, alog_ref, dskip_ref, nw_ref, conv0_ref, ssm0_ref,
         y_ref, ssm_ref, conv_ref, hist_ref) = refs
    else:
        (xbc_ref, dt_ref, z_ref, cw_ref, cb_ref, alog_ref, dskip_ref, nw_ref,
         y_ref, ssm_ref, conv_ref, hist_ref) = refs
    hist_rows = SUBLANES
    first = hist_rows - (CONV_WIDTH - 1)
    group_lanes = HEADS_PER_GROUP * t
    group_cols = HEADS_PER_GROUP * SSD_HEADDIM
    heads_per_block = MXU_DIM // t
    blocks_per_group = group_lanes // MXU_DIM
    log2_t = int(math.log2(t))
    log2_p = int(math.log2(SSD_HEADDIM))

    @pl.when(pl.program_id(1) == 0)
    def _():
        hist_ref[...] = jnp.zeros_like(hist_ref)
        if has_init:
            hist_ref[first:hist_rows, :] = conv0_ref[...]
            ssm_ref[...] = ssm0_ref[...]
        else:
            ssm_ref[...] = jnp.zeros_like(ssm_ref)

    x_raw = xbc_ref[...]
    x_pad = jnp.concatenate([hist_ref[...], x_raw], axis=0)
    conv = cb_ref[...]
    for k in range(CONV_WIDTH):
        conv = conv + x_pad[first + k:first + k + t] * cw_ref[k:k + 1, :]
    hist_ref[first:hist_rows, :] = x_raw[t - (CONV_WIDTH - 1):t]
    conv_ref[...] = x_raw[t - (CONV_WIDTH - 1):t]
    act = _silu(conv)
    xs = act[:, :SSD_INNER]
    b_mat = act[:, SSD_INNER:SSD_INNER + SSD_GROUPS * SSD_STATE]
    c_mat = act[:, SSD_INNER + SSD_GROUPS * SSD_STATE:]

    dt = dt_ref[:, :SSD_HEADS]
    da = dt * (-jnp.exp(alog_ref[...]))
    tril = (lax.broadcasted_iota(I32, (t, t), 0) >= lax.broadcasted_iota(I32, (t, t), 1)).astype(F32)
    cs = _dot_hi(tril, da)
    cs_last = cs[t - 1:t, :]
    head_of = lambda width, shift: (lax.broadcasted_iota(I32, (SSD_HEADS, width), 1) >> shift) == \
        lax.broadcasted_iota(I32, (SSD_HEADS, width), 0)
    expand_p = head_of(SSD_INNER, log2_p).astype(F32)
    expand_t = head_of(SSD_HEADS * t, log2_t).astype(F32)
    cs_t = _dot_hi(cs, expand_t)
    lane = lax.broadcasted_iota(I32, (t, SSD_HEADS * t), 1) & (t - 1)
    rowi = lax.broadcasted_iota(I32, (t, SSD_HEADS * t), 0)
    cs_src = jnp.sum(jnp.where(lane == rowi, cs_t, 0.0), axis=0, keepdims=True)
    decay = jnp.exp(jnp.where(rowi >= lane, cs_t - cs_src, -jnp.inf))
    exp_cs = _dot_hi(jnp.exp(cs), expand_p)
    to_end = _dot_hi(jnp.exp(cs_last - cs), expand_p)
    xdt = xs * _dot_hi(dt, expand_p)
    x_end = (xdt * to_end).astype(BF16)
    chunk_decay = jnp.exp(cs_last)
    block_mask = (lax.broadcasted_iota(I32, (MXU_DIM, heads_per_block * SSD_HEADDIM), 0) >> log2_t) == \
        (lax.broadcasted_iota(I32, (MXU_DIM, heads_per_block * SSD_HEADDIM), 1) >> log2_p)

    y_blocks = []
    for g in range(SSD_GROUPS):
        bg = b_mat[:, g * SSD_STATE:(g + 1) * SSD_STATE].astype(BF16)
        cg = c_mat[:, g * SSD_STATE:(g + 1) * SSD_STATE].astype(BF16)
        cb = _dot_nt(cg, jnp.concatenate([bg] * HEADS_PER_GROUP, axis=0))
        m = (cb * decay[:, g * group_lanes:(g + 1) * group_lanes]).astype(BF16)
        h0, h1 = g * HEADS_PER_GROUP, (g + 1) * HEADS_PER_GROUP
        state = ssm_ref[h0:h1].reshape(group_cols, SSD_STATE)
        y_off = _dot_nt(cg, state.astype(BF16)) * exp_cs[:, g * group_cols:(g + 1) * group_cols]
        for kb in range(blocks_per_group):
            c0 = g * group_cols + kb * heads_per_block * SSD_HEADDIM
            c1 = c0 + heads_per_block * SSD_HEADDIM
            xb = xdt[:, c0:c1]
            bd = jnp.where(block_mask, jnp.concatenate([xb] * heads_per_block, axis=0), 0.0).astype(BF16)
            y_diag = _dot(m[:, kb * MXU_DIM:(kb + 1) * MXU_DIM], bd)
            y_blocks.append(y_diag + y_off[:, c0 - g * group_cols:c1 - g * group_cols]
                            + xs[:, c0:c1] * dskip_ref[:, c0:c1])
        local = _dot_tn(x_end[:, g * group_cols:(g + 1) * group_cols], bg)
        for r in range(HEADS_PER_GROUP):
            hd = h0 + r
            ssm_ref[hd] = ssm_ref[hd] * chunk_decay[:, hd:hd + 1] + \
                local[r * SSD_HEADDIM:(r + 1) * SSD_HEADDIM]
    y = jnp.concatenate(y_blocks, axis=1)
    y_ref[...] = _rms(y * _silu(z_ref[...]), nw_ref[...]).astype(BF16)


def _ssd(xbc, dt, z, lw, bsz, length, conv0, ssm0):
    t = CHUNK if length % CHUNK == 0 else length
    nc = length // t
    n = bsz * length
    has_init = conv0 is not None
    row = lambda w: pl.BlockSpec((t, w), lambda b, c: (b * nc + c, 0))
    conv_spec = pl.BlockSpec((None, CONV_WIDTH - 1, CONV_DIM), lambda b, c: (b, 0, 0))
    ssm_spec = pl.BlockSpec((None, SSD_HEADS, SSD_HEADDIM, SSD_STATE), lambda b, c: (b, 0, 0, 0))
    in_specs = [row(CONV_DIM), row(LANES), row(SSD_INNER), _const_spec((CONV_WIDTH, CONV_DIM)),
                _const_spec((1, CONV_DIM)), _const_spec((1, SSD_HEADS)), _const_spec((1, SSD_INNER)),
                _const_spec((1, SSD_INNER))]
    args = [xbc, dt, z, lw["conv_w"], lw["conv_b"], lw["a_log"], lw["d_skip"], lw["ssd_norm"]]
    if has_init:
        in_specs += [conv_spec, ssm_spec]
        args += [conv0, ssm0]
    return pl.pallas_call(
        functools.partial(_ssd_body, t, has_init),
        grid=(bsz, nc),
        in_specs=in_specs,
        out_specs=[row(SSD_INNER), ssm_spec, conv_spec],
        out_shape=[jax.ShapeDtypeStruct((n, SSD_INNER), BF16),
                   jax.ShapeDtypeStruct((bsz, SSD_HEADS, SSD_HEADDIM, SSD_STATE), F32),
                   jax.ShapeDtypeStruct((bsz, CONV_WIDTH - 1, CONV_DIM), F32)],
        scratch_shapes=[pltpu.VMEM((SUBLANES, CONV_DIM), F32)],
        compiler_params=_params(2),
        name="ssd",
    )(*args)


def _attn_body(tq, q_ref, k_ref, v_ref, o_ref):
    qi = pl.program_id(2)
    q = q_ref[...]
    scale = QK_DIM ** -0.5

    def step(k_tile, v_tile, carry, mask):
        m_prev, l_prev, acc = carry
        s = _dot_nt(q, k_tile) * scale
        if mask is not None:
            s = jnp.where(mask, s, -jnp.inf)
        m_new = jnp.maximum(m_prev, jnp.max(s, axis=-1, keepdims=True))
        alpha = jnp.exp(m_prev - m_new)
        p = jnp.exp(s - m_new)
        l_new = alpha * l_prev + jnp.sum(p, axis=-1, keepdims=True)
        return m_new, l_new, alpha * acc + _dot(p.astype(BF16), v_tile)

    def full_tile(j, carry):
        start = pl.multiple_of(j * tq, tq)
        return step(k_ref[pl.ds(start, tq), :], v_ref[pl.ds(start, tq), :], carry, None)

    init = (jnp.full((tq, 1), -jnp.inf, F32), jnp.zeros((tq, 1), F32), jnp.zeros((tq, V_DIM), F32))
    carry = lax.fori_loop(0, qi, full_tile, init)
    start = pl.multiple_of(qi * tq, tq)
    rows = lax.broadcasted_iota(I32, (tq, tq), 0) >> LOG2_CHUNK
    cols = lax.broadcasted_iota(I32, (tq, tq), 1) >> LOG2_CHUNK
    _, l_fin, acc = step(k_ref[pl.ds(start, tq), :], v_ref[pl.ds(start, tq), :], carry, cols <= rows)
    o_ref[...] = (acc / l_fin).astype(BF16)


def _attention_prompt(q, k, v, bsz, length):
    tq = min(ATT_TILE, length)
    nq = length // tq
    n = bsz * length
    return pl.pallas_call(
        functools.partial(_attn_body, tq),
        grid=(bsz, MLA_HEADS, nq),
        in_specs=[pl.BlockSpec((None, tq, QK_DIM), lambda b, h, i: (h, b * nq + i, 0)),
                  pl.BlockSpec((None, length, QK_DIM), lambda b, h, i: (h, b, 0)),
                  pl.BlockSpec((None, length, V_DIM), lambda b, h, i: (h, b, 0))],
        out_specs=pl.BlockSpec((tq, V_DIM), lambda b, h, i: (b * nq + i, h)),
        out_shape=jax.ShapeDtypeStruct((n, MLA_INNER), BF16),
        compiler_params=_params(3),
        name="attn_prompt",
    )(q, k, v)


def _attn_cache_body(past, length, q_ref, kn_ref, vn_ref, kp_ref, vp_ref, o_ref):
    scale = QK_DIM ** -0.5
    q_chunk = (past + np.arange(length)) // CHUNK
    past_visible = (np.arange(past) // CHUNK)[None, :] <= q_chunk[:, None]
    new_visible = q_chunk[None, :] <= q_chunk[:, None]
    outs = []
    for hd in range(MLA_HEADS):
        q = q_ref[hd]
        s_p = _dot_nt(q, kp_ref[hd]) * scale
        s_n = _dot_nt(q, kn_ref[hd]) * scale
        if not past_visible.all():
            rows = (past + lax.broadcasted_iota(I32, (length, past), 0)) >> LOG2_CHUNK
            cols = lax.broadcasted_iota(I32, (length, past), 1) >> LOG2_CHUNK
            s_p = jnp.where(cols <= rows, s_p, -jnp.inf)
        if not new_visible.all():
            rows = (past + lax.broadcasted_iota(I32, (length, length), 0)) >> LOG2_CHUNK
            cols = (past + lax.broadcasted_iota(I32, (length, length), 1)) >> LOG2_CHUNK
            s_n = jnp.where(cols <= rows, s_n, -jnp.inf)
        m = jnp.maximum(jnp.max(s_p, axis=-1, keepdims=True), jnp.max(s_n, axis=-1, keepdims=True))
        p_p = jnp.exp(s_p - m)
        p_n = jnp.exp(s_n - m)
        denom = jnp.sum(p_p, axis=-1, keepdims=True) + jnp.sum(p_n, axis=-1, keepdims=True)
        o = _dot(p_p.astype(BF16), vp_ref[hd]) + _dot(p_n.astype(BF16), vn_ref[hd])
        outs.append(o / denom)
    o_ref[...] = jnp.concatenate(outs, axis=1).astype(BF16)


def _attention_cache(q, k_new, v_new, k_past, v_past, bsz, length, past):
    n = bsz * length
    new = lambda w: pl.BlockSpec((MLA_HEADS, length, w), lambda b: (0, b, 0))
    old = lambda w: pl.BlockSpec((MLA_HEADS, past, w), lambda b: (0, b, 0))
    return pl.pallas_call(
        functools.partial(_attn_cache_body, past, length),
        grid=(bsz,),
        in_specs=[new(QK_DIM), new(QK_DIM), new(V_DIM), old(QK_DIM), old(V_DIM)],
        out_specs=pl.BlockSpec((length, MLA_INNER), lambda b: (b, 0)),
        out_shape=jax.ShapeDtypeStruct((n, MLA_INNER), BF16),
        compiler_params=_params(1),
        name="attn_cache",
    )(q, k_new, v_new, k_past, v_past)


def _mix_out(x_ref, ys_ref, ym_ref, wo_ref):
    return x_ref[...] + _dot(ys_ref[...], wo_ref[:SSD_INNER, :]) + _dot(ym_ref[...], wo_ref[SSD_INNER:, :])


def _dense_body(final, x_ref, ys_ref, ym_ref, wo_ref, nw_ref, wg_ref, wu_ref, wd_ref, *rest):
    x1 = _mix_out(x_ref, ys_ref, ym_ref, wo_ref)
    h = _rms(x1, nw_ref[...]).astype(BF16)
    act = (_silu(_dot(h, wg_ref[...])) * _dot(h, wu_ref[...])).astype(BF16)
    x2 = x1 + _dot(act, wd_ref[...])
    if final:
        fw_ref, o_ref = rest
        o_ref[...] = _rms(x2, fw_ref[...])
    else:
        (o_ref,) = rest
        o_ref[...] = x2


def _resident(shape):
    zeros = (0,) * len(shape)
    return pl.BlockSpec(shape, lambda *_: zeros, pipeline_mode=pl.Buffered(1))


def _dense_layer(x, ys, ym, lw, fw, final_w, tm):
    n = x.shape[0]
    row = lambda w: pl.BlockSpec((tm, w), lambda i: (i, 0))
    d_ff = fw["w_gate"].shape[1]
    in_specs = [row(D_MODEL), row(SSD_INNER), row(MLA_INNER), _resident((SSD_INNER + MLA_INNER, D_MODEL)),
                _const_spec((1, D_MODEL)), _resident((D_MODEL, d_ff)), _resident((D_MODEL, d_ff)),
                _resident((d_ff, D_MODEL))]
    args = [x, ys, ym, lw["w_out"], lw["norm_ffn"], fw["w_gate"], fw["w_up"], fw["w_down"]]
    if final_w is not None:
        in_specs.append(_const_spec((1, D_MODEL)))
        args.append(final_w)
    return pl.pallas_call(
        functools.partial(_dense_body, final_w is not None),
        grid=(n // tm,),
        in_specs=in_specs,
        out_specs=row(D_MODEL),
        out_shape=jax.ShapeDtypeStruct((n, D_MODEL), F32),
        compiler_params=_params(1),
        name="mixout_dense",
    )(*args)


def _router_body(tm, x_ref, ys_ref, ym_ref, wo_ref, nw_ref, wr_ref,
                 x1_ref, h_ref, ri_ref, rf_ref, cnt_ref):
    @pl.when(pl.program_id(0) == 0)
    def _():
        cnt_ref[...] = jnp.zeros_like(cnt_ref)

    x1 = _mix_out(x_ref, ys_ref, ym_ref, wo_ref)
    x1_ref[...] = x1
    h = _rms(x1, nw_ref[...])
    _store_row_tiles(h_ref, h)
    lane_i =lax.broadcasted_iota(I32, (tm, LANES), 1)
    lane = lane_i.astype(F32)
    logits = jnp.where(lane_i < N_EXPERTS, _dot_hi(h, wr_ref[...]), -jnp.inf)
    top0 = jnp.max(logits, axis=-1, keepdims=True)
    e0 = jnp.min(jnp.where(logits == top0, lane, float(LANES)), axis=-1, keepdims=True)
    rest = jnp.where(lane == e0, -jnp.inf, logits)
    top1 = jnp.max(rest, axis=-1, keepdims=True)
    e1 = jnp.min(jnp.where(rest == top1, lane, float(LANES)), axis=-1, keepdims=True)
    w1 = jnp.exp(top1 - top0)
    g0 = 1.0 / (1.0 + w1)
    g1 = w1 / (1.0 + w1)
    onehot = jnp.where((lane == e0) | (lane == e1), 1.0, 0.0)
    strict = (lax.broadcasted_iota(I32, (tm, tm), 0) > lax.broadcasted_iota(I32, (tm, tm), 1)).astype(BF16)
    before = _dot(strict, onehot.astype(BF16)) + cnt_ref[0:1, :]
    pos0 = jnp.sum(jnp.where(lane == e0, before, 0.0), axis=-1, keepdims=True)
    pos1 = jnp.sum(jnp.where(lane == e1, before, 0.0), axis=-1, keepdims=True)
    cnt_ref[...] = cnt_ref[...] + jnp.sum(onehot, axis=0, keepdims=True)
    packed = jnp.where(lane_i == 0, e0, jnp.where(lane_i == 1, e1, jnp.where(lane_i == 2, pos0, pos1)))
    ri_ref[...] = packed.astype(I32)
    rf_ref[...] = jnp.where(lane_i == 0, g0, g1)


def _router_layer(x, ys, ym, lw, mw, tm):
    n = x.shape[0]
    row = lambda w: pl.BlockSpec((tm, w), lambda i: (i, 0))
    return pl.pallas_call(
        functools.partial(_router_body, tm),
        grid=(n // tm,),
        in_specs=[row(D_MODEL), row(SSD_INNER), row(MLA_INNER), _const_spec((SSD_INNER + MLA_INNER, D_MODEL)),
                  _const_spec((1, D_MODEL)), _const_spec((D_MODEL, LANES))],
        out_specs=[row(D_MODEL), _row_tile_spec(tm), row(LANES), row(LANES), _const_spec((SUBLANES, LANES))],
        out_shape=[jax.ShapeDtypeStruct((n, D_MODEL), F32), jax.ShapeDtypeStruct((n,) + ROW_TILE, F32),
                   jax.ShapeDtypeStruct((n, LANES), I32), jax.ShapeDtypeStruct((n, LANES), F32),
                   jax.ShapeDtypeStruct((SUBLANES, LANES), F32)],
        compiler_params=_params(1),
        name="mixout_router",
    )(x, ys, ym, lw["w_out"], lw["norm_ffn"], mw["w_router"])


def _dispatch_body(tm, bounds_ref, dest_ref, h_ref, xs_ref, dest_smem, zero_ref, sem):
    base = pl.program_id(0) * tm

    @pl.when(pl.program_id(0) == 0)
    def _():
        zero_ref[...] = jnp.zeros_like(zero_ref)

        def fill(start):
            copy = pltpu.make_async_copy(zero_ref, xs_ref.at[pl.ds(pl.multiple_of(start, MOE_TILE), MOE_TILE)], sem)
            copy.start()
            copy.wait()

        for e in range(N_EXPERTS):
            pl.when(bounds_ref[e + 1] > bounds_ref[e])(lambda e=e: fill(bounds_ref[e + 1] - MOE_TILE))
            unused = bounds_ref[N_EXPERTS] + e * MOE_TILE
            pl.when(unused < xs_ref.shape[0])(lambda unused=unused: fill(unused))

    pltpu.sync_copy(dest_ref.at[0], dest_smem)

    def start(j, c):
        for k in range(TOP_K):
            pltpu.make_async_copy(h_ref.at[base + j], xs_ref.at[dest_smem[k, j]], sem).start()
        return c

    lax.fori_loop(0, tm, start, 0)
    for k in range(TOP_K):
        pltpu.make_async_copy(h_ref.at[pl.ds(0, tm)], xs_ref.at[pl.ds(0, tm)], sem).wait()


def _dispatch(h, dest, bounds, rows, tm):
    n = h.shape[0]
    return pl.pallas_call(
        functools.partial(_dispatch_body, tm),
        grid_spec=pltpu.PrefetchScalarGridSpec(
            num_scalar_prefetch=1,
            grid=(n // tm,),
            in_specs=[pl.BlockSpec((1, TOP_K, tm), lambda i, b: (i, 0, 0)), pl.BlockSpec(memory_space=pl.ANY)],
            out_specs=pl.BlockSpec(memory_space=pl.ANY),
            scratch_shapes=[pltpu.SMEM((TOP_K, tm), I32), pltpu.VMEM((MOE_TILE,) + ROW_TILE, F32),
                            pltpu.SemaphoreType.DMA],
        ),
        out_shape=jax.ShapeDtypeStruct((rows,) + ROW_TILE, F32),
        compiler_params=_params(1),
        name="dispatch",
    )(bounds, dest, h)


def _experts_body(tm, te_ref, nv_ref, x_ref, wg_ref, wu_ref, wd_ref, y_ref):
    valid = nv_ref[pl.program_id(0)]

    @pl.when(valid > 0)
    def _():
        x = _load_row_tiles(x_ref).astype(BF16)
        act = (_silu(_dot(x, wg_ref[...])) * _dot(x, wu_ref[...])).astype(BF16)
        _store_row_tiles(y_ref, _dot(act, wd_ref[...]))

    @pl.when(valid == 0)
    def _():
        y_ref[...] = jnp.zeros_like(y_ref)


def _experts(xs, tile_expert, tile_valid, mw, tm):
    rows = xs.shape[0]
    d_ff = mw["w_gate"].shape[2]
    wspec = lambda a, b: pl.BlockSpec((None, a, b), lambda i, te, nv: (te[i], 0, 0),
                                      pipeline_mode=pl.Buffered(1))
    return pl.pallas_call(
        functools.partial(_experts_body, tm),
        grid_spec=pltpu.PrefetchScalarGridSpec(
            num_scalar_prefetch=2,
            grid=(rows // tm,),
            in_specs=[_row_tile_spec(tm), wspec(D_MODEL, d_ff), wspec(D_MODEL, d_ff), wspec(d_ff, D_MODEL)],
            out_specs=_row_tile_spec(tm),
        ),
        out_shape=jax.ShapeDtypeStruct((rows,) + ROW_TILE, F32),
        compiler_params=_params(1),
        name="experts",
    )(tile_expert, tile_valid, xs, mw["w_gate"], mw["w_up"], mw["w_down"])


def _combine_body(tm, final, dest_ref, dnext_ref, x1_ref, rf_ref, ys_ref, *rest):
    if final:
        fw_ref, o_ref, buf_ref, dest_smem, sems = rest
    else:
        o_ref, buf_ref, dest_smem, sems = rest
    i = pl.program_id(0)
    nsteps = pl.num_programs(0)
    slot = i % 2

    def gather(src_ref, s):
        pltpu.sync_copy(src_ref.at[0], dest_smem)

        def start(j, c):
            for k in range(TOP_K):
                pltpu.make_async_copy(ys_ref.at[dest_smem[k, j]], buf_ref.at[s, k, j], sems.at[s]).start()
            return c

        lax.fori_loop(0, tm, start, 0)

    @pl.when(i == 0)
    def _():
        gather(dest_ref, 0)

    for k in range(TOP_K):
        pltpu.make_async_copy(ys_ref.at[pl.ds(0, tm)], buf_ref.at[slot, k], sems.at[slot]).wait()

    @pl.when(i + 1 < nsteps)
    def _():
        gather(dnext_ref, 1 - slot)

    gates = rf_ref[...]
    out = x1_ref[...] + gates[:, 0:1] * _load_row_tiles(buf_ref, slot, 0) + \
        gates[:, 1:2] * _load_row_tiles(buf_ref, slot, 1)
    if final:
        out = _rms(out, fw_ref[...])
    o_ref[...] = out


def _combine(dest, x1, rf, ys, final_w, tm):
    n = x1.shape[0]
    nt = n // tm
    row = lambda w: pl.BlockSpec((tm, w), lambda i: (i, 0))
    in_specs = [pl.BlockSpec((1, TOP_K, tm), lambda i: (i, 0, 0)),
                pl.BlockSpec((1, TOP_K, tm), lambda i: (jnp.minimum(i + 1, nt - 1), 0, 0)),
                row(D_MODEL), row(LANES), pl.BlockSpec(memory_space=pl.ANY)]
    args = [dest, dest, x1, rf, ys]
    if final_w is not None:
        in_specs.append(_const_spec((1, D_MODEL)))
        args.append(final_w)
    return pl.pallas_call(
        functools.partial(_combine_body, tm, final_w is not None),
        grid=(nt,),
        in_specs=in_specs,
        out_specs=row(D_MODEL),
        out_shape=jax.ShapeDtypeStruct((n, D_MODEL), F32),
        scratch_shapes=[pltpu.VMEM((2, TOP_K, tm) + ROW_TILE, F32), pltpu.SMEM((TOP_K, tm), I32),
                        pltpu.SemaphoreType.DMA((2,))],
        compiler_params=_params(1),
        name="combine",
    )(*args)


def _moe_layer(x, ys, ym, lw, mw, final_w):
    n = x.shape[0]
    tm = min(TOKEN_TILE, n)
    x1, h, ri, rf, counts = _router_layer(x, ys, ym, lw, mw, tm)
    counts = counts[0, :N_EXPERTS].astype(I32)
    padded = (counts + MOE_TILE - 1) // MOE_TILE * MOE_TILE
    ends = jnp.cumsum(padded)
    offsets = ends - padded
    rows = (n * TOP_K + N_EXPERTS * (MOE_TILE - 1)) // MOE_TILE * MOE_TILE
    tile_start = jnp.arange(rows // MOE_TILE, dtype=I32) * MOE_TILE
    tile_expert = jnp.minimum(jnp.sum(tile_start[:, None] >= ends[None, :], axis=1), N_EXPERTS - 1).astype(I32)
    group_end = (offsets + counts)[tile_expert]
    tile_valid = jnp.where(tile_start < ends[-1], jnp.clip(group_end - tile_start, 0, MOE_TILE), 0).astype(I32)
    dest =jnp.stack([offsets[ri[:, 0]] + ri[:, 2], offsets[ri[:, 1]] + ri[:, 3]])

    def tiles(t):
        return dest.reshape(TOP_K, n // t, t).transpose(1, 0, 2)

    td = min(DISPATCH_TILE, n)
    tc = min(COMBINE_TILE, n)
    bounds = jnp.concatenate([jnp.zeros((1,), I32), ends.astype(I32)])
    xs = _dispatch(h, tiles(td), bounds, rows, td)
    y_sorted = _experts(xs, tile_expert, tile_valid, mw, MOE_TILE)
    return _combine(tiles(tc), x1, rf, y_sorted, final_w, tc)


def _prep_layer(layer, norm_mix, w_in, conv_w, conv_b, dt_bias, a_log, d_skip, ssd_norm, q_norm, w_q_up,
                kv_norm, w_kv_up, w_out, norm_ffn):
    w = w_in[layer]
    splits = np.cumsum([SSD_INNER, CONV_DIM, SSD_HEADS, Q_LORA, KV_LORA, ROPE_DIM])[:-1].tolist()
    w_z, w_xbc, w_dt, w_cq, w_ckv, w_kr = jnp.split(w, splits, axis=1)
    half = ROPE_DIM // 2
    w_krs = jnp.concatenate([w_kr[:, half:], w_kr[:, :half]], axis=1)
    pad = jnp.zeros((D_MODEL, LANES - SSD_HEADS), F32)
    w_cat = jnp.concatenate([w_z, w_xbc, w_cq, w_ckv, w_kr, w_krs, w_dt, pad], axis=1).astype(BF16)
    wq = w_q_up[layer].reshape(Q_LORA, MLA_HEADS, QK_DIM)
    wq = jnp.concatenate([wq, wq[:, :, NOPE_DIM + half:], wq[:, :, NOPE_DIM:NOPE_DIM + half]], axis=2)
    row = lambda v: v.reshape(1, -1).astype(F32)
    return {
        "norm_mix": row(norm_mix[layer]), "w_in": w_cat,
        "q_norm": row(q_norm[layer]), "w_q": wq.reshape(Q_LORA, MLA_HEADS * Q_HEAD_COLS).astype(BF16),
        "kv_norm": row(kv_norm[layer]), "w_kv": w_kv_up[layer].astype(BF16),
        "dt_bias": jnp.pad(row(dt_bias[layer]), ((0, 0), (0, LANES - SSD_HEADS))),
        "conv_w": conv_w[layer], "conv_b": row(conv_b[layer]), "a_log": row(a_log[layer]),
        "d_skip": row(jnp.repeat(d_skip[layer], SSD_HEADDIM)), "ssd_norm": row(ssd_norm[layer]),
        "w_out": w_out[layer].astype(BF16), "norm_ffn": row(norm_ffn[layer]),
    }


def _rope_tables(pos, tm):
    half = ROPE_DIM // 2
    inv_freq = ROPE_THETA ** (-jnp.arange(half, dtype=F32) / half)
    ang = pos.astype(F32)[:, None] * inv_freq[None, :]
    cos, sin = jnp.cos(ang), jnp.sin(ang)
    cos_tab = jnp.concatenate([cos, cos], axis=1)
    sin_tab = jnp.concatenate([-sin, sin], axis=1)
    reps = max(1, tm // pos.shape[0])
    return jnp.tile(cos_tab, (reps, 1)), jnp.tile(sin_tab, (reps, 1))


def _mixing(x, bsz, length, lw, tables, conv0, ssm0, past_k, past_v, past_len):
    n = bsz * length
    tm = min(TOKEN_TILE, n)
    z, xbc, dt, c_kv, k_rope, q, k, v = _inproj(x, lw, tables[0], tables[1], tm)
    ys, new_ssm, new_conv = _ssd(xbc, dt, z, lw, bsz, length, conv0, ssm0)
    if past_k is None:
        ym = _attention_prompt(q, k, v, bsz, length)
    else:
        ym = _attention_cache(q, k, v, past_k, past_v, bsz, length, past_len)
    return ys, ym, c_kv, k_rope, new_ssm, new_conv


def kernel(x_prompt, x_sample, cache_kv_latent, cache_k_rope, state_ssm, state_conv, norm_mix, w_in, conv_w,
           conv_b, dt_bias, a_log, d_skip, ssd_norm, q_norm, w_q_up, kv_norm, w_kv_up, w_out, norm_ffn,
           ffn_w_gate, ffn_w_up, ffn_w_down, moe_router, moe_w_gate, moe_w_up, moe_w_down, final_norm):
    bp, sp = x_prompt.shape[:2]
    bs, ss = x_sample.shape[:2]
    depth = w_in.shape[0]
    past = cache_kv_latent.shape[2]
    tabs_p = _rope_tables(jnp.arange(sp, dtype=I32), min(TOKEN_TILE, bp * sp))
    tabs_s = _rope_tables(past + jnp.arange(ss, dtype=I32), min(TOKEN_TILE, bs * ss))
    final_w = final_norm.reshape(1, D_MODEL)
    hp = x_prompt.reshape(bp * sp, D_MODEL)
    hs = x_sample.reshape(bs * ss, D_MODEL)
    outs_p, outs_s = [], []
    for layer in range(depth):
        lw = _prep_layer(layer, norm_mix, w_in, conv_w, conv_b, dt_bias, a_log, d_skip, ssd_norm, q_norm,
                         w_q_up, kv_norm, w_kv_up, w_out, norm_ffn)
        last = final_w if layer == depth - 1 else None
        past_k, past_v = _pastkv(cache_kv_latent[layer].reshape(bs * past, KV_LORA),
                                 cache_k_rope[layer].reshape(bs * past, ROPE_DIM), lw["w_kv"],
                                 min(TOKEN_TILE, bs * past))
        mix_p = _mixing(hp, bp, sp, lw, tabs_p, None, None, None, None, 0)
        mix_s = _mixing(hs, bs, ss, lw, tabs_s, state_conv[layer], state_ssm[layer], past_k, past_v, past)
        outs_p.append(mix_p[2:])
        outs_s.append(mix_s[2:])
        idx = layer // 2
        if layer % 2 == 0:
            fw = {"w_gate": ffn_w_gate[idx].astype(BF16), "w_up": ffn_w_up[idx].astype(BF16),
                  "w_down": ffn_w_down[idx].astype(BF16)}
            hp = _dense_layer(hp, mix_p[0], mix_p[1], lw, fw, last, min(TOKEN_TILE, bp * sp))
            hs = _dense_layer(hs, mix_s[0], mix_s[1], lw, fw, last, min(TOKEN_TILE, bs * ss))
        else:
            mw = {"w_router": jnp.pad(moe_router[idx].astype(F32), ((0, 0), (0, LANES - N_EXPERTS))),
                  "w_gate": moe_w_gate[idx].astype(BF16), "w_up": moe_w_up[idx].astype(BF16),
                  "w_down": moe_w_down[idx].astype(BF16)}
            hp = _moe_layer(hp, mix_p[0], mix_p[1], lw, mw, last)
            hs = _moe_layer(hs, mix_s[0], mix_s[1], lw, mw, last)

    def stack(outs, j, shape):
        return jnp.stack([o[j].reshape(shape) for o in outs])

    return (hp.reshape(bp, sp, D_MODEL), hs.reshape(bs, ss, D_MODEL),
            stack(outs_p, 0, (bp, sp, KV_LORA)), stack(outs_p, 1, (bp, sp, ROPE_DIM)),
            stack(outs_p, 2, (bp, SSD_HEADS, SSD_HEADDIM, SSD_STATE)),
            stack(outs_p, 3, (bp, CONV_WIDTH - 1, CONV_DIM)),
            stack(outs_s, 0, (bs, ss, KV_LORA)), stack(outs_s, 1, (bs, ss, ROPE_DIM)),
            stack(outs_s, 2, (bs, SSD_HEADS, SSD_HEADDIM, SSD_STATE)),
            stack(outs_s, 3, (bs, CONV_WIDTH - 1, CONV_DIM)))
```

```python
import functools
import math

import jax
import jax.numpy as jnp
import numpy as np
from jax import lax
from jax.experimental import pallas as pl
from jax.experimental.pallas import tpu as pltpu

F32 = jnp.float32
BF16 = jnp.bfloat16
I32 = jnp.int32
HI = lax.Precision.HIGHEST

D_MODEL = 1024
CHUNK = 64
LOG2_CHUNK = 6
SSD_HEADS = 16
SSD_HEADDIM = 64
SSD_INNER = SSD_HEADS * SSD_HEADDIM
SSD_GROUPS = 2
HEADS_PER_GROUP = SSD_HEADS // SSD_GROUPS
SSD_STATE = 128
CONV_WIDTH = 4
CONV_DIM = SSD_INNER + 2 * SSD_GROUPS * SSD_STATE
MLA_HEADS = 8
Q_LORA = 384
KV_LORA = 256
NOPE_DIM = 128
ROPE_DIM = 64
V_DIM = 128
QK_DIM = NOPE_DIM + ROPE_DIM
ROPE_THETA = 10000.0
MLA_INNER = MLA_HEADS * V_DIM
N_EXPERTS = 8
TOP_K = 2
EPS = 1e-6

LANES = 128
SUBLANES = 8
MXU_DIM = 256
VMEM_LIMIT = 56 * 1024 * 1024

OFF_Z = 0
OFF_XBC = OFF_Z + SSD_INNER
OFF_CQ = OFF_XBC + CONV_DIM
OFF_CKV = OFF_CQ + Q_LORA
OFF_KR = OFF_CKV + KV_LORA
OFF_KRS = OFF_KR + ROPE_DIM
OFF_DT = OFF_KRS + ROPE_DIM
IN_COLS = OFF_DT + LANES
Q_HEAD_COLS = NOPE_DIM + 2 * ROPE_DIM

TOKEN_TILE = 512
ATT_TILE = 512
MOE_TILE = 256
DISPATCH_TILE = 512
COMBINE_TILE = 256


def _params(n_axes):
    return pltpu.CompilerParams(dimension_semantics=("arbitrary",) * n_axes,
                                vmem_limit_bytes=VMEM_LIMIT)


def _const_spec(shape):
    zeros = (0,) * len(shape)
    return pl.BlockSpec(shape, lambda *_: zeros)


def _rms(x, w):
    return x * lax.rsqrt(jnp.mean(x * x, axis=-1, keepdims=True) + EPS) * w


def _silu(x):
    return x * (1.0 / (1.0 + jnp.exp(-x)))


def _dot(a, b):
    return jnp.dot(a, b, preferred_element_type=F32)


def _dot_nt(a, b):
    return lax.dot_general(a, b, (((1,), (1,)), ((), ())), preferred_element_type=F32)


def _dot_tn(a, b):
    return lax.dot_general(a, b, (((0,), (0,)), ((), ())), preferred_element_type=F32)


def _dot_hi(a, b):
    return jnp.dot(a, b, preferred_element_type=F32, precision=HI)


def _split3(x):
    hi = x.astype(BF16)
    rest = x - hi.astype(F32)
    mid = rest.astype(BF16)
    lo = (rest - mid.astype(F32)).astype(BF16)
    return jnp.concatenate([hi, mid, lo], axis=1)


ROW_TILE = (D_MODEL // LANES, LANES)


def _row_tile_spec(tm):
    return pl.BlockSpec((tm,) + ROW_TILE, lambda i, *_: (i, 0, 0))


def _store_row_tiles(ref, x):
    for s in range(ROW_TILE[0]):
        ref[:, s, :] = x[:, s * LANES:(s + 1) * LANES]


def _load_row_tiles(ref, *lead):
    return jnp.concatenate([ref[lead + (slice(None), s, slice(None))] for s in range(ROW_TILE[0])], axis=1)


def _inproj_body(x_ref, nw_ref, win_ref, qnw_ref, wq_ref, kvnw_ref, wkv_ref, cos_ref, sin_ref, dtb_ref,
                 z_ref, xbc_ref, dt_ref, ckv_ref, kr_ref, q_ref, k_ref, v_ref):
    h = _rms(x_ref[...], nw_ref[...]).astype(BF16)
    proj = _dot(h, win_ref[...])
    z_ref[...] = proj[:, OFF_Z:OFF_XBC]
    xbc_ref[...] = proj[:, OFF_XBC:OFF_CQ]
    pre = proj[:, OFF_DT:IN_COLS] + dtb_ref[...]
    dt_ref[...] = jnp.maximum(pre, 0.0) + jnp.log(1.0 + jnp.exp(-jnp.abs(pre)))
    cos = cos_ref[...]
    sin = sin_ref[...]
    k_rope = proj[:, OFF_KR:OFF_KRS] * cos + proj[:, OFF_KRS:OFF_DT] * sin
    kr_ref[...] = k_rope
    c_kv = _rms(proj[:, OFF_CKV:OFF_KR], kvnw_ref[...])
    ckv_ref[...] = c_kv
    qn = _rms(proj[:, OFF_CQ:OFF_CKV], qnw_ref[...]).astype(BF16)
    q_all = _dot(qn, wq_ref[...])
    kv_all = _dot(c_kv.astype(BF16), wkv_ref[...])
    k_rope_b = k_rope.astype(BF16)
    for hd in range(MLA_HEADS):
        qb = q_all[:, hd * Q_HEAD_COLS:(hd + 1) * Q_HEAD_COLS]
        q_rope = qb[:, NOPE_DIM:QK_DIM] * cos + qb[:, QK_DIM:Q_HEAD_COLS] * sin
        q_ref[hd] = jnp.concatenate([qb[:, :NOPE_DIM], q_rope], axis=1).astype(BF16)
        kvb = kv_all[:, hd * (NOPE_DIM + V_DIM):(hd + 1) * (NOPE_DIM + V_DIM)]
        k_ref[hd] = jnp.concatenate([kvb[:, :NOPE_DIM].astype(BF16), k_rope_b], axis=1)
        v_ref[hd] = kvb[:, NOPE_DIM:].astype(BF16)


def _inproj(x, lw, cos_tab, sin_tab, tm):
    n = x.shape[0]
    nt = n // tm
    tab_blocks = cos_tab.shape[0] // tm
    row = lambda w: pl.BlockSpec((tm, w), lambda i: (i, 0))
    head = lambda w: pl.BlockSpec((MLA_HEADS, tm, w), lambda i: (0, i, 0))
    tab = pl.BlockSpec((tm, ROPE_DIM), lambda i: (i % tab_blocks, 0))
    return pl.pallas_call(
        _inproj_body,
        grid=(nt,),
        in_specs=[row(D_MODEL), _const_spec((1, D_MODEL)), _const_spec((D_MODEL, IN_COLS)),
                  _const_spec((1, Q_LORA)), _const_spec((Q_LORA, MLA_HEADS * Q_HEAD_COLS)),
                  _const_spec((1, KV_LORA)), _const_spec((KV_LORA, MLA_HEADS * (NOPE_DIM + V_DIM))),
                  tab, tab, _const_spec((1, LANES))],
        out_specs=[row(SSD_INNER), row(CONV_DIM), row(LANES), row(KV_LORA), row(ROPE_DIM),
                   head(QK_DIM), head(QK_DIM), head(V_DIM)],
        out_shape=[jax.ShapeDtypeStruct((n, SSD_INNER), F32), jax.ShapeDtypeStruct((n, CONV_DIM), F32),
                   jax.ShapeDtypeStruct((n, LANES), F32), jax.ShapeDtypeStruct((n, KV_LORA), F32),
                   jax.ShapeDtypeStruct((n, ROPE_DIM), F32),
                   jax.ShapeDtypeStruct((MLA_HEADS, n, QK_DIM), BF16),
                   jax.ShapeDtypeStruct((MLA_HEADS, n, QK_DIM), BF16),
                   jax.ShapeDtypeStruct((MLA_HEADS, n, V_DIM), BF16)],
        compiler_params=_params(1),
        name="inproj",
    )(x, lw["norm_mix"], lw["w_in"], lw["q_norm"], lw["w_q"], lw["kv_norm"], lw["w_kv"],
      cos_tab, sin_tab, lw["dt_bias"])


def _pastkv_body(lat_ref, kr_ref, wkv_ref, k_ref, v_ref):
    kv_all = _dot(lat_ref[...].astype(BF16), wkv_ref[...])
    k_rope_b = kr_ref[...].astype(BF16)
    for hd in range(MLA_HEADS):
        kvb = kv_all[:, hd * (NOPE_DIM + V_DIM):(hd + 1) * (NOPE_DIM + V_DIM)]
        k_ref[hd] = jnp.concatenate([kvb[:, :NOPE_DIM].astype(BF16), k_rope_b], axis=1)
        v_ref[hd] = kvb[:, NOPE_DIM:].astype(BF16)


def _pastkv(lat, kr, w_kv, tm):
    n = lat.shape[0]
    row = lambda w: pl.BlockSpec((tm, w), lambda i: (i, 0))
    head = lambda w: pl.BlockSpec((MLA_HEADS, tm, w), lambda i: (0, i, 0))
    return pl.pallas_call(
        _pastkv_body,
        grid=(n // tm,),
        in_specs=[row(KV_LORA), row(ROPE_DIM), _const_spec(w_kv.shape)],
        out_specs=[head(QK_DIM), head(V_DIM)],
        out_shape=[jax.ShapeDtypeStruct((MLA_HEADS, n, QK_DIM), BF16),
                   jax.ShapeDtypeStruct((MLA_HEADS, n, V_DIM), BF16)],
        compiler_params=_params(1),
        name="pastkv",
    )(lat, kr, w_kv)


def _ssd_body(t, has_init, *refs):
    if has_init:
        (xbc_ref, dt_ref, z_ref, cw_ref, cb_ref, alog_ref, dskip_ref, nw_ref, conv0_ref, ssm0_ref,
         y_ref, ssm_ref, conv_ref, hist_ref) = refs
    else:
        (xbc_ref, dt_ref, z_ref, cw_ref, cb_ref, alog_ref, dskip_ref, nw_ref,
         y_ref, ssm_ref, conv_ref, hist_ref) = refs
    hist_rows = SUBLANES
    first = hist_rows - (CONV_WIDTH - 1)
    group_lanes = HEADS_PER_GROUP * t
    group_cols = HEADS_PER_GROUP * SSD_HEADDIM
    heads_per_block = MXU_DIM // t
    blocks_per_group = group_lanes // MXU_DIM
    log2_t = int(math.log2(t))
    log2_p = int(math.log2(SSD_HEADDIM))

    @pl.when(pl.program_id(1) == 0)
    def _():
        hist_ref[...] = jnp.zeros_like(hist_ref)
        if has_init:
            hist_ref[first:hist_rows, :] = conv0_ref[...]
            ssm_ref[...] = ssm0_ref[...]
        else:
            ssm_ref[...] = jnp.zeros_like(ssm_ref)

    x_raw = xbc_ref[...]
    hist_ref[hist_rows:, :] = x_raw
    conv = cb_ref[...]
    for k in range(CONV_WIDTH):
        conv = conv + hist_ref[first + k:first + k + t, :] * cw_ref[k:k + 1, :]
    hist_ref[first:hist_rows, :] = x_raw[t - (CONV_WIDTH - 1):t]
    conv_ref[...] = x_raw[t - (CONV_WIDTH - 1):t]
    act = _silu(conv)
    xs = act[:, :SSD_INNER]
    b_mat = act[:, SSD_INNER:SSD_INNER + SSD_GROUPS * SSD_STATE]
    c_mat = act[:, SSD_INNER + SSD_GROUPS * SSD_STATE:]

    dt = dt_ref[:, :SSD_HEADS]
    da = dt * (-jnp.exp(alog_ref[...]))
    tril = (lax.broadcasted_iota(I32, (t, t), 0) >= lax.broadcasted_iota(I32, (t, t), 1)).astype(BF16)
    cs3 = _dot(tril, _split3(da))
    cs = cs3[:, :SSD_HEADS] + cs3[:, SSD_HEADS:2 * SSD_HEADS] + cs3[:, 2 * SSD_HEADS:]
    cs_last = cs[t - 1:t, :]

    def expander(width, shift):
        rows = lax.broadcasted_iota(I32, (3 * SSD_HEADS, width), 0)
        return ((lax.broadcasted_iota(I32, (3 * SSD_HEADS, width), 1) >> shift) == (rows & (SSD_HEADS - 1))
                ).astype(BF16)

    per_p = _dot(_split3(jnp.concatenate([cs, dt], axis=0)), expander(SSD_INNER, log2_p))
    cs_p, dt_p = per_p[:t], per_p[t:]
    cs_t = cs_p if t == SSD_HEADDIM else _dot(_split3(cs), expander(SSD_HEADS * t, log2_t))
    lane = lax.broadcasted_iota(I32, (t, SSD_HEADS * t), 1) & (t - 1)
    rowi = lax.broadcasted_iota(I32, (t, SSD_HEADS * t), 0)
    cs_src = jnp.sum(jnp.where(lane == rowi, cs_t, 0.0), axis=0, keepdims=True)
    decay = jnp.exp(jnp.where(rowi >= lane, cs_t - cs_src, -jnp.inf))
    exp_cs = jnp.exp(cs_p)
    to_end = jnp.exp(cs_p[t - 1:t, :] - cs_p)
    xdt = xs * dt_p
    x_end = (xdt * to_end).astype(BF16)
    chunk_decay = jnp.exp(cs_last)
    block_mask = (lax.broadcasted_iota(I32, (MXU_DIM, heads_per_block * SSD_HEADDIM), 0) >> log2_t) == \
        (lax.broadcasted_iota(I32, (MXU_DIM, heads_per_block * SSD_HEADDIM), 1) >> log2_p)

    y_blocks = []
    for g in range(SSD_GROUPS):
        bg = b_mat[:, g * SSD_STATE:(g + 1) * SSD_STATE].astype(BF16)
        cg = c_mat[:, g * SSD_STATE:(g + 1) * SSD_STATE].astype(BF16)
        cb = _dot_nt(cg, jnp.concatenate([bg] * HEADS_PER_GROUP, axis=0))
        m = (cb * decay[:, g * group_lanes:(g + 1) * group_lanes]).astype(BF16)
        h0, h1 = g * HEADS_PER_GROUP, (g + 1) * HEADS_PER_GROUP
        state = ssm_ref[h0:h1].reshape(group_cols, SSD_STATE)
        y_off = _dot_nt(cg, state.astype(BF16)) * exp_cs[:, g * group_cols:(g + 1) * group_cols]
        for kb in range(blocks_per_group):
            c0 = g * group_cols + kb * heads_per_block * SSD_HEADDIM
            c1 = c0 + heads_per_block * SSD_HEADDIM
            xb = xdt[:, c0:c1]
            bd = jnp.where(block_mask, jnp.concatenate([xb] * heads_per_block, axis=0), 0.0).astype(BF16)
            y_diag = _dot(m[:, kb * MXU_DIM:(kb + 1) * MXU_DIM], bd)
            y_blocks.append(y_diag + y_off[:, c0 - g * group_cols:c1 - g * group_cols]
                            + xs[:, c0:c1] * dskip_ref[:, c0:c1])
        local = _dot_tn(x_end[:, g * group_cols:(g + 1) * group_cols], bg)
        for r in range(HEADS_PER_GROUP):
            hd = h0 + r
            ssm_ref[hd] = ssm_ref[hd] * chunk_decay[:, hd:hd + 1] + \
                local[r * SSD_HEADDIM:(r + 1) * SSD_HEADDIM]
    y = jnp.concatenate(y_blocks, axis=1)
    y_ref[...] = _rms(y * _silu(z_ref[...]), nw_ref[...]).astype(BF16)


def _ssd(xbc, dt, z, lw, bsz, length, conv0, ssm0):
    t = CHUNK if length % CHUNK == 0 else length
    nc = length // t
    n = bsz * length
    has_init = conv0 is not None
    row = lambda w: pl.BlockSpec((t, w), lambda b, c: (b * nc + c, 0))
    conv_spec = pl.BlockSpec((None, CONV_WIDTH - 1, CONV_DIM), lambda b, c: (b, 0, 0))
    ssm_spec = pl.BlockSpec((None, SSD_HEADS, SSD_HEADDIM, SSD_STATE), lambda b, c: (b, 0, 0, 0))
    in_specs = [row(CONV_DIM), row(LANES), row(SSD_INNER), _const_spec((CONV_WIDTH, CONV_DIM)),
                _const_spec((1, CONV_DIM)), _const_spec((1, SSD_HEADS)), _const_spec((1, SSD_INNER)),
                _const_spec((1, SSD_INNER))]
    args = [xbc, dt, z, lw["conv_w"], lw["conv_b"], lw["a_log"], lw["d_skip"], lw["ssd_norm"]]
    if has_init:
        in_specs += [conv_spec, ssm_spec]
        args += [conv0, ssm0]
    return pl.pallas_call(
        functools.partial(_ssd_body, t, has_init),
        grid=(bsz, nc),
        in_specs=in_specs,
        out_specs=[row(SSD_INNER), ssm_spec, conv_spec],
        out_shape=[jax.ShapeDtypeStruct((n, SSD_INNER), BF16),
                   jax.ShapeDtypeStruct((bsz, SSD_HEADS, SSD_HEADDIM, SSD_STATE), F32),
                   jax.ShapeDtypeStruct((bsz, CONV_WIDTH - 1, CONV_DIM), F32)],
        scratch_shapes=[pltpu.VMEM((SUBLANES + t, CONV_DIM), F32)],
        compiler_params=_params(2),
        name="ssd",
    )(*args)


def _lane_groups(x):
    return [x[:, g * LANES:(g + 1) * LANES] for g in range(x.shape[1] // LANES)]


def _attn_body(tq, q_ref, k_ref, v_ref, o_ref, s_ref, m_ref, l_ref, acc_ref):
    qi = pl.program_id(2)
    q = q_ref[...]
    c = (QK_DIM ** -0.5) * math.log2(math.e)

    def keys(ref, j):
        return ref[pl.ds(pl.multiple_of(j * tq, tq), tq), :]

    def lane_max(m, s):
        for part in _lane_groups(s):
            m = jnp.maximum(m, part)
        return m

    m_ref[...] = jnp.full_like(m_ref, -jnp.inf)

    @pl.loop(0, qi)
    def _(j):
        s = _dot_nt(q, keys(k_ref, j))
        s_ref[j] = s
        m_ref[...] = lane_max(m_ref[...], s)

    rows = lax.broadcasted_iota(I32, (tq, tq), 0) >> LOG2_CHUNK
    cols = lax.broadcasted_iota(I32, (tq, tq), 1) >> LOG2_CHUNK
    s = jnp.where(cols <= rows, _dot_nt(q, keys(k_ref, qi)), -jnp.inf)
    s_ref[qi] = s
    row_max = jnp.max(lane_max(m_ref[...], s), axis=-1, keepdims=True)
    m_ref[...] = jnp.broadcast_to(row_max * c, m_ref.shape)
    l_ref[...] = jnp.zeros_like(l_ref)
    acc_ref[...] = jnp.zeros_like(acc_ref)

    @pl.loop(0, qi + 1)
    def _(j):
        s = s_ref[j]
        mc = m_ref[...]
        parts = [jnp.exp2(part * c - mc) for part in _lane_groups(s)]
        l_ref[...] = l_ref[...] + functools.reduce(lambda a, b: a + b, parts)
        p = jnp.concatenate(parts, axis=1).astype(BF16)
        acc_ref[...] = acc_ref[...] + _dot(p, keys(v_ref, j))

    o_ref[...] = (acc_ref[...] / jnp.sum(l_ref[...], axis=-1, keepdims=True)).astype(BF16)


def _attention_prompt(q, k, v, bsz, length):
    tq = min(ATT_TILE, length)
    nq = length // tq
    n = bsz * length
    return pl.pallas_call(
        functools.partial(_attn_body, tq),
        grid=(bsz, MLA_HEADS, nq),
        in_specs=[pl.BlockSpec((None, tq, QK_DIM), lambda b, h, i: (h, b * nq + i, 0)),
                  pl.BlockSpec((None, length, QK_DIM), lambda b, h, i: (h, b, 0)),
                  pl.BlockSpec((None, length, V_DIM), lambda b, h, i: (h, b, 0))],
        out_specs=pl.BlockSpec((tq, V_DIM), lambda b, h, i: (b * nq + i, h)),
        out_shape=jax.ShapeDtypeStruct((n, MLA_INNER), BF16),
        scratch_shapes=[pltpu.VMEM((nq, tq, tq), F32), pltpu.VMEM((tq, LANES), F32),
                        pltpu.VMEM((tq, LANES), F32), pltpu.VMEM((tq, V_DIM), F32)],
        compiler_params=_params(3),
        name="attn_prompt",
    )(q, k, v)


def _attn_cache_body(past, length, q_ref, kn_ref, vn_ref, kp_ref, vp_ref, o_ref):
    scale = QK_DIM ** -0.5
    q_chunk = (past + np.arange(length)) // CHUNK
    past_visible = (np.arange(past) // CHUNK)[None, :] <= q_chunk[:, None]
    new_visible = q_chunk[None, :] <= q_chunk[:, None]
    outs = []
    for hd in range(MLA_HEADS):
        q = q_ref[hd]
        s_p = _dot_nt(q, kp_ref[hd]) * scale
        s_n = _dot_nt(q, kn_ref[hd]) * scale
        if not past_visible.all():
            rows = (past + lax.broadcasted_iota(I32, (length, past), 0)) >> LOG2_CHUNK
            cols = lax.broadcasted_iota(I32, (length, past), 1) >> LOG2_CHUNK
            s_p = jnp.where(cols <= rows, s_p, -jnp.inf)
        if not new_visible.all():
            rows = (past + lax.broadcasted_iota(I32, (length, length), 0)) >> LOG2_CHUNK
            cols = (past + lax.broadcasted_iota(I32, (length, length), 1)) >> LOG2_CHUNK
            s_n = jnp.where(cols <= rows, s_n, -jnp.inf)
        m = jnp.maximum(jnp.max(s_p, axis=-1, keepdims=True), jnp.max(s_n, axis=-1, keepdims=True))
        p_p = jnp.exp(s_p - m)
        p_n = jnp.exp(s_n - m)
        denom = jnp.sum(p_p, axis=-1, keepdims=True) + jnp.sum(p_n, axis=-1, keepdims=True)
        o = _dot(p_p.astype(BF16), vp_ref[hd]) + _dot(p_n.astype(BF16), vn_ref[hd])
        outs.append(o / denom)
    o_ref[...] = jnp.concatenate(outs, axis=1).astype(BF16)


def _attention_cache(q, k_new, v_new, k_past, v_past, bsz, length, past):
    n = bsz * length
    new = lambda w: pl.BlockSpec((MLA_HEADS, length, w), lambda b: (0, b, 0))
    old = lambda w: pl.BlockSpec((MLA_HEADS, past, w), lambda b: (0, b, 0))
    return pl.pallas_call(
        functools.partial(_attn_cache_body, past, length),
        grid=(bsz,),
        in_specs=[new(QK_DIM), new(QK_DIM), new(V_DIM), old(QK_DIM), old(V_DIM)],
        out_specs=pl.BlockSpec((length, MLA_INNER), lambda b: (b, 0)),
        out_shape=jax.ShapeDtypeStruct((n, MLA_INNER), BF16),
        compiler_params=_params(1),
        name="attn_cache",
    )(q, k_new, v_new, k_past, v_past)


def _mix_out(x_ref, ys_ref, ym_ref, wo_ref):
    return x_ref[...] + _dot(ys_ref[...], wo_ref[:SSD_INNER, :]) + _dot(ym_ref[...], wo_ref[SSD_INNER:, :])


def _dense_body(final, x_ref, ys_ref, ym_ref, wo_ref, nw_ref, wg_ref, wu_ref, wd_ref, *rest):
    x1 = _mix_out(x_ref, ys_ref, ym_ref, wo_ref)
    h = _rms(x1, nw_ref[...]).astype(BF16)
    act = (_silu(_dot(h, wg_ref[...])) * _dot(h, wu_ref[...])).astype(BF16)
    x2 = x1 + _dot(act, wd_ref[...])
    if final:
        fw_ref, o_ref = rest
        o_ref[...] = _rms(x2, fw_ref[...])
    else:
        (o_ref,) = rest
        o_ref[...] = x2


def _resident(shape):
    zeros = (0,) * len(shape)
    return pl.BlockSpec(shape, lambda *_: zeros, pipeline_mode=pl.Buffered(1))


def _dense_layer(x, ys, ym, lw, fw, final_w, tm):
    n = x.shape[0]
    row = lambda w: pl.BlockSpec((tm, w), lambda i: (i, 0))
    d_ff = fw["w_gate"].shape[1]
    in_specs = [row(D_MODEL), row(SSD_INNER), row(MLA_INNER), _resident((SSD_INNER + MLA_INNER, D_MODEL)),
                _const_spec((1, D_MODEL)), _resident((D_MODEL, d_ff)), _resident((D_MODEL, d_ff)),
                _resident((d_ff, D_MODEL))]
    args = [x, ys, ym, lw["w_out"], lw["norm_ffn"], fw["w_gate"], fw["w_up"], fw["w_down"]]
    if final_w is not None:
        in_specs.append(_const_spec((1, D_MODEL)))
        args.append(final_w)
    return pl.pallas_call(
        functools.partial(_dense_body, final_w is not None),
        grid=(n // tm,),
        in_specs=in_specs,
        out_specs=row(D_MODEL),
        out_shape=jax.ShapeDtypeStruct((n, D_MODEL), F32),
        compiler_params=_params(1),
        name="mixout_dense",
    )(*args)


def _router_body(tm, x_ref, ys_ref, ym_ref, wo_ref, nw_ref, wr_ref,
                 x1_ref, h_ref, ri_ref, rf_ref, cnt_ref):
    @pl.when(pl.program_id(0) == 0)
    def _():
        cnt_ref[...] = jnp.zeros_like(cnt_ref)

    x1 = _mix_out(x_ref, ys_ref, ym_ref, wo_ref)
    x1_ref[...] = x1
    h = _rms(x1, nw_ref[...])
    _store_row_tiles(h_ref, h)
    lane_i =lax.broadcasted_iota(I32, (tm, LANES), 1)
    lane = lane_i.astype(F32)
    logits = jnp.where(lane_i < N_EXPERTS, _dot_hi(h, wr_ref[...]), -jnp.inf)
    top0 = jnp.max(logits, axis=-1, keepdims=True)
    e0 = jnp.min(jnp.where(logits == top0, lane, float(LANES)), axis=-1, keepdims=True)
    rest = jnp.where(lane == e0, -jnp.inf, logits)
    top1 = jnp.max(rest, axis=-1, keepdims=True)
    e1 = jnp.min(jnp.where(rest == top1, lane, float(LANES)), axis=-1, keepdims=True)
    w1 = jnp.exp(top1 - top0)
    g0 = 1.0 / (1.0 + w1)
    g1 = w1 / (1.0 + w1)
    onehot = jnp.where((lane == e0) | (lane == e1), 1.0, 0.0)
    strict = (lax.broadcasted_iota(I32, (tm, tm), 0) > lax.broadcasted_iota(I32, (tm, tm), 1)).astype(BF16)
    before = _dot(strict, onehot.astype(BF16)) + cnt_ref[0:1, :]
    pos0 = jnp.sum(jnp.where(lane == e0, before, 0.0), axis=-1, keepdims=True)
    pos1 = jnp.sum(jnp.where(lane == e1, before, 0.0), axis=-1, keepdims=True)
    cnt_ref[...] = cnt_ref[...] + jnp.sum(onehot, axis=0, keepdims=True)
    packed = jnp.where(lane_i == 0, e0, jnp.where(lane_i == 1, e1, jnp.where(lane_i == 2, pos0, pos1)))
    ri_ref[...] = packed.astype(I32)
    rf_ref[...] = jnp.where(lane_i == 0, g0, g1)


def _router_layer(x, ys, ym, lw, mw, tm):
    n = x.shape[0]
    row = lambda w: pl.BlockSpec((tm, w), lambda i: (i, 0))
    return pl.pallas_call(
        functools.partial(_router_body, tm),
        grid=(n // tm,),
        in_specs=[row(D_MODEL), row(SSD_INNER), row(MLA_INNER), _const_spec((SSD_INNER + MLA_INNER, D_MODEL)),
                  _const_spec((1, D_MODEL)), _const_spec((D_MODEL, LANES))],
        out_specs=[row(D_MODEL), _row_tile_spec(tm), row(LANES), row(LANES), _const_spec((SUBLANES, LANES))],
        out_shape=[jax.ShapeDtypeStruct((n, D_MODEL), F32), jax.ShapeDtypeStruct((n,) + ROW_TILE, F32),
                   jax.ShapeDtypeStruct((n, LANES), I32), jax.ShapeDtypeStruct((n, LANES), F32),
                   jax.ShapeDtypeStruct((SUBLANES, LANES), F32)],
        compiler_params=_params(1),
        name="mixout_router",
    )(x, ys, ym, lw["w_out"], lw["norm_ffn"], mw["w_router"])


def _dispatch_body(tm, bounds_ref, dest_ref, h_ref, xs_ref, dest_smem, zero_ref, sem):
    @pl.when(pl.program_id(0) == 0)
    def _():
        zero_ref[...] = jnp.zeros_like(zero_ref)

        def fill(start):
            copy = pltpu.make_async_copy(zero_ref, xs_ref.at[pl.ds(pl.multiple_of(start, MOE_TILE), MOE_TILE)], sem)
            copy.start()
            copy.wait()

        for e in range(N_EXPERTS):
            pl.when(bounds_ref[e + 1] > bounds_ref[e])(lambda e=e: fill(bounds_ref[e + 1] - MOE_TILE))
            unused = bounds_ref[N_EXPERTS] + e * MOE_TILE
            pl.when(unused < xs_ref.shape[0])(lambda unused=unused: fill(unused))

    pltpu.sync_copy(dest_ref.at[0], dest_smem)

    def start(j, c):
        for k in range(TOP_K):
            pltpu.make_async_copy(h_ref.at[j], xs_ref.at[dest_smem[k, j]], sem).start()
        return c

    lax.fori_loop(0, tm, start, 0, unroll=8)
    for k in range(TOP_K):
        pltpu.make_async_copy(h_ref, xs_ref.at[pl.ds(0, tm)], sem).wait()


def _dispatch(h, dest, bounds, rows, tm):
    n = h.shape[0]
    return pl.pallas_call(
        functools.partial(_dispatch_body, tm),
        grid_spec=pltpu.PrefetchScalarGridSpec(
            num_scalar_prefetch=1,
            grid=(n // tm,),
            in_specs=[pl.BlockSpec((1, TOP_K, tm), lambda i, b: (i, 0, 0)), _row_tile_spec(tm)],
            out_specs=pl.BlockSpec(memory_space=pl.ANY),
            scratch_shapes=[pltpu.SMEM((TOP_K, tm), I32), pltpu.VMEM((MOE_TILE,) + ROW_TILE, F32),
                            pltpu.SemaphoreType.DMA],
        ),
        out_shape=jax.ShapeDtypeStruct((rows,) + ROW_TILE, F32),
        compiler_params=_params(1),
        name="dispatch",
    )(bounds, dest, h)


def _experts_body(tm, te_ref, nv_ref, x_ref, wg_ref, wu_ref, wd_ref, y_ref):
    valid = nv_ref[pl.program_id(0)]

    @pl.when(valid > 0)
    def _():
        x = _load_row_tiles(x_ref).astype(BF16)
        act = (_silu(_dot(x, wg_ref[...])) * _dot(x, wu_ref[...])).astype(BF16)
        _store_row_tiles(y_ref, _dot(act, wd_ref[...]))

    @pl.when(valid == 0)
    def _():
        y_ref[...] = jnp.zeros_like(y_ref)


def _experts(xs, tile_expert, tile_valid, mw, tm):
    rows = xs.shape[0]
    d_ff = mw["w_gate"].shape[2]
    wspec = lambda a, b: pl.BlockSpec((None, a, b), lambda i, te, nv: (te[i], 0, 0),
                                      pipeline_mode=pl.Buffered(1))
    return pl.pallas_call(
        functools.partial(_experts_body, tm),
        grid_spec=pltpu.PrefetchScalarGridSpec(
            num_scalar_prefetch=2,
            grid=(rows // tm,),
            in_specs=[_row_tile_spec(tm), wspec(D_MODEL, d_ff), wspec(D_MODEL, d_ff), wspec(d_ff, D_MODEL)],
            out_specs=_row_tile_spec(tm),
        ),
        out_shape=jax.ShapeDtypeStruct((rows,) + ROW_TILE, F32),
        compiler_params=_params(1),
        name="experts",
    )(tile_expert, tile_valid, xs, mw["w_gate"], mw["w_up"], mw["w_down"])


def _combine_body(tm, final, dest_ref, dnext_ref, x1_ref, rf_ref, ys_ref, *rest):
    if final:
        fw_ref, o_ref, buf_ref, dest_smem, sems = rest
    else:
        o_ref, buf_ref, dest_smem, sems = rest
    i = pl.program_id(0)
    nsteps = pl.num_programs(0)
    slot = i % 2

    def gather(src_ref, s):
        pltpu.sync_copy(src_ref.at[0], dest_smem)

        def start(j, c):
            for k in range(TOP_K):
                pltpu.make_async_copy(ys_ref.at[dest_smem[k, j]], buf_ref.at[s, k, j], sems.at[s]).start()
            return c

        lax.fori_loop(0, tm, start, 0, unroll=8)

    @pl.when(i == 0)
    def _():
        gather(dest_ref, 0)

    for k in range(TOP_K):
        pltpu.make_async_copy(ys_ref.at[pl.ds(0, tm)], buf_ref.at[slot, k], sems.at[slot]).wait()

    @pl.when(i + 1 < nsteps)
    def _():
        gather(dnext_ref, 1 - slot)

    gates = rf_ref[...]
    out = x1_ref[...] + gates[:, 0:1] * _load_row_tiles(buf_ref, slot, 0) + \
        gates[:, 1:2] * _load_row_tiles(buf_ref, slot, 1)
    if final:
        out = _rms(out, fw_ref[...])
    o_ref[...] = out


def _combine(dest, x1, rf, ys, final_w, tm):
    n = x1.shape[0]
    nt = n // tm
    row = lambda w: pl.BlockSpec((tm, w), lambda i: (i, 0))
    in_specs = [pl.BlockSpec((1, TOP_K, tm), lambda i: (i, 0, 0)),
                pl.BlockSpec((1, TOP_K, tm), lambda i: (jnp.minimum(i + 1, nt - 1), 0, 0)),
                row(D_MODEL), row(LANES), pl.BlockSpec(memory_space=pl.ANY)]
    args = [dest, dest, x1, rf, ys]
    if final_w is not None:
        in_specs.append(_const_spec((1, D_MODEL)))
        args.append(final_w)
    return pl.pallas_call(
        functools.partial(_combine_body, tm, final_w is not None),
        grid=(nt,),
        in_specs=in_specs,
        out_specs=row(D_MODEL),
        out_shape=jax.ShapeDtypeStruct((n, D_MODEL), F32),
        scratch_shapes=[pltpu.VMEM((2, TOP_K, tm) + ROW_TILE, F32), pltpu.SMEM((TOP_K, tm), I32),
                        pltpu.SemaphoreType.DMA((2,))],
        compiler_params=_params(1),
        name="combine",
    )(*args)


def _moe_layer(x, ys, ym, lw, mw, final_w):
    n = x.shape[0]
    tm = min(TOKEN_TILE, n)
    x1, h, ri, rf, counts = _router_layer(x, ys, ym, lw, mw, tm)
    counts = counts[0, :N_EXPERTS].astype(I32)
    padded = (counts + MOE_TILE - 1) // MOE_TILE * MOE_TILE
    ends = jnp.cumsum(padded)
    offsets = ends - padded
    rows = (n * TOP_K + N_EXPERTS * (MOE_TILE - 1)) // MOE_TILE * MOE_TILE
    tile_start = jnp.arange(rows // MOE_TILE, dtype=I32) * MOE_TILE
    tile_expert = jnp.minimum(jnp.sum(tile_start[:, None] >= ends[None, :], axis=1), N_EXPERTS - 1).astype(I32)
    group_end = (offsets + counts)[tile_expert]
    tile_valid = jnp.where(tile_start < ends[-1], jnp.clip(group_end - tile_start, 0, MOE_TILE), 0).astype(I32)
    dest =jnp.stack([offsets[ri[:, 0]] + ri[:, 2], offsets[ri[:, 1]] + ri[:, 3]])

    def tiles(t):
        return dest.reshape(TOP_K, n // t, t).transpose(1, 0, 2)

    td = min(DISPATCH_TILE, n)
    tc = min(COMBINE_TILE, n)
    bounds = jnp.concatenate([jnp.zeros((1,), I32), ends.astype(I32)])
    xs = _dispatch(h, tiles(td), bounds, rows, td)
    y_sorted = _experts(xs, tile_expert, tile_valid, mw, MOE_TILE)
    return _combine(tiles(tc), x1, rf, y_sorted, final_w, tc)


def _prep_layer(layer, norm_mix, w_in, conv_w, conv_b, dt_bias, a_log, d_skip, ssd_norm, q_norm, w_q_up,
                kv_norm, w_kv_up, w_out, norm_ffn):
    w = w_in[layer]
    splits = np.cumsum([SSD_INNER, CONV_DIM, SSD_HEADS, Q_LORA, KV_LORA, ROPE_DIM])[:-1].tolist()
    w_z, w_xbc, w_dt, w_cq, w_ckv, w_kr = jnp.split(w, splits, axis=1)
    half = ROPE_DIM // 2
    w_krs = jnp.concatenate([w_kr[:, half:], w_kr[:, :half]], axis=1)
    pad = jnp.zeros((D_MODEL, LANES - SSD_HEADS), F32)
    w_cat = jnp.concatenate([w_z, w_xbc, w_cq, w_ckv, w_kr, w_krs, w_dt, pad], axis=1).astype(BF16)
    wq = w_q_up[layer].reshape(Q_LORA, MLA_HEADS, QK_DIM)
    wq = jnp.concatenate([wq, wq[:, :, NOPE_DIM + half:], wq[:, :, NOPE_DIM:NOPE_DIM + half]], axis=2)
    row = lambda v: v.reshape(1, -1).astype(F32)
    return {
        "norm_mix": row(norm_mix[layer]), "w_in": w_cat,
        "q_norm": row(q_norm[layer]), "w_q": wq.reshape(Q_LORA, MLA_HEADS * Q_HEAD_COLS).astype(BF16),
        "kv_norm": row(kv_norm[layer]), "w_kv": w_kv_up[layer].astype(BF16),
        "dt_bias": jnp.pad(row(dt_bias[layer]), ((0, 0), (0, LANES - SSD_HEADS))),
        "conv_w": conv_w[layer], "conv_b": row(conv_b[layer]), "a_log": row(a_log[layer]),
        "d_skip": row(jnp.repeat(d_skip[layer], SSD_HEADDIM)), "ssd_norm": row(ssd_norm[layer]),
        "w_out": w_out[layer].astype(BF16), "norm_ffn": row(norm_ffn[layer]),
    }


def _rope_tables(pos, tm):
    half = ROPE_DIM // 2
    inv_freq = ROPE_THETA ** (-jnp.arange(half, dtype=F32) / half)
    ang = pos.astype(F32)[:, None] * inv_freq[None, :]
    cos, sin = jnp.cos(ang), jnp.sin(ang)
    cos_tab = jnp.concatenate([cos, cos], axis=1)
    sin_tab = jnp.concatenate([-sin, sin], axis=1)
    reps = max(1, tm // pos.shape[0])
    return jnp.tile(cos_tab, (reps, 1)), jnp.tile(sin_tab, (reps, 1))


def _mixing(x, bsz, length, lw, tables, conv0, ssm0, past_k, past_v, past_len):
    n = bsz * length
    tm = min(TOKEN_TILE, n)
    z, xbc, dt, c_kv, k_rope, q, k, v = _inproj(x, lw, tables[0], tables[1], tm)
    ys, new_ssm, new_conv = _ssd(xbc, dt, z, lw, bsz, length, conv0, ssm0)
    if past_k is None:
        ym = _attention_prompt(q, k, v, bsz, length)
    else:
        ym = _attention_cache(q, k, v, past_k, past_v, bsz, length, past_len)
    return ys, ym, c_kv, k_rope, new_ssm, new_conv


def kernel(x_prompt, x_sample, cache_kv_latent, cache_k_rope, state_ssm, state_conv, norm_mix, w_in, conv_w,
           conv_b, dt_bias, a_log, d_skip, ssd_norm, q_norm, w_q_up, kv_norm, w_kv_up, w_out, norm_ffn,
           ffn_w_gate, ffn_w_up, ffn_w_down, moe_router, moe_w_gate, moe_w_up, moe_w_down, final_norm):
    bp, sp = x_prompt.shape[:2]
    bs, ss = x_sample.shape[:2]
    depth = w_in.shape[0]
    past = cache_kv_latent.shape[2]
    tabs_p = _rope_tables(jnp.arange(sp, dtype=I32), min(TOKEN_TILE, bp * sp))
    tabs_s = _rope_tables(past + jnp.arange(ss, dtype=I32), min(TOKEN_TILE, bs * ss))
    final_w = final_norm.reshape(1, D_MODEL)
    hp = x_prompt.reshape(bp * sp, D_MODEL)
    hs = x_sample.reshape(bs * ss, D_MODEL)
    outs_p, outs_s = [], []
    for layer in range(depth):
        lw = _prep_layer(layer, norm_mix, w_in, conv_w, conv_b, dt_bias, a_log, d_skip, ssd_norm, q_norm,
                         w_q_up, kv_norm, w_kv_up, w_out, norm_ffn)
        last = final_w if layer == depth - 1 else None
        past_k, past_v = _pastkv(cache_kv_latent[layer].reshape(bs * past, KV_LORA),
                                 cache_k_rope[layer].reshape(bs * past, ROPE_DIM), lw["w_kv"],
                                 min(TOKEN_TILE, bs * past))
        mix_p = _mixing(hp, bp, sp, lw, tabs_p, None, None, None, None, 0)
        mix_s = _mixing(hs, bs, ss, lw, tabs_s, state_conv[layer], state_ssm[layer], past_k, past_v, past)
        outs_p.append(mix_p[2:])
        outs_s.append(mix_s[2:])
        idx = layer // 2
        if layer % 2 == 0:
            fw = {"w_gate": ffn_w_gate[idx].astype(BF16), "w_up": ffn_w_up[idx].astype(BF16),
                  "w_down": ffn_w_down[idx].astype(BF16)}
            hp = _dense_layer(hp, mix_p[0], mix_p[1], lw, fw, last, min(TOKEN_TILE, bp * sp))
            hs = _dense_layer(hs, mix_s[0], mix_s[1], lw, fw, last, min(TOKEN_TILE, bs * ss))
        else:
            mw = {"w_router": jnp.pad(moe_router[idx].astype(F32), ((0, 0), (0, LANES - N_EXPERTS))),
                  "w_gate": moe_w_gate[idx].astype(BF16), "w_up": moe_w_up[idx].astype(BF16),
                  "w_down": moe_w_down[idx].astype(BF16)}
            hp = _moe_layer(hp, mix_p[0], mix_p[1], lw, mw, last)
            hs = _moe_layer(hs, mix_s[0], mix_s[1], lw, mw, last)

    def stack(outs, j, shape):
        return jnp.stack([o[j].reshape(shape) for o in outs])

    return (hp.reshape(bp, sp, D_MODEL), hs.reshape(bs, ss, D_MODEL),
            stack(outs_p, 0, (bp, sp, KV_LORA)), stack(outs_p, 1, (bp, sp, ROPE_DIM)),
            stack(outs_p, 2, (bp, SSD_HEADS, SSD_HEADDIM, SSD_STATE)),
            stack(outs_p, 3, (bp, CONV_WIDTH - 1, CONV_DIM)),
            stack(outs_s, 0, (bs, ss, KV_LORA)), stack(outs_s, 1, (bs, ss, ROPE_DIM)),
            stack(outs_s, 2, (bs, SSD_HEADS, SSD_HEADDIM, SSD_STATE)),
            stack(outs_s, 3, (bs, CONV_WIDTH - 1, CONV_DIM)))
```

```python
import functools
import math

import jax
import jax.numpy as jnp
import numpy as np
from jax import lax
from jax.experimental import pallas as pl
from jax.experimental.pallas import tpu as pltpu

F32 = jnp.float32
BF16 = jnp.bfloat16
I32 = jnp.int32

D_MODEL = 1024
CHUNK = 64
LOG2_CHUNK = 6
SSD_HEADS = 16
SSD_HEADDIM = 64
SSD_INNER = SSD_HEADS * SSD_HEADDIM
SSD_GROUPS = 2
HEADS_PER_GROUP = SSD_HEADS // SSD_GROUPS
SSD_STATE = 128
CONV_WIDTH = 4
CONV_DIM = SSD_INNER + 2 * SSD_GROUPS * SSD_STATE
MLA_HEADS = 8
Q_LORA = 384
KV_LORA = 256
NOPE_DIM = 128
ROPE_DIM = 64
V_DIM = 128
QK_DIM = NOPE_DIM + ROPE_DIM
ROPE_THETA = 10000.0
MLA_INNER = MLA_HEADS * V_DIM
N_EXPERTS = 8
TOP_K = 2
EPS = 1e-6

LANES = 128
SUBLANES = 8
MXU_DIM = 256
VMEM_LIMIT = 56 * 1024 * 1024

OFF_Z = 0
OFF_XBC = OFF_Z + SSD_INNER
OFF_CQ = OFF_XBC + CONV_DIM
OFF_CKV = OFF_CQ + Q_LORA
OFF_KR = OFF_CKV + KV_LORA
OFF_KRS = OFF_KR + ROPE_DIM
OFF_DT = OFF_KRS + ROPE_DIM
IN_COLS = OFF_DT + LANES
Q_HEAD_COLS = NOPE_DIM + 2 * ROPE_DIM

TOKEN_TILE = 512
ATT_TILE = 512
ATT_HEADS_PER_STEP = 4
MOE_TILE = 256
DISPATCH_TILE = 512
COMBINE_TILE = 256


def _params(n_axes):
    return pltpu.CompilerParams(dimension_semantics=("arbitrary",) * n_axes,
                                vmem_limit_bytes=VMEM_LIMIT)


def _const_spec(shape):
    zeros = (0,) * len(shape)
    return pl.BlockSpec(shape, lambda *_: zeros)


def _rms(x, w):
    return x * lax.rsqrt(jnp.mean(x * x, axis=-1, keepdims=True) + EPS) * w


def _silu(x):
    return x * (1.0 / (1.0 + jnp.exp(-x)))


def _dot(a, b):
    return jnp.dot(a, b, preferred_element_type=F32)


def _dot_nt(a, b):
    return lax.dot_general(a, b, (((1,), (1,)), ((), ())), preferred_element_type=F32)


def _dot_tn(a, b):
    return lax.dot_general(a, b, (((0,), (0,)), ((), ())), preferred_element_type=F32)


def _split3(x):
    hi = x.astype(BF16)
    rest = x - hi.astype(F32)
    mid = rest.astype(BF16)
    lo = (rest - mid.astype(F32)).astype(BF16)
    return jnp.concatenate([hi, mid, lo], axis=1)


ROW_TILE = (D_MODEL // LANES, LANES)


def _row_tile_spec(tm):
    return pl.BlockSpec((tm,) + ROW_TILE, lambda i, *_: (i, 0, 0))


def _store_row_tiles(ref, x):
    for s in range(ROW_TILE[0]):
        ref[:, s, :] = x[:, s * LANES:(s + 1) * LANES]


def _load_row_tiles(ref, *lead):
    return jnp.concatenate([ref[lead + (slice(None), s, slice(None))] for s in range(ROW_TILE[0])], axis=1)


def _inproj_body(x_ref, nw_ref, win_ref, qnw_ref, wq_ref, kvnw_ref, wkv_ref, cos_ref, sin_ref, dtb_ref,
                 z_ref, xbc_ref, dt_ref, ckv_ref, kr_ref, q_ref, k_ref, v_ref):
    h = _rms(x_ref[...], nw_ref[...]).astype(BF16)
    proj = _dot(h, win_ref[...])
    z_ref[...] = proj[:, OFF_Z:OFF_XBC]
    xbc_ref[...] = proj[:, OFF_XBC:OFF_CQ]
    pre = proj[:, OFF_DT:IN_COLS] + dtb_ref[...]
    dt_ref[...] = jnp.maximum(pre, 0.0) + jnp.log(1.0 + jnp.exp(-jnp.abs(pre)))
    cos = cos_ref[...]
    sin = sin_ref[...]
    k_rope = proj[:, OFF_KR:OFF_KRS] * cos + proj[:, OFF_KRS:OFF_DT] * sin
    kr_ref[...] = k_rope
    c_kv = _rms(proj[:, OFF_CKV:OFF_KR], kvnw_ref[...])
    ckv_ref[...] = c_kv
    qn = _rms(proj[:, OFF_CQ:OFF_CKV], qnw_ref[...]).astype(BF16)
    q_all = _dot(qn, wq_ref[...])
    kv_all = _dot(c_kv.astype(BF16), wkv_ref[...])
    k_rope_b = k_rope.astype(BF16)
    for hd in range(MLA_HEADS):
        qb = q_all[:, hd * Q_HEAD_COLS:(hd + 1) * Q_HEAD_COLS]
        q_rope = qb[:, NOPE_DIM:QK_DIM] * cos + qb[:, QK_DIM:Q_HEAD_COLS] * sin
        q_ref[hd] = jnp.concatenate([qb[:, :NOPE_DIM], q_rope], axis=1).astype(BF16)
        kvb = kv_all[:, hd * (NOPE_DIM + V_DIM):(hd + 1) * (NOPE_DIM + V_DIM)]
        k_ref[hd] = jnp.concatenate([kvb[:, :NOPE_DIM].astype(BF16), k_rope_b], axis=1)
        v_ref[hd] = kvb[:, NOPE_DIM:].astype(BF16)


def _inproj(x, lw, cos_tab, sin_tab, tm):
    n = x.shape[0]
    nt = n // tm
    tab_blocks = cos_tab.shape[0] // tm
    row = lambda w: pl.BlockSpec((tm, w), lambda i: (i, 0))
    head = lambda w: pl.BlockSpec((MLA_HEADS, tm, w), lambda i: (0, i, 0))
    tab = pl.BlockSpec((tm, ROPE_DIM), lambda i: (i % tab_blocks, 0))
    return pl.pallas_call(
        _inproj_body,
        grid=(nt,),
        in_specs=[row(D_MODEL), _const_spec((1, D_MODEL)), _const_spec((D_MODEL, IN_COLS)),
                  _const_spec((1, Q_LORA)), _const_spec((Q_LORA, MLA_HEADS * Q_HEAD_COLS)),
                  _const_spec((1, KV_LORA)), _const_spec((KV_LORA, MLA_HEADS * (NOPE_DIM + V_DIM))),
                  tab, tab, _const_spec((1, LANES))],
        out_specs=[row(SSD_INNER), row(CONV_DIM), row(LANES), row(KV_LORA), row(ROPE_DIM),
                   head(QK_DIM), head(QK_DIM), head(V_DIM)],
        out_shape=[jax.ShapeDtypeStruct((n, SSD_INNER), F32), jax.ShapeDtypeStruct((n, CONV_DIM), F32),
                   jax.ShapeDtypeStruct((n, LANES), F32), jax.ShapeDtypeStruct((n, KV_LORA), F32),
                   jax.ShapeDtypeStruct((n, ROPE_DIM), F32),
                   jax.ShapeDtypeStruct((MLA_HEADS, n, QK_DIM), BF16),
                   jax.ShapeDtypeStruct((MLA_HEADS, n, QK_DIM), BF16),
                   jax.ShapeDtypeStruct((MLA_HEADS, n, V_DIM), BF16)],
        compiler_params=_params(1),
        name="inproj",
    )(x, lw["norm_mix"], lw["w_in"], lw["q_norm"], lw["w_q"], lw["kv_norm"], lw["w_kv"],
      cos_tab, sin_tab, lw["dt_bias"])


def _pastkv_body(lat_ref, kr_ref, wkv_ref, k_ref, v_ref):
    kv_all = _dot(lat_ref[...].astype(BF16), wkv_ref[...])
    k_rope_b = kr_ref[...].astype(BF16)
    for hd in range(MLA_HEADS):
        kvb = kv_all[:, hd * (NOPE_DIM + V_DIM):(hd + 1) * (NOPE_DIM + V_DIM)]
        k_ref[hd] = jnp.concatenate([kvb[:, :NOPE_DIM].astype(BF16), k_rope_b], axis=1)
        v_ref[hd] = kvb[:, NOPE_DIM:].astype(BF16)


def _pastkv(lat, kr, w_kv, tm):
    n = lat.shape[0]
    row = lambda w: pl.BlockSpec((tm, w), lambda i: (i, 0))
    head = lambda w: pl.BlockSpec((MLA_HEADS, tm, w), lambda i: (0, i, 0))
    return pl.pallas_call(
        _pastkv_body,
        grid=(n // tm,),
        in_specs=[row(KV_LORA), row(ROPE_DIM), _const_spec(w_kv.shape)],
        out_specs=[head(QK_DIM), head(V_DIM)],
        out_shape=[jax.ShapeDtypeStruct((MLA_HEADS, n, QK_DIM), BF16),
                   jax.ShapeDtypeStruct((MLA_HEADS, n, V_DIM), BF16)],
        compiler_params=_params(1),
        name="pastkv",
    )(lat, kr, w_kv)


def _ssd_body(t, has_init, *refs):
    if has_init:
        (xbc_ref, dt_ref, z_ref, cw_ref, cb_ref, alog_ref, dskip_ref, nw_ref, conv0_ref, ssm0_ref,
         y_ref, ssm_ref, conv_ref, hist_ref) = refs
    else:
        (xbc_ref, dt_ref, z_ref, cw_ref, cb_ref, alog_ref, dskip_ref, nw_ref,
         y_ref, ssm_ref, conv_ref, hist_ref) = refs
    hist_rows = SUBLANES
    first = hist_rows - (CONV_WIDTH - 1)
    group_lanes = HEADS_PER_GROUP * t
    group_cols = HEADS_PER_GROUP * SSD_HEADDIM
    heads_per_block = MXU_DIM // t
    blocks_per_group = group_lanes // MXU_DIM
    log2_t = int(math.log2(t))
    log2_p = int(math.log2(SSD_HEADDIM))

    @pl.when(pl.program_id(1) == 0)
    def _():
        hist_ref[...] = jnp.zeros_like(hist_ref)
        if has_init:
            hist_ref[first:hist_rows, :] = conv0_ref[...]
            ssm_ref[...] = ssm0_ref[...]
        else:
            ssm_ref[...] = jnp.zeros_like(ssm_ref)

    x_raw = xbc_ref[...]
    hist_ref[hist_rows:, :] = x_raw
    conv = cb_ref[...]
    for k in range(CONV_WIDTH):
        conv = conv + hist_ref[first + k:first + k + t, :] * cw_ref[k:k + 1, :]
    hist_ref[first:hist_rows, :] = x_raw[t - (CONV_WIDTH - 1):t]
    conv_ref[...] = x_raw[t - (CONV_WIDTH - 1):t]
    act = _silu(conv)
    xs = act[:, :SSD_INNER]
    b_mat = act[:, SSD_INNER:SSD_INNER + SSD_GROUPS * SSD_STATE]
    c_mat = act[:, SSD_INNER + SSD_GROUPS * SSD_STATE:]

    dt = dt_ref[:, :SSD_HEADS]
    da = dt * (-jnp.exp(alog_ref[...]))
    tril = (lax.broadcasted_iota(I32, (t, t), 0) >= lax.broadcasted_iota(I32, (t, t), 1)).astype(BF16)
    cs3 = _dot(tril, _split3(da))
    cs = cs3[:, :SSD_HEADS] + cs3[:, SSD_HEADS:2 * SSD_HEADS] + cs3[:, 2 * SSD_HEADS:]
    cs_last = cs[t - 1:t, :]

    def expander(width, shift):
        rows = lax.broadcasted_iota(I32, (3 * SSD_HEADS, width), 0)
        return ((lax.broadcasted_iota(I32, (3 * SSD_HEADS, width), 1) >> shift) == (rows & (SSD_HEADS - 1))
                ).astype(BF16)

    per_p = _dot(_split3(jnp.concatenate([cs, dt], axis=0)), expander(SSD_INNER, log2_p))
    cs_p, dt_p = per_p[:t], per_p[t:]
    cs_t = cs_p if t == SSD_HEADDIM else _dot(_split3(cs), expander(SSD_HEADS * t, log2_t))
    lane = lax.broadcasted_iota(I32, (t, SSD_HEADS * t), 1) & (t - 1)
    rowi = lax.broadcasted_iota(I32, (t, SSD_HEADS * t), 0)
    cs_src = jnp.sum(jnp.where(lane == rowi, cs_t, 0.0), axis=0, keepdims=True)
    decay = jnp.exp(jnp.where(rowi >= lane, cs_t - cs_src, -jnp.inf))
    exp_cs = jnp.exp(cs_p)
    to_end = jnp.exp(cs_p[t - 1:t, :] - cs_p)
    xdt = xs * dt_p
    x_end = (xdt * to_end).astype(BF16)
    chunk_decay = jnp.exp(cs_last)
    block_mask = (lax.broadcasted_iota(I32, (MXU_DIM, heads_per_block * SSD_HEADDIM), 0) >> log2_t) == \
        (lax.broadcasted_iota(I32, (MXU_DIM, heads_per_block * SSD_HEADDIM), 1) >> log2_p)

    y_blocks = []
    for g in range(SSD_GROUPS):
        bg = b_mat[:, g * SSD_STATE:(g + 1) * SSD_STATE].astype(BF16)
        cg = c_mat[:, g * SSD_STATE:(g + 1) * SSD_STATE].astype(BF16)
        cb = _dot_nt(cg, jnp.concatenate([bg] * HEADS_PER_GROUP, axis=0))
        m = (cb * decay[:, g * group_lanes:(g + 1) * group_lanes]).astype(BF16)
        h0, h1 = g * HEADS_PER_GROUP, (g + 1) * HEADS_PER_GROUP
        state = ssm_ref[h0:h1].reshape(group_cols, SSD_STATE)
        y_off = _dot_nt(cg, state.astype(BF16)) * exp_cs[:, g * group_cols:(g + 1) * group_cols]
        for kb in range(blocks_per_group):
            c0 = g * group_cols + kb * heads_per_block * SSD_HEADDIM
            c1 = c0 + heads_per_block * SSD_HEADDIM
            xb = xdt[:, c0:c1]
            bd = jnp.where(block_mask, jnp.concatenate([xb] * heads_per_block, axis=0), 0.0).astype(BF16)
            y_diag = _dot(m[:, kb * MXU_DIM:(kb + 1) * MXU_DIM], bd)
            y_blocks.append(y_diag + y_off[:, c0 - g * group_cols:c1 - g * group_cols]
                            + xs[:, c0:c1] * dskip_ref[:, c0:c1])
        local = _dot_tn(x_end[:, g * group_cols:(g + 1) * group_cols], bg)
        for r in range(HEADS_PER_GROUP):
            hd = h0 + r
            ssm_ref[hd] = ssm_ref[hd] * chunk_decay[:, hd:hd + 1] + \
                local[r * SSD_HEADDIM:(r + 1) * SSD_HEADDIM]
    y = jnp.concatenate(y_blocks, axis=1)
    y_ref[...] = _rms(y * _silu(z_ref[...]), nw_ref[...]).astype(BF16)


def _ssd(xbc, dt, z, lw, bsz, length, conv0, ssm0):
    t = CHUNK if length % CHUNK == 0 else length
    nc = length // t
    n = bsz * length
    has_init = conv0 is not None
    row = lambda w: pl.BlockSpec((t, w), lambda b, c: (b * nc + c, 0))
    conv_spec = pl.BlockSpec((None, CONV_WIDTH - 1, CONV_DIM), lambda b, c: (b, 0, 0))
    ssm_spec = pl.BlockSpec((None, SSD_HEADS, SSD_HEADDIM, SSD_STATE), lambda b, c: (b, 0, 0, 0))
    in_specs = [row(CONV_DIM), row(LANES), row(SSD_INNER), _const_spec((CONV_WIDTH, CONV_DIM)),
                _const_spec((1, CONV_DIM)), _const_spec((1, SSD_HEADS)), _const_spec((1, SSD_INNER)),
                _const_spec((1, SSD_INNER))]
    args = [xbc, dt, z, lw["conv_w"], lw["conv_b"], lw["a_log"], lw["d_skip"], lw["ssd_norm"]]
    if has_init:
        in_specs += [conv_spec, ssm_spec]
        args += [conv0, ssm0]
    return pl.pallas_call(
        functools.partial(_ssd_body, t, has_init),
        grid=(bsz, nc),
        in_specs=in_specs,
        out_specs=[row(SSD_INNER), ssm_spec, conv_spec],
        out_shape=[jax.ShapeDtypeStruct((n, SSD_INNER), BF16),
                   jax.ShapeDtypeStruct((bsz, SSD_HEADS, SSD_HEADDIM, SSD_STATE), F32),
                   jax.ShapeDtypeStruct((bsz, CONV_WIDTH - 1, CONV_DIM), F32)],
        scratch_shapes=[pltpu.VMEM((SUBLANES + t, CONV_DIM), F32)],
        compiler_params=_params(2),
        name="ssd",
    )(*args)


def _lane_groups(x):
    return [x[:, g * LANES:(g + 1) * LANES] for g in range(x.shape[1] // LANES)]


def _attn_body(tq, nh, q_ref, k_ref, v_ref, o_ref, s_ref, m_ref, l_ref, acc_ref):
    qi = pl.program_id(2)
    heads = range(nh)
    q = [q_ref[h] for h in heads]
    c = (QK_DIM ** -0.5) * math.log2(math.e)

    def keys(ref, h, j):
        return ref[h, pl.ds(pl.multiple_of(j * tq, tq), tq), :]

    def lane_max(m, s):
        for part in _lane_groups(s):
            m = jnp.maximum(m, part)
        return m

    m_ref[...] = jnp.full_like(m_ref, -jnp.inf)

    @pl.loop(0, qi)
    def _(j):
        for h in heads:
            s = _dot_nt(q[h], keys(k_ref, h, j)) * c
            s_ref[h, j] = s
            m_ref[h] = lane_max(m_ref[h], s)

    rows = lax.broadcasted_iota(I32, (tq, tq), 0) >> LOG2_CHUNK
    cols = lax.broadcasted_iota(I32, (tq, tq), 1) >> LOG2_CHUNK
    for h in heads:
        s = jnp.where(cols <= rows, _dot_nt(q[h], keys(k_ref, h, qi)) * c, -jnp.inf)
        s_ref[h, qi] = s
        row_max = jnp.max(lane_max(m_ref[h], s), axis=-1, keepdims=True)
        m_ref[h] = jnp.broadcast_to(row_max, (tq, LANES))
    l_ref[...] = jnp.zeros_like(l_ref)
    acc_ref[...] = jnp.zeros_like(acc_ref)

    @pl.loop(0, qi + 1)
    def _(j):
        for h in heads:
            mc = m_ref[h]
            parts = [jnp.exp2(part - mc) for part in _lane_groups(s_ref[h, j])]
            l_ref[h] = l_ref[h] + functools.reduce(lambda a, b: a + b, parts)
            p = jnp.concatenate(parts, axis=1).astype(BF16)
            acc_ref[h] = acc_ref[h] + _dot(p, keys(v_ref, h, j))

    o_ref[...] = jnp.concatenate(
        [acc_ref[h] / jnp.sum(l_ref[h], axis=-1, keepdims=True) for h in heads], axis=1).astype(BF16)


def _attention_prompt(q, k, v, bsz, length):
    tq = min(ATT_TILE, length)
    nh = ATT_HEADS_PER_STEP
    nq = length // tq
    n = bsz * length
    return pl.pallas_call(
        functools.partial(_attn_body, tq, nh),
        grid=(bsz, MLA_HEADS // nh, nq),
        in_specs=[pl.BlockSpec((nh, tq, QK_DIM), lambda b, h, i: (h, b * nq + i, 0)),
                  pl.BlockSpec((nh, length, QK_DIM), lambda b, h, i: (h, b, 0), pipeline_mode=pl.Buffered(1)),
                  pl.BlockSpec((nh, length, V_DIM), lambda b, h, i: (h, b, 0), pipeline_mode=pl.Buffered(1))],
        out_specs=pl.BlockSpec((tq, nh * V_DIM), lambda b, h, i: (b * nq + i, h)),
        out_shape=jax.ShapeDtypeStruct((n, MLA_INNER), BF16),
        scratch_shapes=[pltpu.VMEM((nh, nq, tq, tq), F32), pltpu.VMEM((nh, tq, LANES), F32),
                        pltpu.VMEM((nh, tq, LANES), F32), pltpu.VMEM((nh, tq, V_DIM), F32)],
        compiler_params=_params(3),
        name="attn_prompt",
    )(q, k, v)


def _attn_cache_body(past, length, q_ref, kn_ref, vn_ref, kp_ref, vp_ref, o_ref):
    scale = QK_DIM ** -0.5
    q_chunk = (past + np.arange(length)) // CHUNK
    past_visible = (np.arange(past) // CHUNK)[None, :] <= q_chunk[:, None]
    new_visible = q_chunk[None, :] <= q_chunk[:, None]
    outs = []
    for hd in range(MLA_HEADS):
        q = q_ref[hd]
        s_p = _dot_nt(q, kp_ref[hd]) * scale
        s_n = _dot_nt(q, kn_ref[hd]) * scale
        if not past_visible.all():
            rows = (past + lax.broadcasted_iota(I32, (length, past), 0)) >> LOG2_CHUNK
            cols = lax.broadcasted_iota(I32, (length, past), 1) >> LOG2_CHUNK
            s_p = jnp.where(cols <= rows, s_p, -jnp.inf)
        if not new_visible.all():
            rows = (past + lax.broadcasted_iota(I32, (length, length), 0)) >> LOG2_CHUNK
            cols = (past + lax.broadcasted_iota(I32, (length, length), 1)) >> LOG2_CHUNK
            s_n = jnp.where(cols <= rows, s_n, -jnp.inf)
        m = jnp.maximum(jnp.max(s_p, axis=-1, keepdims=True), jnp.max(s_n, axis=-1, keepdims=True))
        p_p = jnp.exp(s_p - m)
        p_n = jnp.exp(s_n - m)
        denom = jnp.sum(p_p, axis=-1, keepdims=True) + jnp.sum(p_n, axis=-1, keepdims=True)
        o = _dot(p_p.astype(BF16), vp_ref[hd]) + _dot(p_n.astype(BF16), vn_ref[hd])
        outs.append(o / denom)
    o_ref[...] = jnp.concatenate(outs, axis=1).astype(BF16)


def _attention_cache(q, k_new, v_new, k_past, v_past, bsz, length, past):
    n = bsz * length
    new = lambda w: pl.BlockSpec((MLA_HEADS, length, w), lambda b: (0, b, 0))
    old = lambda w: pl.BlockSpec((MLA_HEADS, past, w), lambda b: (0, b, 0))
    return pl.pallas_call(
        functools.partial(_attn_cache_body, past, length),
        grid=(bsz,),
        in_specs=[new(QK_DIM), new(QK_DIM), new(V_DIM), old(QK_DIM), old(V_DIM)],
        out_specs=pl.BlockSpec((length, MLA_INNER), lambda b: (b, 0)),
        out_shape=jax.ShapeDtypeStruct((n, MLA_INNER), BF16),
        compiler_params=_params(1),
        name="attn_cache",
    )(q, k_new, v_new, k_past, v_past)


def _mix_out(x_ref, ys_ref, ym_ref, wo_ref):
    return x_ref[...] + _dot(ys_ref[...], wo_ref[:SSD_INNER, :]) + _dot(ym_ref[...], wo_ref[SSD_INNER:, :])


def _dense_body(final, x_ref, ys_ref, ym_ref, wo_ref, nw_ref, wg_ref, wu_ref, wd_ref, *rest):
    x1 = _mix_out(x_ref, ys_ref, ym_ref, wo_ref)
    h = _rms(x1, nw_ref[...]).astype(BF16)
    act = (_silu(_dot(h, wg_ref[...])) * _dot(h, wu_ref[...])).astype(BF16)
    x2 = x1 + _dot(act, wd_ref[...])
    if final:
        fw_ref, o_ref = rest
        o_ref[...] = _rms(x2, fw_ref[...])
    else:
        (o_ref,) = rest
        o_ref[...] = x2


def _resident(shape):
    zeros = (0,) * len(shape)
    return pl.BlockSpec(shape, lambda *_: zeros, pipeline_mode=pl.Buffered(1))


def _dense_layer(x, ys, ym, lw, fw, final_w, tm):
    n = x.shape[0]
    row = lambda w: pl.BlockSpec((tm, w), lambda i: (i, 0))
    d_ff = fw["w_gate"].shape[1]
    in_specs = [row(D_MODEL), row(SSD_INNER), row(MLA_INNER), _resident((SSD_INNER + MLA_INNER, D_MODEL)),
                _const_spec((1, D_MODEL)), _resident((D_MODEL, d_ff)), _resident((D_MODEL, d_ff)),
                _resident((d_ff, D_MODEL))]
    args = [x, ys, ym, lw["w_out"], lw["norm_ffn"], fw["w_gate"], fw["w_up"], fw["w_down"]]
    if final_w is not None:
        in_specs.append(_const_spec((1, D_MODEL)))
        args.append(final_w)
    return pl.pallas_call(
        functools.partial(_dense_body, final_w is not None),
        grid=(n // tm,),
        in_specs=in_specs,
        out_specs=row(D_MODEL),
        out_shape=jax.ShapeDtypeStruct((n, D_MODEL), F32),
        compiler_params=_params(1),
        name="mixout_dense",
    )(*args)


def _router_body(tm, x_ref, ys_ref, ym_ref, wo_ref, nw_ref, wr_ref,
                 x1_ref, h_ref, ri_ref, rf_ref, cnt_ref):
    @pl.when(pl.program_id(0) == 0)
    def _():
        cnt_ref[...] = jnp.zeros_like(cnt_ref)

    x1 = _mix_out(x_ref, ys_ref, ym_ref, wo_ref)
    x1_ref[...] = x1
    h = _rms(x1, nw_ref[...])
    _store_row_tiles(h_ref, h)
    lane_i =lax.broadcasted_iota(I32, (tm, LANES), 1)
    lane = lane_i.astype(F32)
    h_hi = h.astype(BF16)
    h_lo = (h - h_hi.astype(F32)).astype(BF16)
    hi_terms = _dot(h_hi, wr_ref[...])
    logits = hi_terms + pltpu.roll(hi_terms, LANES - N_EXPERTS, axis=1) + _dot(h_lo, wr_ref[...])
    logits = jnp.where(lane_i < N_EXPERTS, logits, -jnp.inf)
    top0 = jnp.max(logits, axis=-1, keepdims=True)
    e0 = jnp.min(jnp.where(logits == top0, lane, float(LANES)), axis=-1, keepdims=True)
    rest = jnp.where(lane == e0, -jnp.inf, logits)
    top1 = jnp.max(rest, axis=-1, keepdims=True)
    e1 = jnp.min(jnp.where(rest == top1, lane, float(LANES)), axis=-1, keepdims=True)
    w1 = jnp.exp(top1 - top0)
    g0 = 1.0 / (1.0 + w1)
    g1 = w1 / (1.0 + w1)
    onehot = jnp.where((lane == e0) | (lane == e1), 1.0, 0.0)
    strict = (lax.broadcasted_iota(I32, (tm, tm), 0) > lax.broadcasted_iota(I32, (tm, tm), 1)).astype(BF16)
    before = _dot(strict, onehot.astype(BF16)) + cnt_ref[0:1, :]
    pos0 = jnp.sum(jnp.where(lane == e0, before, 0.0), axis=-1, keepdims=True)
    pos1 = jnp.sum(jnp.where(lane == e1, before, 0.0), axis=-1, keepdims=True)
    cnt_ref[...] = cnt_ref[...] + jnp.sum(onehot, axis=0, keepdims=True)
    packed = jnp.where(lane_i == 0, e0, jnp.where(lane_i == 1, e1, jnp.where(lane_i == 2, pos0, pos1)))
    ri_ref[...] = packed.astype(I32)
    rf_ref[...] = jnp.where(lane_i == 0, g0, g1)


def _router_layer(x, ys, ym, lw, mw, tm):
    n = x.shape[0]
    row = lambda w: pl.BlockSpec((tm, w), lambda i: (i, 0))
    return pl.pallas_call(
        functools.partial(_router_body, tm),
        grid=(n // tm,),
        in_specs=[row(D_MODEL), row(SSD_INNER), row(MLA_INNER), _const_spec((SSD_INNER + MLA_INNER, D_MODEL)),
                  _const_spec((1, D_MODEL)), _const_spec((D_MODEL, LANES))],
        out_specs=[row(D_MODEL), _row_tile_spec(tm), row(LANES), row(LANES), _const_spec((SUBLANES, LANES))],
        out_shape=[jax.ShapeDtypeStruct((n, D_MODEL), F32), jax.ShapeDtypeStruct((n,) + ROW_TILE, F32),
                   jax.ShapeDtypeStruct((n, LANES), I32), jax.ShapeDtypeStruct((n, LANES), F32),
                   jax.ShapeDtypeStruct((SUBLANES, LANES), F32)],
        compiler_params=_params(1),
        name="mixout_router",
    )(x, ys, ym, lw["w_out"], lw["norm_ffn"], mw["w_router"])


def _dispatch_body(tm, bounds_ref, dest_ref, h_ref, xs_ref, dest_smem, zero_ref, sem):
    @pl.when(pl.program_id(0) == 0)
    def _():
        zero_ref[...] = jnp.zeros_like(zero_ref)

        def fill(start):
            copy = pltpu.make_async_copy(zero_ref, xs_ref.at[pl.ds(pl.multiple_of(start, MOE_TILE), MOE_TILE)], sem)
            copy.start()
            copy.wait()

        for e in range(N_EXPERTS):
            pl.when(bounds_ref[e + 1] > bounds_ref[e])(lambda e=e: fill(bounds_ref[e + 1] - MOE_TILE))
            unused = bounds_ref[N_EXPERTS] + e * MOE_TILE
            pl.when(unused < xs_ref.shape[0])(lambda unused=unused: fill(unused))

    pltpu.sync_copy(dest_ref.at[0], dest_smem)

    def start(j, c):
        for k in range(TOP_K):
            pltpu.make_async_copy(h_ref.at[j], xs_ref.at[dest_smem[k, j]], sem).start()
        return c

    lax.fori_loop(0, tm, start, 0, unroll=8)
    for k in range(TOP_K):
        pltpu.make_async_copy(h_ref, xs_ref.at[pl.ds(0, tm)], sem).wait()


def _dispatch(h, dest, bounds, rows, tm):
    n = h.shape[0]
    return pl.pallas_call(
        functools.partial(_dispatch_body, tm),
        grid_spec=pltpu.PrefetchScalarGridSpec(
            num_scalar_prefetch=1,
            grid=(n // tm,),
            in_specs=[pl.BlockSpec((1, TOP_K, tm), lambda i, b: (i, 0, 0)), _row_tile_spec(tm)],
            out_specs=pl.BlockSpec(memory_space=pl.ANY),
            scratch_shapes=[pltpu.SMEM((TOP_K, tm), I32), pltpu.VMEM((MOE_TILE,) + ROW_TILE, F32),
                            pltpu.SemaphoreType.DMA],
        ),
        out_shape=jax.ShapeDtypeStruct((rows,) + ROW_TILE, F32),
        compiler_params=_params(1),
        name="dispatch",
    )(bounds, dest, h)


def _experts_body(tm, te_ref, nv_ref, x_ref, wg_ref, wu_ref, wd_ref, y_ref):
    valid = nv_ref[pl.program_id(0)]

    @pl.when(valid > 0)
    def _():
        x = _load_row_tiles(x_ref).astype(BF16)
        act = (_silu(_dot(x, wg_ref[...])) * _dot(x, wu_ref[...])).astype(BF16)
        _store_row_tiles(y_ref, _dot(act, wd_ref[...]))

    @pl.when(valid == 0)
    def _():
        y_ref[...] = jnp.zeros_like(y_ref)


def _experts(xs, tile_expert, tile_valid, mw, tm):
    rows = xs.shape[0]
    d_ff = mw["w_gate"].shape[2]
    wspec = lambda a, b: pl.BlockSpec((None, a, b), lambda i, te, nv: (te[i], 0, 0),
                                      pipeline_mode=pl.Buffered(1))
    return pl.pallas_call(
        functools.partial(_experts_body, tm),
        grid_spec=pltpu.PrefetchScalarGridSpec(
            num_scalar_prefetch=2,
            grid=(rows // tm,),
            in_specs=[_row_tile_spec(tm), wspec(D_MODEL, d_ff), wspec(D_MODEL, d_ff), wspec(d_ff, D_MODEL)],
            out_specs=_row_tile_spec(tm),
        ),
        out_shape=jax.ShapeDtypeStruct((rows,) + ROW_TILE, F32),
        compiler_params=_params(1),
        name="experts",
    )(tile_expert, tile_valid, xs, mw["w_gate"], mw["w_up"], mw["w_down"])


def _combine_body(tm, final, dest_ref, dnext_ref, x1_ref, rf_ref, ys_ref, *rest):
    if final:
        fw_ref, o_ref, buf_ref, dest_smem, sems = rest
    else:
        o_ref, buf_ref, dest_smem, sems = rest
    i = pl.program_id(0)
    nsteps = pl.num_programs(0)
    slot = i % 2

    def gather(src_ref, s):
        pltpu.sync_copy(src_ref.at[0], dest_smem)

        def start(j, c):
            for k in range(TOP_K):
                pltpu.make_async_copy(ys_ref.at[dest_smem[k, j]], buf_ref.at[s, k, j], sems.at[s]).start()
            return c

        lax.fori_loop(0, tm, start, 0, unroll=8)

    @pl.when(i == 0)
    def _():
        gather(dest_ref, 0)

    for k in range(TOP_K):
        pltpu.make_async_copy(ys_ref.at[pl.ds(0, tm)], buf_ref.at[slot, k], sems.at[slot]).wait()

    @pl.when(i + 1 < nsteps)
    def _():
        gather(dnext_ref, 1 - slot)

    gates = rf_ref[...]
    out = x1_ref[...] + gates[:, 0:1] * _load_row_tiles(buf_ref, slot, 0) + \
        gates[:, 1:2] * _load_row_tiles(buf_ref, slot, 1)
    if final:
        out = _rms(out, fw_ref[...])
    o_ref[...] = out


def _combine(dest, x1, rf, ys, final_w, tm):
    n = x1.shape[0]
    nt = n // tm
    row = lambda w: pl.BlockSpec((tm, w), lambda i: (i, 0))
    in_specs = [pl.BlockSpec((1, TOP_K, tm), lambda i: (i, 0, 0)),
                pl.BlockSpec((1, TOP_K, tm), lambda i: (jnp.minimum(i + 1, nt - 1), 0, 0)),
                row(D_MODEL), row(LANES), pl.BlockSpec(memory_space=pl.ANY)]
    args = [dest, dest, x1, rf, ys]
    if final_w is not None:
        in_specs.append(_const_spec((1, D_MODEL)))
        args.append(final_w)
    return pl.pallas_call(
        functools.partial(_combine_body, tm, final_w is not None),
        grid=(nt,),
        in_specs=in_specs,
        out_specs=row(D_MODEL),
        out_shape=jax.ShapeDtypeStruct((n, D_MODEL), F32),
        scratch_shapes=[pltpu.VMEM((2, TOP_K, tm) + ROW_TILE, F32), pltpu.SMEM((TOP_K, tm), I32),
                        pltpu.SemaphoreType.DMA((2,))],
        compiler_params=_params(1),
        name="combine",
    )(*args)


def _moe_layer(x, ys, ym, lw, mw, final_w):
    n = x.shape[0]
    tm = min(TOKEN_TILE, n)
    x1, h, ri, rf, counts = _router_layer(x, ys, ym, lw, mw, tm)
    counts = counts[0, :N_EXPERTS].astype(I32)
    padded = (counts + MOE_TILE - 1) // MOE_TILE * MOE_TILE
    ends = jnp.cumsum(padded)
    offsets = ends - padded
    rows = (n * TOP_K + N_EXPERTS * (MOE_TILE - 1)) // MOE_TILE * MOE_TILE
    tile_start = jnp.arange(rows // MOE_TILE, dtype=I32) * MOE_TILE
    tile_expert = jnp.minimum(jnp.sum(tile_start[:, None] >= ends[None, :], axis=1), N_EXPERTS - 1).astype(I32)
    group_end = (offsets + counts)[tile_expert]
    tile_valid = jnp.where(tile_start < ends[-1], jnp.clip(group_end - tile_start, 0, MOE_TILE), 0).astype(I32)
    dest =jnp.stack([offsets[ri[:, 0]] + ri[:, 2], offsets[ri[:, 1]] + ri[:, 3]])

    def tiles(t):
        return dest.reshape(TOP_K, n // t, t).transpose(1, 0, 2)

    td = min(DISPATCH_TILE, n)
    tc = min(COMBINE_TILE, n)
    bounds = jnp.concatenate([jnp.zeros((1,), I32), ends.astype(I32)])
    xs = _dispatch(h, tiles(td), bounds, rows, td)
    y_sorted = _experts(xs, tile_expert, tile_valid, mw, MOE_TILE)
    return _combine(tiles(tc), x1, rf, y_sorted, final_w, tc)


def _prep_layer(layer, norm_mix, w_in, conv_w, conv_b, dt_bias, a_log, d_skip, ssd_norm, q_norm, w_q_up,
                kv_norm, w_kv_up, w_out, norm_ffn):
    w = w_in[layer]
    splits = np.cumsum([SSD_INNER, CONV_DIM, SSD_HEADS, Q_LORA, KV_LORA, ROPE_DIM])[:-1].tolist()
    w_z, w_xbc, w_dt, w_cq, w_ckv, w_kr = jnp.split(w, splits, axis=1)
    half = ROPE_DIM // 2
    w_krs = jnp.concatenate([w_kr[:, half:], w_kr[:, :half]], axis=1)
    pad = jnp.zeros((D_MODEL, LANES - SSD_HEADS), F32)
    w_cat = jnp.concatenate([w_z, w_xbc, w_cq, w_ckv, w_kr, w_krs, w_dt, pad], axis=1).astype(BF16)
    wq = w_q_up[layer].reshape(Q_LORA, MLA_HEADS, QK_DIM)
    wq = jnp.concatenate([wq, wq[:, :, NOPE_DIM + half:], wq[:, :, NOPE_DIM:NOPE_DIM + half]], axis=2)
    row = lambda v: v.reshape(1, -1).astype(F32)
    return {
        "norm_mix": row(norm_mix[layer]), "w_in": w_cat,
        "q_norm": row(q_norm[layer]), "w_q": wq.reshape(Q_LORA, MLA_HEADS * Q_HEAD_COLS).astype(BF16),
        "kv_norm": row(kv_norm[layer]), "w_kv": w_kv_up[layer].astype(BF16),
        "dt_bias": jnp.pad(row(dt_bias[layer]), ((0, 0), (0, LANES - SSD_HEADS))),
        "conv_w": conv_w[layer], "conv_b": row(conv_b[layer]), "a_log": row(a_log[layer]),
        "d_skip": row(jnp.repeat(d_skip[layer], SSD_HEADDIM)), "ssd_norm": row(ssd_norm[layer]),
        "w_out": w_out[layer].astype(BF16), "norm_ffn": row(norm_ffn[layer]),
    }


def _rope_tables(pos, tm):
    half = ROPE_DIM // 2
    inv_freq = ROPE_THETA ** (-jnp.arange(half, dtype=F32) / half)
    ang = pos.astype(F32)[:, None] * inv_freq[None, :]
    cos, sin = jnp.cos(ang), jnp.sin(ang)
    cos_tab = jnp.concatenate([cos, cos], axis=1)
    sin_tab = jnp.concatenate([-sin, sin], axis=1)
    reps = max(1, tm // pos.shape[0])
    return jnp.tile(cos_tab, (reps, 1)), jnp.tile(sin_tab, (reps, 1))


def _mixing(x, bsz, length, lw, tables, conv0, ssm0, past_k, past_v, past_len):
    n = bsz * length
    tm = min(TOKEN_TILE, n)
    z, xbc, dt, c_kv, k_rope, q, k, v = _inproj(x, lw, tables[0], tables[1], tm)
    ys, new_ssm, new_conv = _ssd(xbc, dt, z, lw, bsz, length, conv0, ssm0)
    if past_k is None:
        ym = _attention_prompt(q, k, v, bsz, length)
    else:
        ym = _attention_cache(q, k, v, past_k, past_v, bsz, length, past_len)
    return ys, ym, c_kv, k_rope, new_ssm, new_conv


def kernel(x_prompt, x_sample, cache_kv_latent, cache_k_rope, state_ssm, state_conv, norm_mix, w_in, conv_w,
           conv_b, dt_bias, a_log, d_skip, ssd_norm, q_norm, w_q_up, kv_norm, w_kv_up, w_out, norm_ffn,
           ffn_w_gate, ffn_w_up, ffn_w_down, moe_router, moe_w_gate, moe_w_up, moe_w_down, final_norm):
    bp, sp = x_prompt.shape[:2]
    bs, ss = x_sample.shape[:2]
    depth = w_in.shape[0]
    past = cache_kv_latent.shape[2]
    tabs_p = _rope_tables(jnp.arange(sp, dtype=I32), min(TOKEN_TILE, bp * sp))
    tabs_s = _rope_tables(past + jnp.arange(ss, dtype=I32), min(TOKEN_TILE, bs * ss))
    final_w = final_norm.reshape(1, D_MODEL)
    hp = x_prompt.reshape(bp * sp, D_MODEL)
    hs = x_sample.reshape(bs * ss, D_MODEL)
    outs_p, outs_s = [], []
    for layer in range(depth):
        lw = _prep_layer(layer, norm_mix, w_in, conv_w, conv_b, dt_bias, a_log, d_skip, ssd_norm, q_norm,
                         w_q_up, kv_norm, w_kv_up, w_out, norm_ffn)
        last = final_w if layer == depth - 1 else None
        past_k, past_v = _pastkv(cache_kv_latent[layer].reshape(bs * past, KV_LORA),
                                 cache_k_rope[layer].reshape(bs * past, ROPE_DIM), lw["w_kv"],
                                 min(TOKEN_TILE, bs * past))
        mix_p = _mixing(hp, bp, sp, lw, tabs_p, None, None, None, None, 0)
        mix_s = _mixing(hs, bs, ss, lw, tabs_s, state_conv[layer], state_ssm[layer], past_k, past_v, past)
        outs_p.append(mix_p[2:])
        outs_s.append(mix_s[2:])
        idx = layer // 2
        if layer % 2 == 0:
            fw = {"w_gate": ffn_w_gate[idx].astype(BF16), "w_up": ffn_w_up[idx].astype(BF16),
                  "w_down": ffn_w_down[idx].astype(BF16)}
            hp = _dense_layer(hp, mix_p[0], mix_p[1], lw, fw, last, min(TOKEN_TILE, bp * sp))
            hs = _dense_layer(hs, mix_s[0], mix_s[1], lw, fw, last, min(TOKEN_TILE, bs * ss))
        else:
            wr_hi = moe_router[idx].astype(BF16)
            wr_lo = (moe_router[idx] - wr_hi.astype(F32)).astype(BF16)
            mw = {"w_router": jnp.pad(jnp.concatenate([wr_hi, wr_lo], axis=1),
                                      ((0, 0), (0, LANES - 2 * N_EXPERTS))),
                  "w_gate": moe_w_gate[idx].astype(BF16), "w_up": moe_w_up[idx].astype(BF16),
                  "w_down": moe_w_down[idx].astype(BF16)}
            hp = _moe_layer(hp, mix_p[0], mix_p[1], lw, mw, last)
            hs = _moe_layer(hs, mix_s[0], mix_s[1], lw, mw, last)

    def stack(outs, j, shape):
        return jnp.stack([o[j].reshape(shape) for o in outs])

    return (hp.reshape(bp, sp, D_MODEL), hs.reshape(bs, ss, D_MODEL),
            stack(outs_p, 0, (bp, sp, KV_LORA)), stack(outs_p, 1, (bp, sp, ROPE_DIM)),
            stack(outs_p, 2, (bp, SSD_HEADS, SSD_HEADDIM, SSD_STATE)),
            stack(outs_p, 3, (bp, CONV_WIDTH - 1, CONV_DIM)),
            stack(outs_s, 0, (bs, ss, KV_LORA)), stack(outs_s, 1, (bs, ss, ROPE_DIM)),
            stack(outs_s, 2, (bs, SSD_HEADS, SSD_HEADDIM, SSD_STATE)),
            stack(outs_s, 3, (bs, CONV_WIDTH - 1, CONV_DIM)))
```

```python
import functools
import math

import jax
import jax.numpy as jnp
import numpy as np
from jax import lax
from jax.experimental import pallas as pl
from jax.experimental.pallas import tpu as pltpu

F32 = jnp.float32
BF16 = jnp.bfloat16
I32 = jnp.int32

D_MODEL = 1024
CHUNK = 64
LOG2_CHUNK = 6
SSD_HEADS = 16
SSD_HEADDIM = 64
SSD_INNER = SSD_HEADS * SSD_HEADDIM
SSD_GROUPS = 2
HEADS_PER_GROUP = SSD_HEADS // SSD_GROUPS
SSD_STATE = 128
CONV_WIDTH = 4
CONV_DIM = SSD_INNER + 2 * SSD_GROUPS * SSD_STATE
MLA_HEADS = 8
Q_LORA = 384
KV_LORA = 256
NOPE_DIM = 128
ROPE_DIM = 64
V_DIM = 128
QK_DIM = NOPE_DIM + ROPE_DIM
ROPE_THETA = 10000.0
MLA_INNER = MLA_HEADS * V_DIM
N_EXPERTS = 8
TOP_K = 2
EPS = 1e-6

LANES = 128
SUBLANES = 8
MXU_DIM = 256
VMEM_LIMIT = 56 * 1024 * 1024

OFF_Z = 0
OFF_XBC = OFF_Z + SSD_INNER
OFF_CQ = OFF_XBC + CONV_DIM
OFF_CKV = OFF_CQ + Q_LORA
OFF_KR = OFF_CKV + KV_LORA
OFF_KRS = OFF_KR + ROPE_DIM
OFF_DT = OFF_KRS + ROPE_DIM
IN_COLS = OFF_DT + LANES
Q_HEAD_COLS = NOPE_DIM + 2 * ROPE_DIM

TOKEN_TILE = 512
ATT_TILE = 512
ATT_HEADS_PER_STEP = 4
SSD_CHUNKS_PER_STEP = 4
MOE_TILE = 256
DISPATCH_TILE = 512
COMBINE_TILE = 256


def _params(n_axes):
    return pltpu.CompilerParams(dimension_semantics=("arbitrary",) * n_axes,
                                vmem_limit_bytes=VMEM_LIMIT)


def _const_spec(shape):
    zeros = (0,) * len(shape)
    return pl.BlockSpec(shape, lambda *_: zeros)


def _rms(x, w):
    return x * lax.rsqrt(jnp.mean(x * x, axis=-1, keepdims=True) + EPS) * w


def _silu(x):
    return x * (1.0 / (1.0 + jnp.exp(-x)))


def _dot(a, b):
    return jnp.dot(a, b, preferred_element_type=F32)


def _dot_nt(a, b):
    return lax.dot_general(a, b, (((1,), (1,)), ((), ())), preferred_element_type=F32)


def _dot_tn(a, b):
    return lax.dot_general(a, b, (((0,), (0,)), ((), ())), preferred_element_type=F32)


def _split3(x):
    hi = x.astype(BF16)
    rest = x - hi.astype(F32)
    mid = rest.astype(BF16)
    lo = (rest - mid.astype(F32)).astype(BF16)
    return jnp.concatenate([hi, mid, lo], axis=1)


ROW_TILE = (D_MODEL // LANES, LANES)


def _row_tile_spec(tm):
    return pl.BlockSpec((tm,) + ROW_TILE, lambda i, *_: (i, 0, 0))


def _store_row_tiles(ref, x):
    for s in range(ROW_TILE[0]):
        ref[:, s, :] = x[:, s * LANES:(s + 1) * LANES]


def _load_row_tiles(ref, *lead):
    return jnp.concatenate([ref[lead + (slice(None), s, slice(None))] for s in range(ROW_TILE[0])], axis=1)


def _inproj_body(x_ref, nw_ref, win_ref, qnw_ref, wq_ref, kvnw_ref, wkv_ref, cos_ref, sin_ref, dtb_ref,
                 z_ref, xbc_ref, dt_ref, ckv_ref, kr_ref, q_ref, k_ref, v_ref):
    h = _rms(x_ref[...], nw_ref[...]).astype(BF16)
    proj = _dot(h, win_ref[...])
    z_ref[...] = proj[:, OFF_Z:OFF_XBC]
    xbc_ref[...] = proj[:, OFF_XBC:OFF_CQ]
    pre = proj[:, OFF_DT:IN_COLS] + dtb_ref[...]
    dt_ref[...] = jnp.maximum(pre, 0.0) + jnp.log(1.0 + jnp.exp(-jnp.abs(pre)))
    cos = cos_ref[...]
    sin = sin_ref[...]
    k_rope = proj[:, OFF_KR:OFF_KRS] * cos + proj[:, OFF_KRS:OFF_DT] * sin
    kr_ref[...] = k_rope
    c_kv = _rms(proj[:, OFF_CKV:OFF_KR], kvnw_ref[...])
    ckv_ref[...] = c_kv
    qn = _rms(proj[:, OFF_CQ:OFF_CKV], qnw_ref[...]).astype(BF16)
    q_all = _dot(qn, wq_ref[...])
    kv_all = _dot(c_kv.astype(BF16), wkv_ref[...])
    k_rope_b = k_rope.astype(BF16)
    for hd in range(MLA_HEADS):
        qb = q_all[:, hd * Q_HEAD_COLS:(hd + 1) * Q_HEAD_COLS]
        q_rope = qb[:, NOPE_DIM:QK_DIM] * cos + qb[:, QK_DIM:Q_HEAD_COLS] * sin
        q_ref[hd] = jnp.concatenate([qb[:, :NOPE_DIM], q_rope], axis=1).astype(BF16)
        kvb = kv_all[:, hd * (NOPE_DIM + V_DIM):(hd + 1) * (NOPE_DIM + V_DIM)]
        k_ref[hd] = jnp.concatenate([kvb[:, :NOPE_DIM].astype(BF16), k_rope_b], axis=1)
        v_ref[hd] = kvb[:, NOPE_DIM:].astype(BF16)


def _inproj(x, lw, cos_tab, sin_tab, tm):
    n = x.shape[0]
    nt = n // tm
    tab_blocks = cos_tab.shape[0] // tm
    row = lambda w: pl.BlockSpec((tm, w), lambda i: (i, 0))
    head = lambda w: pl.BlockSpec((MLA_HEADS, tm, w), lambda i: (0, i, 0))
    tab = pl.BlockSpec((tm, ROPE_DIM), lambda i: (i % tab_blocks, 0))
    return pl.pallas_call(
        _inproj_body,
        grid=(nt,),
        in_specs=[row(D_MODEL), _const_spec((1, D_MODEL)), _const_spec((D_MODEL, IN_COLS)),
                  _const_spec((1, Q_LORA)), _const_spec((Q_LORA, MLA_HEADS * Q_HEAD_COLS)),
                  _const_spec((1, KV_LORA)), _const_spec((KV_LORA, MLA_HEADS * (NOPE_DIM + V_DIM))),
                  tab, tab, _const_spec((1, LANES))],
        out_specs=[row(SSD_INNER), row(CONV_DIM), row(LANES), row(KV_LORA), row(ROPE_DIM),
                   head(QK_DIM), head(QK_DIM), head(V_DIM)],
        out_shape=[jax.ShapeDtypeStruct((n, SSD_INNER), F32), jax.ShapeDtypeStruct((n, CONV_DIM), F32),
                   jax.ShapeDtypeStruct((n, LANES), F32), jax.ShapeDtypeStruct((n, KV_LORA), F32),
                   jax.ShapeDtypeStruct((n, ROPE_DIM), F32),
                   jax.ShapeDtypeStruct((MLA_HEADS, n, QK_DIM), BF16),
                   jax.ShapeDtypeStruct((MLA_HEADS, n, QK_DIM), BF16),
                   jax.ShapeDtypeStruct((MLA_HEADS, n, V_DIM), BF16)],
        compiler_params=_params(1),
        name="inproj",
    )(x, lw["norm_mix"], lw["w_in"], lw["q_norm"], lw["w_q"], lw["kv_norm"], lw["w_kv"],
      cos_tab, sin_tab, lw["dt_bias"])


def _pastkv_body(lat_ref, kr_ref, wkv_ref, k_ref, v_ref):
    kv_all = _dot(lat_ref[...].astype(BF16), wkv_ref[...])
    k_rope_b = kr_ref[...].astype(BF16)
    for hd in range(MLA_HEADS):
        kvb = kv_all[:, hd * (NOPE_DIM + V_DIM):(hd + 1) * (NOPE_DIM + V_DIM)]
        k_ref[hd] = jnp.concatenate([kvb[:, :NOPE_DIM].astype(BF16), k_rope_b], axis=1)
        v_ref[hd] = kvb[:, NOPE_DIM:].astype(BF16)


def _pastkv(lat, kr, w_kv, tm):
    n = lat.shape[0]
    row = lambda w: pl.BlockSpec((tm, w), lambda i: (i, 0))
    head = lambda w: pl.BlockSpec((MLA_HEADS, tm, w), lambda i: (0, i, 0))
    return pl.pallas_call(
        _pastkv_body,
        grid=(n // tm,),
        in_specs=[row(KV_LORA), row(ROPE_DIM), _const_spec(w_kv.shape)],
        out_specs=[head(QK_DIM), head(V_DIM)],
        out_shape=[jax.ShapeDtypeStruct((MLA_HEADS, n, QK_DIM), BF16),
                   jax.ShapeDtypeStruct((MLA_HEADS, n, V_DIM), BF16)],
        compiler_params=_params(1),
        name="pastkv",
    )(lat, kr, w_kv)


def _ssd_body(t, cps, has_init, *refs):
    if has_init:
        (xbc_ref, dt_ref, z_ref, cw_ref, cb_ref, alog_ref, dskip_ref, nw_ref, conv0_ref, ssm0_ref,
         y_ref, ssm_ref, conv_ref, hist_ref) = refs
    else:
        (xbc_ref, dt_ref, z_ref, cw_ref, cb_ref, alog_ref, dskip_ref, nw_ref,
         y_ref, ssm_ref, conv_ref, hist_ref) = refs
    rows = cps * t
    hist_rows = SUBLANES
    first = hist_rows - (CONV_WIDTH - 1)
    group_lanes = HEADS_PER_GROUP * t
    group_cols = HEADS_PER_GROUP * SSD_HEADDIM
    heads_per_block = MXU_DIM // t
    blocks_per_group = group_lanes // MXU_DIM
    log2_t = int(math.log2(t))
    log2_p = int(math.log2(SSD_HEADDIM))

    @pl.when(pl.program_id(1) == 0)
    def _():
        hist_ref[...] = jnp.zeros_like(hist_ref)
        if has_init:
            hist_ref[first:hist_rows, :] = conv0_ref[...]
            ssm_ref[...] = ssm0_ref[...]
        else:
            ssm_ref[...] = jnp.zeros_like(ssm_ref)

    x_raw = xbc_ref[...]
    hist_ref[hist_rows:, :] = x_raw
    padded = hist_ref[...]
    conv = cb_ref[...]
    for k in range(CONV_WIDTH):
        back = CONV_WIDTH - 1 - k
        tap = x_raw if back == 0 else pltpu.roll(padded, back, axis=0)[hist_rows:, :]
        conv = conv + tap * cw_ref[k:k + 1, :]
    hist_ref[first:hist_rows, :] = x_raw[rows - (CONV_WIDTH - 1):rows]
    conv_ref[...] = x_raw[rows - (CONV_WIDTH - 1):rows]
    act = _silu(conv)
    xs = act[:, :SSD_INNER]
    b_mat = act[:, SSD_INNER:SSD_INNER + SSD_GROUPS * SSD_STATE].astype(BF16)
    c_mat = act[:, SSD_INNER + SSD_GROUPS * SSD_STATE:].astype(BF16)

    dt = dt_ref[:, :SSD_HEADS]
    da = dt * (-jnp.exp(alog_ref[...]))
    ri = lax.broadcasted_iota(I32, (rows, rows), 0)
    ci = lax.broadcasted_iota(I32, (rows, rows), 1)
    tril = ((ri >= ci) & ((ri >> log2_t) == (ci >> log2_t))).astype(BF16)
    cs3 = _dot(tril, _split3(da))
    cs = cs3[:, :SSD_HEADS] + cs3[:, SSD_HEADS:2 * SSD_HEADS] + cs3[:, 2 * SSD_HEADS:]

    def expander(width, shift):
        e_rows = lax.broadcasted_iota(I32, (3 * SSD_HEADS, width), 0)
        return ((lax.broadcasted_iota(I32, (3 * SSD_HEADS, width), 1) >> shift) == (e_rows & (SSD_HEADS - 1))
                ).astype(BF16)

    per_p = _dot(_split3(jnp.concatenate([cs, dt], axis=0)), expander(SSD_INNER, log2_p))
    cs_p, dt_p = per_p[:rows], per_p[rows:]
    cs_t = cs_p if t == SSD_HEADDIM else _dot(_split3(cs), expander(SSD_HEADS * t, log2_t))
    xdt = xs * dt_p
    lane = lax.broadcasted_iota(I32, (t, SSD_HEADS * t), 1) & (t - 1)
    rowi = lax.broadcasted_iota(I32, (t, SSD_HEADS * t), 0)
    block_mask = (lax.broadcasted_iota(I32, (MXU_DIM, heads_per_block * SSD_HEADDIM), 0) >> log2_t) == \
        (lax.broadcasted_iota(I32, (MXU_DIM, heads_per_block * SSD_HEADDIM), 1) >> log2_p)

    y_chunks = []
    for ch in range(cps):
        r0, r1 = ch * t, (ch + 1) * t
        cs_t_c, cs_p_c = cs_t[r0:r1], cs_p[r0:r1]
        cs_src = jnp.sum(jnp.where(lane == rowi, cs_t_c, 0.0), axis=0, keepdims=True)
        decay = jnp.exp(jnp.where(rowi >= lane, cs_t_c - cs_src, -jnp.inf))
        exp_cs = jnp.exp(cs_p_c)
        x_end = (xdt[r0:r1] * jnp.exp(cs_p_c[t - 1:t, :] - cs_p_c)).astype(BF16)
        chunk_decay = jnp.exp(cs[r1 - 1:r1, :])
        y_blocks = []
        for g in range(SSD_GROUPS):
            bg = b_mat[r0:r1, g * SSD_STATE:(g + 1) * SSD_STATE]
            cg = c_mat[r0:r1, g * SSD_STATE:(g + 1) * SSD_STATE]
            cb = _dot_nt(cg, jnp.concatenate([bg] * HEADS_PER_GROUP, axis=0))
            m = (cb * decay[:, g * group_lanes:(g + 1) * group_lanes]).astype(BF16)
            h0, h1 = g * HEADS_PER_GROUP, (g + 1) * HEADS_PER_GROUP
            state = ssm_ref[h0:h1].reshape(group_cols, SSD_STATE)
            y_off = _dot_nt(cg, state.astype(BF16)) * exp_cs[:, g * group_cols:(g + 1) * group_cols]
            for kb in range(blocks_per_group):
                c0 = g * group_cols + kb * heads_per_block * SSD_HEADDIM
                c1 = c0 + heads_per_block * SSD_HEADDIM
                xb = xdt[r0:r1, c0:c1]
                bd = jnp.where(block_mask, jnp.concatenate([xb] * heads_per_block, axis=0), 0.0).astype(BF16)
                y_diag = _dot(m[:, kb * MXU_DIM:(kb + 1) * MXU_DIM], bd)
                y_blocks.append(y_diag + y_off[:, c0 - g * group_cols:c1 - g * group_cols])
            local = _dot_tn(x_end[:, g * group_cols:(g + 1) * group_cols], bg)
            for r in range(HEADS_PER_GROUP):
                hd = h0 + r
                ssm_ref[hd] = ssm_ref[hd] * chunk_decay[:, hd:hd + 1] + \
                    local[r * SSD_HEADDIM:(r + 1) * SSD_HEADDIM]
        y_chunks.append(jnp.concatenate(y_blocks, axis=1))
    y = jnp.concatenate(y_chunks, axis=0) + xs * dskip_ref[...]
    y_ref[...] = _rms(y * _silu(z_ref[...]), nw_ref[...]).astype(BF16)


def _ssd(xbc, dt, z, lw, bsz, length, conv0, ssm0):
    t = CHUNK if length % CHUNK == 0 else length
    nc = length // t
    cps = max(c for c in (1, 2, SSD_CHUNKS_PER_STEP) if nc % c == 0)
    steps = nc // cps
    n = bsz * length
    has_init = conv0 is not None
    row = lambda w: pl.BlockSpec((cps * t, w), lambda b, c: (b * steps + c, 0))
    conv_spec = pl.BlockSpec((None, CONV_WIDTH - 1, CONV_DIM), lambda b, c: (b, 0, 0))
    ssm_spec = pl.BlockSpec((None, SSD_HEADS, SSD_HEADDIM, SSD_STATE), lambda b, c: (b, 0, 0, 0))
    in_specs = [row(CONV_DIM), row(LANES), row(SSD_INNER), _const_spec((CONV_WIDTH, CONV_DIM)),
                _const_spec((1, CONV_DIM)), _const_spec((1, SSD_HEADS)), _const_spec((1, SSD_INNER)),
                _const_spec((1, SSD_INNER))]
    args = [xbc, dt, z, lw["conv_w"], lw["conv_b"], lw["a_log"], lw["d_skip"], lw["ssd_norm"]]
    if has_init:
        in_specs += [conv_spec, ssm_spec]
        args += [conv0, ssm0]
    return pl.pallas_call(
        functools.partial(_ssd_body, t, cps, has_init),
        grid=(bsz, steps),
        in_specs=in_specs,
        out_specs=[row(SSD_INNER), ssm_spec, conv_spec],
        out_shape=[jax.ShapeDtypeStruct((n, SSD_INNER), BF16),
                   jax.ShapeDtypeStruct((bsz, SSD_HEADS, SSD_HEADDIM, SSD_STATE), F32),
                   jax.ShapeDtypeStruct((bsz, CONV_WIDTH - 1, CONV_DIM), F32)],
        scratch_shapes=[pltpu.VMEM((SUBLANES + cps * t, CONV_DIM), F32)],
        compiler_params=_params(2),
        name="ssd",
    )(*args)


def _lane_groups(x):
    return [x[:, g * LANES:(g + 1) * LANES] for g in range(x.shape[1] // LANES)]


def _attn_body(tq, nh, q_ref, k_ref, v_ref, o_ref, s_ref, m_ref, l_ref, acc_ref):
    qi = pl.program_id(2)
    heads = range(nh)
    q = [q_ref[h] for h in heads]
    c = (QK_DIM ** -0.5) * math.log2(math.e)

    def keys(ref, h, j):
        return ref[h, pl.ds(pl.multiple_of(j * tq, tq), tq), :]

    def lane_max(m, s):
        for part in _lane_groups(s):
            m = jnp.maximum(m, part)
        return m

    m_ref[...] = jnp.full_like(m_ref, -jnp.inf)

    @pl.loop(0, qi)
    def _(j):
        for h in heads:
            s = _dot_nt(q[h], keys(k_ref, h, j)) * c
            s_ref[h, j] = s
            m_ref[h] = lane_max(m_ref[h], s)

    rows = lax.broadcasted_iota(I32, (tq, tq), 0) >> LOG2_CHUNK
    cols = lax.broadcasted_iota(I32, (tq, tq), 1) >> LOG2_CHUNK
    for h in heads:
        s = jnp.where(cols <= rows, _dot_nt(q[h], keys(k_ref, h, qi)) * c, -jnp.inf)
        s_ref[h, qi] = s
        row_max = jnp.max(lane_max(m_ref[h], s), axis=-1, keepdims=True)
        m_ref[h] = jnp.broadcast_to(row_max, (tq, LANES))
    l_ref[...] = jnp.zeros_like(l_ref)
    acc_ref[...] = jnp.zeros_like(acc_ref)

    @pl.loop(0, qi + 1)
    def _(j):
        for h in heads:
            mc = m_ref[h]
            parts = [jnp.exp2(part - mc) for part in _lane_groups(s_ref[h, j])]
            l_ref[h] = l_ref[h] + functools.reduce(lambda a, b: a + b, parts)
            p = jnp.concatenate(parts, axis=1).astype(BF16)
            acc_ref[h] = acc_ref[h] + _dot(p, keys(v_ref, h, j))

    o_ref[...] = jnp.concatenate(
        [acc_ref[h] / jnp.sum(l_ref[h], axis=-1, keepdims=True) for h in heads], axis=1).astype(BF16)


def _attention_prompt(q, k, v, bsz, length):
    tq = min(ATT_TILE, length)
    nh = ATT_HEADS_PER_STEP
    nq = length // tq
    n = bsz * length
    return pl.pallas_call(
        functools.partial(_attn_body, tq, nh),
        grid=(bsz, MLA_HEADS // nh, nq),
        in_specs=[pl.BlockSpec((nh, tq, QK_DIM), lambda b, h, i: (h, b * nq + i, 0)),
                  pl.BlockSpec((nh, length, QK_DIM), lambda b, h, i: (h, b, 0), pipeline_mode=pl.Buffered(1)),
                  pl.BlockSpec((nh, length, V_DIM), lambda b, h, i: (h, b, 0), pipeline_mode=pl.Buffered(1))],
        out_specs=pl.BlockSpec((tq, nh * V_DIM), lambda b, h, i: (b * nq + i, h)),
        out_shape=jax.ShapeDtypeStruct((n, MLA_INNER), BF16),
        scratch_shapes=[pltpu.VMEM((nh, nq, tq, tq), F32), pltpu.VMEM((nh, tq, LANES), F32),
                        pltpu.VMEM((nh, tq, LANES), F32), pltpu.VMEM((nh, tq, V_DIM), F32)],
        compiler_params=_params(3),
        name="attn_prompt",
    )(q, k, v)


def _attn_cache_body(past, length, q_ref, kn_ref, vn_ref, kp_ref, vp_ref, o_ref):
    scale = QK_DIM ** -0.5
    q_chunk = (past + np.arange(length)) // CHUNK
    past_visible = (np.arange(past) // CHUNK)[None, :] <= q_chunk[:, None]
    new_visible = q_chunk[None, :] <= q_chunk[:, None]
    outs = []
    for hd in range(MLA_HEADS):
        q = q_ref[hd]
        s_p = _dot_nt(q, kp_ref[hd]) * scale
        s_n = _dot_nt(q, kn_ref[hd]) * scale
        if not past_visible.all():
            rows = (past + lax.broadcasted_iota(I32, (length, past), 0)) >> LOG2_CHUNK
            cols = lax.broadcasted_iota(I32, (length, past), 1) >> LOG2_CHUNK
            s_p = jnp.where(cols <= rows, s_p, -jnp.inf)
        if not new_visible.all():
            rows = (past + lax.broadcasted_iota(I32, (length, length), 0)) >> LOG2_CHUNK
            cols = (past + lax.broadcasted_iota(I32, (length, length), 1)) >> LOG2_CHUNK
            s_n = jnp.where(cols <= rows, s_n, -jnp.inf)
        m = jnp.maximum(jnp.max(s_p, axis=-1, keepdims=True), jnp.max(s_n, axis=-1, keepdims=True))
        p_p = jnp.exp(s_p - m)
        p_n = jnp.exp(s_n - m)
        denom = jnp.sum(p_p, axis=-1, keepdims=True) + jnp.sum(p_n, axis=-1, keepdims=True)
        o = _dot(p_p.astype(BF16), vp_ref[hd]) + _dot(p_n.astype(BF16), vn_ref[hd])
        outs.append(o / denom)
    o_ref[...] = jnp.concatenate(outs, axis=1).astype(BF16)


def _attention_cache(q, k_new, v_new, k_past, v_past, bsz, length, past):
    n = bsz * length
    new = lambda w: pl.BlockSpec((MLA_HEADS, length, w), lambda b: (0, b, 0))
    old = lambda w: pl.BlockSpec((MLA_HEADS, past, w), lambda b: (0, b, 0))
    return pl.pallas_call(
        functools.partial(_attn_cache_body, past, length),
        grid=(bsz,),
        in_specs=[new(QK_DIM), new(QK_DIM), new(V_DIM), old(QK_DIM), old(V_DIM)],
        out_specs=pl.BlockSpec((length, MLA_INNER), lambda b: (b, 0)),
        out_shape=jax.ShapeDtypeStruct((n, MLA_INNER), BF16),
        compiler_params=_params(1),
        name="attn_cache",
    )(q, k_new, v_new, k_past, v_past)


def _mix_out(x_ref, ys_ref, ym_ref, wo_ref):
    return x_ref[...] + _dot(ys_ref[...], wo_ref[:SSD_INNER, :]) + _dot(ym_ref[...], wo_ref[SSD_INNER:, :])


def _dense_body(final, x_ref, ys_ref, ym_ref, wo_ref, nw_ref, wg_ref, wu_ref, wd_ref, *rest):
    x1 = _mix_out(x_ref, ys_ref, ym_ref, wo_ref)
    h = _rms(x1, nw_ref[...]).astype(BF16)
    act = (_silu(_dot(h, wg_ref[...])) * _dot(h, wu_ref[...])).astype(BF16)
    x2 = x1 + _dot(act, wd_ref[...])
    if final:
        fw_ref, o_ref = rest
        o_ref[...] = _rms(x2, fw_ref[...])
    else:
        (o_ref,) = rest
        o_ref[...] = x2


def _resident(shape):
    zeros = (0,) * len(shape)
    return pl.BlockSpec(shape, lambda *_: zeros, pipeline_mode=pl.Buffered(1))


def _dense_layer(x, ys, ym, lw, fw, final_w, tm):
    n = x.shape[0]
    row = lambda w: pl.BlockSpec((tm, w), lambda i: (i, 0))
    d_ff = fw["w_gate"].shape[1]
    in_specs = [row(D_MODEL), row(SSD_INNER), row(MLA_INNER), _resident((SSD_INNER + MLA_INNER, D_MODEL)),
                _const_spec((1, D_MODEL)), _resident((D_MODEL, d_ff)), _resident((D_MODEL, d_ff)),
                _resident((d_ff, D_MODEL))]
    args = [x, ys, ym, lw["w_out"], lw["norm_ffn"], fw["w_gate"], fw["w_up"], fw["w_down"]]
    if final_w is not None:
        in_specs.append(_const_spec((1, D_MODEL)))
        args.append(final_w)
    return pl.pallas_call(
        functools.partial(_dense_body, final_w is not None),
        grid=(n // tm,),
        in_specs=in_specs,
        out_specs=row(D_MODEL),
        out_shape=jax.ShapeDtypeStruct((n, D_MODEL), F32),
        compiler_params=_params(1),
        name="mixout_dense",
    )(*args)


def _router_body(tm, x_ref, ys_ref, ym_ref, wo_ref, nw_ref, wr_ref,
                 x1_ref, h_ref, ri_ref, rf_ref, cnt_ref):
    @pl.when(pl.program_id(0) == 0)
    def _():
        cnt_ref[...] = jnp.zeros_like(cnt_ref)

    x1 = _mix_out(x_ref, ys_ref, ym_ref, wo_ref)
    x1_ref[...] = x1
    h = _rms(x1, nw_ref[...])
    _store_row_tiles(h_ref, h)
    lane_i =lax.broadcasted_iota(I32, (tm, LANES), 1)
    lane = lane_i.astype(F32)
    h_hi = h.astype(BF16)
    h_lo = (h - h_hi.astype(F32)).astype(BF16)
    hi_terms = _dot(h_hi, wr_ref[...])
    logits = hi_terms + pltpu.roll(hi_terms, LANES - N_EXPERTS, axis=1) + _dot(h_lo, wr_ref[...])
    logits = jnp.where(lane_i < N_EXPERTS, logits, -jnp.inf)
    top0 = jnp.max(logits, axis=-1, keepdims=True)
    e0 = jnp.min(jnp.where(logits == top0, lane, float(LANES)), axis=-1, keepdims=True)
    rest = jnp.where(lane == e0, -jnp.inf, logits)
    top1 = jnp.max(rest, axis=-1, keepdims=True)
    e1 = jnp.min(jnp.where(rest == top1, lane, float(LANES)), axis=-1, keepdims=True)
    w1 = jnp.exp(top1 - top0)
    g0 = 1.0 / (1.0 + w1)
    g1 = w1 / (1.0 + w1)
    onehot = jnp.where((lane == e0) | (lane == e1), 1.0, 0.0)
    strict = (lax.broadcasted_iota(I32, (tm, tm), 0) > lax.broadcasted_iota(I32, (tm, tm), 1)).astype(BF16)
    before = _dot(strict, onehot.astype(BF16)) + cnt_ref[0:1, :]
    pos0 = jnp.sum(jnp.where(lane == e0, before, 0.0), axis=-1, keepdims=True)
    pos1 = jnp.sum(jnp.where(lane == e1, before, 0.0), axis=-1, keepdims=True)
    cnt_ref[...] = cnt_ref[...] + jnp.sum(onehot, axis=0, keepdims=True)
    packed = jnp.where(lane_i == 0, e0, jnp.where(lane_i == 1, e1, jnp.where(lane_i == 2, pos0, pos1)))
    ri_ref[...] = packed.astype(I32)
    rf_ref[...] = jnp.where(lane_i == 0, g0, g1)


def _router_layer(x, ys, ym, lw, mw, tm):
    n = x.shape[0]
    row = lambda w: pl.BlockSpec((tm, w), lambda i: (i, 0))
    return pl.pallas_call(
        functools.partial(_router_body, tm),
        grid=(n // tm,),
        in_specs=[row(D_MODEL), row(SSD_INNER), row(MLA_INNER), _const_spec((SSD_INNER + MLA_INNER, D_MODEL)),
                  _const_spec((1, D_MODEL)), _const_spec((D_MODEL, LANES))],
        out_specs=[row(D_MODEL), _row_tile_spec(tm), row(LANES), row(LANES), _const_spec((SUBLANES, LANES))],
        out_shape=[jax.ShapeDtypeStruct((n, D_MODEL), F32), jax.ShapeDtypeStruct((n,) + ROW_TILE, F32),
                   jax.ShapeDtypeStruct((n, LANES), I32), jax.ShapeDtypeStruct((n, LANES), F32),
                   jax.ShapeDtypeStruct((SUBLANES, LANES), F32)],
        compiler_params=_params(1),
        name="mixout_router",
    )(x, ys, ym, lw["w_out"], lw["norm_ffn"], mw["w_router"])


def _dispatch_body(tm, bounds_ref, dest_ref, h_ref, xs_ref, dest_smem, zero_ref, sem):
    @pl.when(pl.program_id(0) == 0)
    def _():
        zero_ref[...] = jnp.zeros_like(zero_ref)

        def fill(start):
            copy = pltpu.make_async_copy(zero_ref, xs_ref.at[pl.ds(pl.multiple_of(start, MOE_TILE), MOE_TILE)], sem)
            copy.start()
            copy.wait()

        for e in range(N_EXPERTS):
            pl.when(bounds_ref[e + 1] > bounds_ref[e])(lambda e=e: fill(bounds_ref[e + 1] - MOE_TILE))
            unused = bounds_ref[N_EXPERTS] + e * MOE_TILE
            pl.when(unused < xs_ref.shape[0])(lambda unused=unused: fill(unused))

    pltpu.sync_copy(dest_ref.at[0], dest_smem)

    def start(j, c):
        for k in range(TOP_K):
            pltpu.make_async_copy(h_ref.at[j], xs_ref.at[dest_smem[k, j]], sem).start(priority=k)
        return c

    lax.fori_loop(0, tm, start, 0, unroll=8)
    for k in range(TOP_K):
        pltpu.make_async_copy(h_ref, xs_ref.at[pl.ds(0, tm)], sem).wait()


def _dispatch(h, dest, bounds, rows, tm):
    n = h.shape[0]
    return pl.pallas_call(
        functools.partial(_dispatch_body, tm),
        grid_spec=pltpu.PrefetchScalarGridSpec(
            num_scalar_prefetch=1,
            grid=(n // tm,),
            in_specs=[pl.BlockSpec((1, TOP_K, tm), lambda i, b: (i, 0, 0)), _row_tile_spec(tm)],
            out_specs=pl.BlockSpec(memory_space=pl.ANY),
            scratch_shapes=[pltpu.SMEM((TOP_K, tm), I32), pltpu.VMEM((MOE_TILE,) + ROW_TILE, F32),
                            pltpu.SemaphoreType.DMA],
        ),
        out_shape=jax.ShapeDtypeStruct((rows,) + ROW_TILE, F32),
        compiler_params=_params(1),
        name="dispatch",
    )(bounds, dest, h)


def _experts_body(tm, te_ref, nv_ref, x_ref, wg_ref, wu_ref, wd_ref, y_ref):
    valid = nv_ref[pl.program_id(0)]

    @pl.when(valid > 0)
    def _():
        x = _load_row_tiles(x_ref).astype(BF16)
        act = (_silu(_dot(x, wg_ref[...])) * _dot(x, wu_ref[...])).astype(BF16)
        _store_row_tiles(y_ref, _dot(act, wd_ref[...]))

    @pl.when(valid == 0)
    def _():
        y_ref[...] = jnp.zeros_like(y_ref)


def _experts(xs, tile_expert, tile_valid, mw, tm):
    rows = xs.shape[0]
    d_ff = mw["w_gate"].shape[2]
    wspec = lambda a, b: pl.BlockSpec((None, a, b), lambda i, te, nv: (te[i], 0, 0),
                                      pipeline_mode=pl.Buffered(1))
    return pl.pallas_call(
        functools.partial(_experts_body, tm),
        grid_spec=pltpu.PrefetchScalarGridSpec(
            num_scalar_prefetch=2,
            grid=(rows // tm,),
            in_specs=[_row_tile_spec(tm), wspec(D_MODEL, d_ff), wspec(D_MODEL, d_ff), wspec(d_ff, D_MODEL)],
            out_specs=_row_tile_spec(tm),
        ),
        out_shape=jax.ShapeDtypeStruct((rows,) + ROW_TILE, F32),
        compiler_params=_params(1),
        name="experts",
    )(tile_expert, tile_valid, xs, mw["w_gate"], mw["w_up"], mw["w_down"])


def _combine_body(tm, final, dest_ref, dnext_ref, x1_ref, rf_ref, ys_ref, *rest):
    if final:
        fw_ref, o_ref, buf_ref, dest_smem, sems = rest
    else:
        o_ref, buf_ref, dest_smem, sems = rest
    i = pl.program_id(0)
    nsteps = pl.num_programs(0)
    slot = i % 2

    def gather(src_ref, s):
        pltpu.sync_copy(src_ref.at[0], dest_smem)

        def start(j, c):
            for k in range(TOP_K):
                pltpu.make_async_copy(ys_ref.at[dest_smem[k, j]], buf_ref.at[s, k, j], sems.at[s]).start(
                    priority=k)
            return c

        lax.fori_loop(0, tm, start, 0, unroll=8)

    @pl.when(i == 0)
    def _():
        gather(dest_ref, 0)

    for k in range(TOP_K):
        pltpu.make_async_copy(ys_ref.at[pl.ds(0, tm)], buf_ref.at[slot, k], sems.at[slot]).wait()

    @pl.when(i + 1 < nsteps)
    def _():
        gather(dnext_ref, 1 - slot)

    gates = rf_ref[...]
    out = x1_ref[...] + gates[:, 0:1] * _load_row_tiles(buf_ref, slot, 0) + \
        gates[:, 1:2] * _load_row_tiles(buf_ref, slot, 1)
    if final:
        out = _rms(out, fw_ref[...])
    o_ref[...] = out


def _combine(dest, x1, rf, ys, final_w, tm):
    n = x1.shape[0]
    nt = n // tm
    row = lambda w: pl.BlockSpec((tm, w), lambda i: (i, 0))
    in_specs = [pl.BlockSpec((1, TOP_K, tm), lambda i: (i, 0, 0)),
                pl.BlockSpec((1, TOP_K, tm), lambda i: (jnp.minimum(i + 1, nt - 1), 0, 0)),
                row(D_MODEL), row(LANES), pl.BlockSpec(memory_space=pl.ANY)]
    args = [dest, dest, x1, rf, ys]
    if final_w is not None:
        in_specs.append(_const_spec((1, D_MODEL)))
        args.append(final_w)
    return pl.pallas_call(
        functools.partial(_combine_body, tm, final_w is not None),
        grid=(nt,),
        in_specs=in_specs,
        out_specs=row(D_MODEL),
        out_shape=jax.ShapeDtypeStruct((n, D_MODEL), F32),
        scratch_shapes=[pltpu.VMEM((2, TOP_K, tm) + ROW_TILE, F32), pltpu.SMEM((TOP_K, tm), I32),
                        pltpu.SemaphoreType.DMA((2,))],
        compiler_params=_params(1),
        name="combine",
    )(*args)


def _moe_layer(x, ys, ym, lw, mw, final_w):
    n = x.shape[0]
    tm = min(TOKEN_TILE, n)
    x1, h, ri, rf, counts = _router_layer(x, ys, ym, lw, mw, tm)
    counts = counts[0, :N_EXPERTS].astype(I32)
    padded = (counts + MOE_TILE - 1) // MOE_TILE * MOE_TILE
    ends = jnp.cumsum(padded)
    offsets = ends - padded
    rows = (n * TOP_K + N_EXPERTS * (MOE_TILE - 1)) // MOE_TILE * MOE_TILE
    tile_start = jnp.arange(rows // MOE_TILE, dtype=I32) * MOE_TILE
    tile_expert = jnp.minimum(jnp.sum(tile_start[:, None] >= ends[None, :], axis=1), N_EXPERTS - 1).astype(I32)
    group_end = (offsets + counts)[tile_expert]
    tile_valid = jnp.where(tile_start < ends[-1], jnp.clip(group_end - tile_start, 0, MOE_TILE), 0).astype(I32)
    dest =jnp.stack([offsets[ri[:, 0]] + ri[:, 2], offsets[ri[:, 1]] + ri[:, 3]])

    def tiles(t):
        return dest.reshape(TOP_K, n // t, t).transpose(1, 0, 2)

    td = min(DISPATCH_TILE, n)
    tc = min(COMBINE_TILE, n)
    bounds = jnp.concatenate([jnp.zeros((1,), I32), ends.astype(I32)])
    xs = _dispatch(h, tiles(td), bounds, rows, td)
    y_sorted = _experts(xs, tile_expert, tile_valid, mw, MOE_TILE)
    return _combine(tiles(tc), x1, rf, y_sorted, final_w, tc)


def _prep_layer(layer, norm_mix, w_in, conv_w, conv_b, dt_bias, a_log, d_skip, ssd_norm, q_norm, w_q_up,
                kv_norm, w_kv_up, w_out, norm_ffn):
    w = w_in[layer]
    splits = np.cumsum([SSD_INNER, CONV_DIM, SSD_HEADS, Q_LORA, KV_LORA, ROPE_DIM])[:-1].tolist()
    w_z, w_xbc, w_dt, w_cq, w_ckv, w_kr = jnp.split(w, splits, axis=1)
    half = ROPE_DIM // 2
    w_krs = jnp.concatenate([w_kr[:, half:], w_kr[:, :half]], axis=1)
    pad = jnp.zeros((D_MODEL, LANES - SSD_HEADS), F32)
    w_cat = jnp.concatenate([w_z, w_xbc, w_cq, w_ckv, w_kr, w_krs, w_dt, pad], axis=1).astype(BF16)
    wq = w_q_up[layer].reshape(Q_LORA, MLA_HEADS, QK_DIM)
    wq = jnp.concatenate([wq, wq[:, :, NOPE_DIM + half:], wq[:, :, NOPE_DIM:NOPE_DIM + half]], axis=2)
    row = lambda v: v.reshape(1, -1).astype(F32)
    return {
        "norm_mix": row(norm_mix[layer]), "w_in": w_cat,
        "q_norm": row(q_norm[layer]), "w_q": wq.reshape(Q_LORA, MLA_HEADS * Q_HEAD_COLS).astype(BF16),
        "kv_norm": row(kv_norm[layer]), "w_kv": w_kv_up[layer].astype(BF16),
        "dt_bias": jnp.pad(row(dt_bias[layer]), ((0, 0), (0, LANES - SSD_HEADS))),
        "conv_w": conv_w[layer], "conv_b": row(conv_b[layer]), "a_log": row(a_log[layer]),
        "d_skip": row(jnp.repeat(d_skip[layer], SSD_HEADDIM)), "ssd_norm": row(ssd_norm[layer]),
        "w_out": w_out[layer].astype(BF16), "norm_ffn": row(norm_ffn[layer]),
    }


def _rope_tables(pos, tm):
    half = ROPE_DIM // 2
    inv_freq = ROPE_THETA ** (-jnp.arange(half, dtype=F32) / half)
    ang = pos.astype(F32)[:, None] * inv_freq[None, :]
    cos, sin = jnp.cos(ang), jnp.sin(ang)
    cos_tab = jnp.concatenate([cos, cos], axis=1)
    sin_tab = jnp.concatenate([-sin, sin], axis=1)
    reps = max(1, tm // pos.shape[0])
    return jnp.tile(cos_tab, (reps, 1)), jnp.tile(sin_tab, (reps, 1))


def _mixing(x, bsz, length, lw, tables, conv0, ssm0, past_k, past_v, past_len):
    n = bsz * length
    tm = min(TOKEN_TILE, n)
    z, xbc, dt, c_kv, k_rope, q, k, v = _inproj(x, lw, tables[0], tables[1], tm)
    ys, new_ssm, new_conv = _ssd(xbc, dt, z, lw, bsz, length, conv0, ssm0)
    if past_k is None:
        ym = _attention_prompt(q, k, v, bsz, length)
    else:
        ym = _attention_cache(q, k, v, past_k, past_v, bsz, length, past_len)
    return ys, ym, c_kv, k_rope, new_ssm, new_conv


def kernel(x_prompt, x_sample, cache_kv_latent, cache_k_rope, state_ssm, state_conv, norm_mix, w_in, conv_w,
           conv_b, dt_bias, a_log, d_skip, ssd_norm, q_norm, w_q_up, kv_norm, w_kv_up, w_out, norm_ffn,
           ffn_w_gate, ffn_w_up, ffn_w_down, moe_router, moe_w_gate, moe_w_up, moe_w_down, final_norm):
    bp, sp = x_prompt.shape[:2]
    bs, ss = x_sample.shape[:2]
    depth = w_in.shape[0]
    past = cache_kv_latent.shape[2]
    tabs_p = _rope_tables(jnp.arange(sp, dtype=I32), min(TOKEN_TILE, bp * sp))
    tabs_s = _rope_tables(past + jnp.arange(ss, dtype=I32), min(TOKEN_TILE, bs * ss))
    final_w = final_norm.reshape(1, D_MODEL)
    hp = x_prompt.reshape(bp * sp, D_MODEL)
    hs = x_sample.reshape(bs * ss, D_MODEL)
    outs_p, outs_s = [], []
    for layer in range(depth):
        lw = _prep_layer(layer, norm_mix, w_in, conv_w, conv_b, dt_bias, a_log, d_skip, ssd_norm, q_norm,
                         w_q_up, kv_norm, w_kv_up, w_out, norm_ffn)
        last = final_w if layer == depth - 1 else None
        past_k, past_v = _pastkv(cache_kv_latent[layer].reshape(bs * past, KV_LORA),
                                 cache_k_rope[layer].reshape(bs * past, ROPE_DIM), lw["w_kv"],
                                 min(TOKEN_TILE, bs * past))
        mix_p = _mixing(hp, bp, sp, lw, tabs_p, None, None, None, None, 0)
        mix_s = _mixing(hs, bs, ss, lw, tabs_s, state_conv[layer], state_ssm[layer], past_k, past_v, past)
        outs_p.append(mix_p[2:])
        outs_s.append(mix_s[2:])
        idx = layer // 2
        if layer % 2 == 0:
            fw = {"w_gate": ffn_w_gate[idx].astype(BF16), "w_up": ffn_w_up[idx].astype(BF16),
                  "w_down": ffn_w_down[idx].astype(BF16)}
            hp = _dense_layer(hp, mix_p[0], mix_p[1], lw, fw, last, min(TOKEN_TILE, bp * sp))
            hs = _dense_layer(hs, mix_s[0], mix_s[1], lw, fw, last, min(TOKEN_TILE, bs * ss))
        else:
            wr_hi = moe_router[idx].astype(BF16)
            wr_lo = (moe_router[idx] - wr_hi.astype(F32)).astype(BF16)
            mw = {"w_router": jnp.pad(jnp.concatenate([wr_hi, wr_lo], axis=1),
                                      ((0, 0), (0, LANES - 2 * N_EXPERTS))),
                  "w_gate": moe_w_gate[idx].astype(BF16), "w_up": moe_w_up[idx].astype(BF16),
                  "w_down": moe_w_down[idx].astype(BF16)}
            hp = _moe_layer(hp, mix_p[0], mix_p[1], lw, mw, last)
            hs = _moe_layer(hs, mix_s[0], mix_s[1], lw, mw, last)

    def stack(outs, j, shape):
        return jnp.stack([o[j].reshape(shape) for o in outs])

    return (hp.reshape(bp, sp, D_MODEL), hs.reshape(bs, ss, D_MODEL),
            stack(outs_p, 0, (bp, sp, KV_LORA)), stack(outs_p, 1, (bp, sp, ROPE_DIM)),
            stack(outs_p, 2, (bp, SSD_HEADS, SSD_HEADDIM, SSD_STATE)),
            stack(outs_p, 3, (bp, CONV_WIDTH - 1, CONV_DIM)),
            stack(outs_s, 0, (bs, ss, KV_LORA)), stack(outs_s, 1, (bs, ss, ROPE_DIM)),
            stack(outs_s, 2, (bs, SSD_HEADS, SSD_HEADDIM, SSD_STATE)),
            stack(outs_s, 3, (bs, CONV_WIDTH - 1, CONV_DIM)))
```

```python
import functools
import math

import jax
import jax.numpy as jnp
import numpy as np
from jax import lax
from jax.experimental import pallas as pl
from jax.experimental.pallas import tpu as pltpu

F32 = jnp.float32
BF16 = jnp.bfloat16
I32 = jnp.int32

D_MODEL = 1024
CHUNK = 64
LOG2_CHUNK = 6
SSD_HEADS = 16
SSD_HEADDIM = 64
SSD_INNER = SSD_HEADS * SSD_HEADDIM
SSD_GROUPS = 2
HEADS_PER_GROUP = SSD_HEADS // SSD_GROUPS
SSD_STATE = 128
CONV_WIDTH = 4
CONV_DIM = SSD_INNER + 2 * SSD_GROUPS * SSD_STATE
MLA_HEADS = 8
Q_LORA = 384
KV_LORA = 256
NOPE_DIM = 128
ROPE_DIM = 64
V_DIM = 128
QK_DIM = NOPE_DIM + ROPE_DIM
ROPE_THETA = 10000.0
MLA_INNER = MLA_HEADS * V_DIM
N_EXPERTS = 8
TOP_K = 2
EPS = 1e-6

LANES = 128
SUBLANES = 8
MXU_DIM = 256
VMEM_LIMIT = 56 * 1024 * 1024

OFF_Z = 0
OFF_XBC = OFF_Z + SSD_INNER
OFF_CQ = OFF_XBC + CONV_DIM
OFF_CKV = OFF_CQ + Q_LORA
OFF_KR = OFF_CKV + KV_LORA
OFF_KRS = OFF_KR + ROPE_DIM
OFF_DT = OFF_KRS + ROPE_DIM
IN_COLS = OFF_DT + LANES
Q_HEAD_COLS = NOPE_DIM + 2 * ROPE_DIM

TOKEN_TILE = 512
ATT_TILE = 512
ATT_HEADS_PER_STEP = 4
SSD_CHUNKS_PER_STEP = 4
MOE_TILE = 256
DISPATCH_TILE = 512
COMBINE_TILE = 256
COMBINE_PIECES = 8


def _params(n_axes):
    return pltpu.CompilerParams(dimension_semantics=("arbitrary",) * n_axes,
                                vmem_limit_bytes=VMEM_LIMIT)


def _const_spec(shape):
    zeros = (0,) * len(shape)
    return pl.BlockSpec(shape, lambda *_: zeros)


def _rms(x, w):
    return x * lax.rsqrt(jnp.mean(x * x, axis=-1, keepdims=True) + EPS) * w


def _silu(x):
    return x * (1.0 / (1.0 + jnp.exp(-x)))


def _dot(a, b):
    return jnp.dot(a, b, preferred_element_type=F32)


def _dot_nt(a, b):
    return lax.dot_general(a, b, (((1,), (1,)), ((), ())), preferred_element_type=F32)


def _dot_tn(a, b):
    return lax.dot_general(a, b, (((0,), (0,)), ((), ())), preferred_element_type=F32)


def _split3(x):
    hi = x.astype(BF16)
    rest = x - hi.astype(F32)
    mid = rest.astype(BF16)
    lo = (rest - mid.astype(F32)).astype(BF16)
    return jnp.concatenate([hi, mid, lo], axis=1)


ROW_TILE = (D_MODEL // LANES, LANES)


def _row_tile_spec(tm):
    return pl.BlockSpec((tm,) + ROW_TILE, lambda i, *_: (i, 0, 0))


def _store_row_tiles(ref, x):
    for s in range(ROW_TILE[0]):
        ref[:, s, :] = x[:, s * LANES:(s + 1) * LANES]


def _load_row_tiles(ref, *lead, rows=slice(None)):
    return jnp.concatenate([ref[lead + (rows, s, slice(None))] for s in range(ROW_TILE[0])], axis=1)


def _inproj_body(x_ref, nw_ref, win_ref, qnw_ref, wq_ref, kvnw_ref, wkv_ref, cos_ref, sin_ref, dtb_ref,
                 z_ref, xbc_ref, dt_ref, ckv_ref, kr_ref, q_ref, k_ref, v_ref):
    h = _rms(x_ref[...], nw_ref[...]).astype(BF16)
    proj = _dot(h, win_ref[...])
    z_ref[...] = proj[:, OFF_Z:OFF_XBC]
    xbc_ref[...] = proj[:, OFF_XBC:OFF_CQ]
    pre = proj[:, OFF_DT:IN_COLS] + dtb_ref[...]
    dt_ref[...] = jnp.maximum(pre, 0.0) + jnp.log(1.0 + jnp.exp(-jnp.abs(pre)))
    cos = cos_ref[...]
    sin = sin_ref[...]
    k_rope = proj[:, OFF_KR:OFF_KRS] * cos + proj[:, OFF_KRS:OFF_DT] * sin
    kr_ref[...] = k_rope
    c_kv = _rms(proj[:, OFF_CKV:OFF_KR], kvnw_ref[...])
    ckv_ref[...] = c_kv
    qn = _rms(proj[:, OFF_CQ:OFF_CKV], qnw_ref[...]).astype(BF16)
    q_all = _dot(qn, wq_ref[...])
    kv_all = _dot(c_kv.astype(BF16), wkv_ref[...])
    k_rope_b = k_rope.astype(BF16)
    for hd in range(MLA_HEADS):
        qb = q_all[:, hd * Q_HEAD_COLS:(hd + 1) * Q_HEAD_COLS]
        q_rope = qb[:, NOPE_DIM:QK_DIM] * cos + qb[:, QK_DIM:Q_HEAD_COLS] * sin
        q_ref[hd] = jnp.concatenate([qb[:, :NOPE_DIM], q_rope], axis=1).astype(BF16)
        kvb = kv_all[:, hd * (NOPE_DIM + V_DIM):(hd + 1) * (NOPE_DIM + V_DIM)]
        k_ref[hd] = jnp.concatenate([kvb[:, :NOPE_DIM].astype(BF16), k_rope_b], axis=1)
        v_ref[hd] = kvb[:, NOPE_DIM:].astype(BF16)


def _inproj(x, lw, cos_tab, sin_tab, tm):
    n = x.shape[0]
    nt = n // tm
    tab_blocks = cos_tab.shape[0] // tm
    row = lambda w: pl.BlockSpec((tm, w), lambda i: (i, 0))
    head = lambda w: pl.BlockSpec((MLA_HEADS, tm, w), lambda i: (0, i, 0))
    tab = pl.BlockSpec((tm, ROPE_DIM), lambda i: (i % tab_blocks, 0))
    return pl.pallas_call(
        _inproj_body,
        grid=(nt,),
        in_specs=[row(D_MODEL), _const_spec((1, D_MODEL)), _const_spec((D_MODEL, IN_COLS)),
                  _const_spec((1, Q_LORA)), _const_spec((Q_LORA, MLA_HEADS * Q_HEAD_COLS)),
                  _const_spec((1, KV_LORA)), _const_spec((KV_LORA, MLA_HEADS * (NOPE_DIM + V_DIM))),
                  tab, tab, _const_spec((1, LANES))],
        out_specs=[row(SSD_INNER), row(CONV_DIM), row(LANES), row(KV_LORA), row(ROPE_DIM),
                   head(QK_DIM), head(QK_DIM), head(V_DIM)],
        out_shape=[jax.ShapeDtypeStruct((n, SSD_INNER), F32), jax.ShapeDtypeStruct((n, CONV_DIM), F32),
                   jax.ShapeDtypeStruct((n, LANES), F32), jax.ShapeDtypeStruct((n, KV_LORA), F32),
                   jax.ShapeDtypeStruct((n, ROPE_DIM), F32),
                   jax.ShapeDtypeStruct((MLA_HEADS, n, QK_DIM), BF16),
                   jax.ShapeDtypeStruct((MLA_HEADS, n, QK_DIM), BF16),
                   jax.ShapeDtypeStruct((MLA_HEADS, n, V_DIM), BF16)],
        compiler_params=_params(1),
        name="inproj",
    )(x, lw["norm_mix"], lw["w_in"], lw["q_norm"], lw["w_q"], lw["kv_norm"], lw["w_kv"],
      cos_tab, sin_tab, lw["dt_bias"])


def _ssd_body(t, cps, has_init, *refs):
    if has_init:
        (xbc_ref, dt_ref, z_ref, cw_ref, cb_ref, alog_ref, dskip_ref, nw_ref, conv0_ref, ssm0_ref,
         y_ref, ssm_ref, conv_ref, hist_ref) = refs
    else:
        (xbc_ref, dt_ref, z_ref, cw_ref, cb_ref, alog_ref, dskip_ref, nw_ref,
         y_ref, ssm_ref, conv_ref, hist_ref) = refs
    rows = cps * t
    hist_rows = SUBLANES
    first = hist_rows - (CONV_WIDTH - 1)
    group_lanes = HEADS_PER_GROUP * t
    group_cols = HEADS_PER_GROUP * SSD_HEADDIM
    heads_per_block = MXU_DIM // t
    blocks_per_group = group_lanes // MXU_DIM
    log2_t = int(math.log2(t))
    log2_p = int(math.log2(SSD_HEADDIM))

    @pl.when(pl.program_id(1) == 0)
    def _():
        hist_ref[...] = jnp.zeros_like(hist_ref)
        if has_init:
            hist_ref[first:hist_rows, :] = conv0_ref[...]
            ssm_ref[...] = ssm0_ref[...]
        else:
            ssm_ref[...] = jnp.zeros_like(ssm_ref)

    x_raw = xbc_ref[...]
    hist_ref[hist_rows:, :] = x_raw
    padded = hist_ref[...]
    conv = cb_ref[...]
    for k in range(CONV_WIDTH):
        back = CONV_WIDTH - 1 - k
        tap = x_raw if back == 0 else pltpu.roll(padded, back, axis=0)[hist_rows:, :]
        conv = conv + tap * cw_ref[k:k + 1, :]
    hist_ref[first:hist_rows, :] = x_raw[rows - (CONV_WIDTH - 1):rows]
    conv_ref[...] = x_raw[rows - (CONV_WIDTH - 1):rows]
    act = _silu(conv)
    xs = act[:, :SSD_INNER]
    b_mat = act[:, SSD_INNER:SSD_INNER + SSD_GROUPS * SSD_STATE].astype(BF16)
    c_mat = act[:, SSD_INNER + SSD_GROUPS * SSD_STATE:].astype(BF16)

    dt = dt_ref[:, :SSD_HEADS]
    da = dt * (-jnp.exp(alog_ref[...]))
    ri = lax.broadcasted_iota(I32, (rows, rows), 0)
    ci = lax.broadcasted_iota(I32, (rows, rows), 1)
    tril = ((ri >= ci) & ((ri >> log2_t) == (ci >> log2_t))).astype(BF16)
    cs3 = _dot(tril, _split3(da))
    cs = cs3[:, :SSD_HEADS] + cs3[:, SSD_HEADS:2 * SSD_HEADS] + cs3[:, 2 * SSD_HEADS:]

    def expander(width, shift):
        e_rows = lax.broadcasted_iota(I32, (3 * SSD_HEADS, width), 0)
        return ((lax.broadcasted_iota(I32, (3 * SSD_HEADS, width), 1) >> shift) == (e_rows & (SSD_HEADS - 1))
                ).astype(BF16)

    per_p = _dot(_split3(jnp.concatenate([cs, dt], axis=0)), expander(SSD_INNER, log2_p))
    cs_p, dt_p = per_p[:rows], per_p[rows:]
    cs_t = cs_p if t == SSD_HEADDIM else _dot(_split3(cs), expander(SSD_HEADS * t, log2_t))
    xdt = xs * dt_p
    lane = lax.broadcasted_iota(I32, (t, SSD_HEADS * t), 1) & (t - 1)
    rowi = lax.broadcasted_iota(I32, (t, SSD_HEADS * t), 0)
    block_mask = (lax.broadcasted_iota(I32, (MXU_DIM, heads_per_block * SSD_HEADDIM), 0) >> log2_t) == \
        (lax.broadcasted_iota(I32, (MXU_DIM, heads_per_block * SSD_HEADDIM), 1) >> log2_p)

    y_chunks = []
    for ch in range(cps):
        r0, r1 = ch * t, (ch + 1) * t
        cs_t_c, cs_p_c = cs_t[r0:r1], cs_p[r0:r1]
        cs_src = jnp.sum(jnp.where(lane == rowi, cs_t_c, 0.0), axis=0, keepdims=True)
        decay = jnp.exp(jnp.where(rowi >= lane, cs_t_c - cs_src, -jnp.inf))
        exp_cs = jnp.exp(cs_p_c)
        x_end = (xdt[r0:r1] * jnp.exp(cs_p_c[t - 1:t, :] - cs_p_c)).astype(BF16)
        chunk_decay = jnp.exp(cs[r1 - 1:r1, :])
        y_blocks = []
        for g in range(SSD_GROUPS):
            bg = b_mat[r0:r1, g * SSD_STATE:(g + 1) * SSD_STATE]
            cg = c_mat[r0:r1, g * SSD_STATE:(g + 1) * SSD_STATE]
            cb = _dot_nt(cg, jnp.concatenate([bg] * HEADS_PER_GROUP, axis=0))
            m = (cb * decay[:, g * group_lanes:(g + 1) * group_lanes]).astype(BF16)
            h0, h1 = g * HEADS_PER_GROUP, (g + 1) * HEADS_PER_GROUP
            state = ssm_ref[h0:h1].reshape(group_cols, SSD_STATE)
            y_off = _dot_nt(cg, state.astype(BF16)) * exp_cs[:, g * group_cols:(g + 1) * group_cols]
            for kb in range(blocks_per_group):
                c0 = g * group_cols + kb * heads_per_block * SSD_HEADDIM
                c1 = c0 + heads_per_block * SSD_HEADDIM
                xb = xdt[r0:r1, c0:c1]
                bd = jnp.where(block_mask, jnp.concatenate([xb] * heads_per_block, axis=0), 0.0).astype(BF16)
                y_diag = _dot(m[:, kb * MXU_DIM:(kb + 1) * MXU_DIM], bd)
                y_blocks.append(y_diag + y_off[:, c0 - g * group_cols:c1 - g * group_cols])
            local = _dot_tn(x_end[:, g * group_cols:(g + 1) * group_cols], bg)
            for r in range(HEADS_PER_GROUP):
                hd = h0 + r
                ssm_ref[hd] = ssm_ref[hd] * chunk_decay[:, hd:hd + 1] + \
                    local[r * SSD_HEADDIM:(r + 1) * SSD_HEADDIM]
        y_chunks.append(jnp.concatenate(y_blocks, axis=1))
    y = jnp.concatenate(y_chunks, axis=0) + xs * dskip_ref[...]
    y_ref[...] = _rms(y * _silu(z_ref[...]), nw_ref[...]).astype(BF16)


def _ssd(xbc, dt, z, lw, bsz, length, conv0, ssm0):
    t = CHUNK if length % CHUNK == 0 else length
    nc = length // t
    cps = max(c for c in (1, 2, SSD_CHUNKS_PER_STEP) if nc % c == 0)
    steps = nc // cps
    n = bsz * length
    has_init = conv0 is not None
    row = lambda w: pl.BlockSpec((cps * t, w), lambda b, c: (b * steps + c, 0))
    conv_spec = pl.BlockSpec((None, CONV_WIDTH - 1, CONV_DIM), lambda b, c: (b, 0, 0))
    ssm_spec = pl.BlockSpec((None, SSD_HEADS, SSD_HEADDIM, SSD_STATE), lambda b, c: (b, 0, 0, 0))
    in_specs = [row(CONV_DIM), row(LANES), row(SSD_INNER), _const_spec((CONV_WIDTH, CONV_DIM)),
                _const_spec((1, CONV_DIM)), _const_spec((1, SSD_HEADS)), _const_spec((1, SSD_INNER)),
                _const_spec((1, SSD_INNER))]
    args = [xbc, dt, z, lw["conv_w"], lw["conv_b"], lw["a_log"], lw["d_skip"], lw["ssd_norm"]]
    if has_init:
        in_specs += [conv_spec, ssm_spec]
        args += [conv0, ssm0]
    return pl.pallas_call(
        functools.partial(_ssd_body, t, cps, has_init),
        grid=(bsz, steps),
        in_specs=in_specs,
        out_specs=[row(SSD_INNER), ssm_spec, conv_spec],
        out_shape=[jax.ShapeDtypeStruct((n, SSD_INNER), BF16),
                   jax.ShapeDtypeStruct((bsz, SSD_HEADS, SSD_HEADDIM, SSD_STATE), F32),
                   jax.ShapeDtypeStruct((bsz, CONV_WIDTH - 1, CONV_DIM), F32)],
        scratch_shapes=[pltpu.VMEM((SUBLANES + cps * t, CONV_DIM), F32)],
        compiler_params=_params(2),
        name="ssd",
    )(*args)


def _lane_groups(x):
    return [x[:, g * LANES:(g + 1) * LANES] for g in range(x.shape[1] // LANES)]


def _attn_body(tq, nh, q_ref, k_ref, v_ref, o_ref, s_ref, m_ref, acc_ref):
    qi = pl.program_id(2)
    heads = range(nh)
    q = [q_ref[h] for h in heads]
    c = (QK_DIM ** -0.5) * math.log2(math.e)

    def keys(ref, h, j):
        return ref[h, pl.ds(pl.multiple_of(j * tq, tq), tq), :]

    def lane_max(m, s):
        for part in _lane_groups(s):
            m = jnp.maximum(m, part)
        return m

    m_ref[...] = jnp.full_like(m_ref, -jnp.inf)

    @pl.loop(0, qi)
    def _(j):
        for h in heads:
            s = _dot_nt(q[h], keys(k_ref, h, j)) * c
            s_ref[h, j] = s
            m_ref[h] = lane_max(m_ref[h], s)

    rows = lax.broadcasted_iota(I32, (tq, tq), 0) >> LOG2_CHUNK
    cols = lax.broadcasted_iota(I32, (tq, tq), 1) >> LOG2_CHUNK
    for h in heads:
        s = jnp.where(cols <= rows, _dot_nt(q[h], keys(k_ref, h, qi)) * c, -jnp.inf)
        s_ref[h, qi] = s
        row_max = jnp.max(lane_max(m_ref[h], s), axis=-1, keepdims=True)
        m_ref[h] = jnp.broadcast_to(row_max, (tq, LANES))
    acc_ref[...] = jnp.zeros_like(acc_ref)
    ones_col = (lax.broadcasted_iota(I32, (tq, LANES), 1) == 0).astype(BF16)

    @pl.loop(0, qi + 1)
    def _(j):
        for h in heads:
            mc = m_ref[h]
            p = jnp.concatenate([jnp.exp2(part - mc) for part in _lane_groups(s_ref[h, j])], axis=1).astype(BF16)
            acc_ref[h] = acc_ref[h] + _dot(p, jnp.concatenate([keys(v_ref, h, j), ones_col], axis=1))

    o_ref[...] = jnp.concatenate(
        [acc_ref[h, :, :V_DIM] / acc_ref[h, :, V_DIM:V_DIM + 1] for h in heads], axis=1).astype(BF16)


def _attention_prompt(q, k, v, bsz, length):
    tq = min(ATT_TILE, length)
    nh = ATT_HEADS_PER_STEP
    nq = length // tq
    n = bsz * length
    return pl.pallas_call(
        functools.partial(_attn_body, tq, nh),
        grid=(bsz, MLA_HEADS // nh, nq),
        in_specs=[pl.BlockSpec((nh, tq, QK_DIM), lambda b, h, i: (h, b * nq + i, 0)),
                  pl.BlockSpec((nh, length, QK_DIM), lambda b, h, i: (h, b, 0), pipeline_mode=pl.Buffered(1)),
                  pl.BlockSpec((nh, length, V_DIM), lambda b, h, i: (h, b, 0), pipeline_mode=pl.Buffered(1))],
        out_specs=pl.BlockSpec((tq, nh * V_DIM), lambda b, h, i: (b * nq + i, h)),
        out_shape=jax.ShapeDtypeStruct((n, MLA_INNER), BF16),
        scratch_shapes=[pltpu.VMEM((nh, nq, tq, tq), F32), pltpu.VMEM((nh, tq, LANES), F32),
                        pltpu.VMEM((nh, tq, V_DIM + LANES), F32)],
        compiler_params=_params(3),
        name="attn_prompt",
    )(q, k, v)


def _attn_cache_body(past, length, q_ref, kn_ref, vn_ref, lat_ref, kr_ref, wkv_ref, o_ref):
    scale = QK_DIM ** -0.5
    q_chunk = (past + np.arange(length)) // CHUNK
    past_visible = (np.arange(past) // CHUNK)[None, :] <= q_chunk[:, None]
    new_visible = q_chunk[None, :] <= q_chunk[:, None]
    kv_past = _dot(lat_ref[...].astype(BF16), wkv_ref[...])
    kr_past = kr_ref[...].astype(BF16)
    outs = []
    for hd in range(MLA_HEADS):
        q = q_ref[hd]
        kvb = kv_past[:, hd * (NOPE_DIM + V_DIM):(hd + 1) * (NOPE_DIM + V_DIM)]
        k_past = jnp.concatenate([kvb[:, :NOPE_DIM].astype(BF16), kr_past], axis=1)
        v_past = kvb[:, NOPE_DIM:].astype(BF16)
        s_p = _dot_nt(q, k_past) * scale
        s_n = _dot_nt(q, kn_ref[hd]) * scale
        if not past_visible.all():
            rows = (past + lax.broadcasted_iota(I32, (length, past), 0)) >> LOG2_CHUNK
            cols = lax.broadcasted_iota(I32, (length, past), 1) >> LOG2_CHUNK
            s_p = jnp.where(cols <= rows, s_p, -jnp.inf)
        if not new_visible.all():
            rows = (past + lax.broadcasted_iota(I32, (length, length), 0)) >> LOG2_CHUNK
            cols = (past + lax.broadcasted_iota(I32, (length, length), 1)) >> LOG2_CHUNK
            s_n = jnp.where(cols <= rows, s_n, -jnp.inf)
        m = jnp.maximum(jnp.max(s_p, axis=-1, keepdims=True), jnp.max(s_n, axis=-1, keepdims=True))
        p_p = jnp.exp(s_p - m)
        p_n = jnp.exp(s_n - m)
        denom = jnp.sum(p_p, axis=-1, keepdims=True) + jnp.sum(p_n, axis=-1, keepdims=True)
        o = _dot(p_p.astype(BF16), v_past) + _dot(p_n.astype(BF16), vn_ref[hd])
        outs.append(o / denom)
    o_ref[...] = jnp.concatenate(outs, axis=1).astype(BF16)


def _attention_cache(q, k_new, v_new, lat_past, kr_past, w_kv, bsz, length, past):
    n = bsz * length
    new = lambda w: pl.BlockSpec((MLA_HEADS, length, w), lambda b: (0, b, 0))
    old = lambda w: pl.BlockSpec((past, w), lambda b: (b, 0))
    return pl.pallas_call(
        functools.partial(_attn_cache_body, past, length),
        grid=(bsz,),
        in_specs=[new(QK_DIM), new(QK_DIM), new(V_DIM), old(KV_LORA), old(ROPE_DIM), _const_spec(w_kv.shape)],
        out_specs=pl.BlockSpec((length, MLA_INNER), lambda b: (b, 0)),
        out_shape=jax.ShapeDtypeStruct((n, MLA_INNER), BF16),
        compiler_params=_params(1),
        name="attn_cache",
    )(q, k_new, v_new, lat_past, kr_past, w_kv)


def _mix_out(x_ref, ys_ref, ym_ref, wo_ref):
    return x_ref[...] + _dot(ys_ref[...], wo_ref[:SSD_INNER, :]) + _dot(ym_ref[...], wo_ref[SSD_INNER:, :])


def _dense_body(final, x_ref, ys_ref, ym_ref, wo_ref, nw_ref, wg_ref, wu_ref, wd_ref, *rest):
    x1 = _mix_out(x_ref, ys_ref, ym_ref, wo_ref)
    h = _rms(x1, nw_ref[...]).astype(BF16)
    act = (_silu(_dot(h, wg_ref[...])) * _dot(h, wu_ref[...])).astype(BF16)
    x2 = x1 + _dot(act, wd_ref[...])
    if final:
        fw_ref, o_ref = rest
        o_ref[...] = _rms(x2, fw_ref[...])
    else:
        (o_ref,) = rest
        o_ref[...] = x2


def _resident(shape):
    zeros = (0,) * len(shape)
    return pl.BlockSpec(shape, lambda *_: zeros, pipeline_mode=pl.Buffered(1))


def _dense_layer(x, ys, ym, lw, fw, final_w, tm):
    n = x.shape[0]
    row = lambda w: pl.BlockSpec((tm, w), lambda i: (i, 0))
    d_ff = fw["w_gate"].shape[1]
    in_specs = [row(D_MODEL), row(SSD_INNER), row(MLA_INNER), _resident((SSD_INNER + MLA_INNER, D_MODEL)),
                _const_spec((1, D_MODEL)), _resident((D_MODEL, d_ff)), _resident((D_MODEL, d_ff)),
                _resident((d_ff, D_MODEL))]
    args = [x, ys, ym, lw["w_out"], lw["norm_ffn"], fw["w_gate"], fw["w_up"], fw["w_down"]]
    if final_w is not None:
        in_specs.append(_const_spec((1, D_MODEL)))
        args.append(final_w)
    return pl.pallas_call(
        functools.partial(_dense_body, final_w is not None),
        grid=(n // tm,),
        in_specs=in_specs,
        out_specs=row(D_MODEL),
        out_shape=jax.ShapeDtypeStruct((n, D_MODEL), F32),
        compiler_params=_params(1),
        name="mixout_dense",
    )(*args)


def _router_body(tm, x_ref, ys_ref, ym_ref, wo_ref, nw_ref, wr_ref,
                 x1_ref, h_ref, ri_ref, rf_ref, cnt_ref):
    @pl.when(pl.program_id(0) == 0)
    def _():
        cnt_ref[...] = jnp.zeros_like(cnt_ref)

    x1 = _mix_out(x_ref, ys_ref, ym_ref, wo_ref)
    x1_ref[...] = x1
    h = _rms(x1, nw_ref[...])
    _store_row_tiles(h_ref, h)
    lane_i =lax.broadcasted_iota(I32, (tm, LANES), 1)
    lane = lane_i.astype(F32)
    h_hi = h.astype(BF16)
    h_lo = (h - h_hi.astype(F32)).astype(BF16)
    hi_terms = _dot(h_hi, wr_ref[...])
    logits = hi_terms + pltpu.roll(hi_terms, LANES - N_EXPERTS, axis=1) + _dot(h_lo, wr_ref[...])
    logits = jnp.where(lane_i < N_EXPERTS, logits, -jnp.inf)
    top0 = jnp.max(logits, axis=-1, keepdims=True)
    e0 = jnp.min(jnp.where(logits == top0, lane, float(LANES)), axis=-1, keepdims=True)
    rest = jnp.where(lane == e0, -jnp.inf, logits)
    top1 = jnp.max(rest, axis=-1, keepdims=True)
    e1 = jnp.min(jnp.where(rest == top1, lane, float(LANES)), axis=-1, keepdims=True)
    w1 = jnp.exp(top1 - top0)
    g0 = 1.0 / (1.0 + w1)
    g1 = w1 / (1.0 + w1)
    onehot = jnp.where((lane == e0) | (lane == e1), 1.0, 0.0)
    strict = (lax.broadcasted_iota(I32, (tm, tm), 0) > lax.broadcasted_iota(I32, (tm, tm), 1)).astype(BF16)
    before = _dot(strict, onehot.astype(BF16)) + cnt_ref[0:1, :]
    pos0 = jnp.sum(jnp.where(lane == e0, before, 0.0), axis=-1, keepdims=True)
    pos1 = jnp.sum(jnp.where(lane == e1, before, 0.0), axis=-1, keepdims=True)
    cnt_ref[...] = cnt_ref[...] + jnp.sum(onehot, axis=0, keepdims=True)
    packed = jnp.where(lane_i == 0, e0, jnp.where(lane_i == 1, e1, jnp.where(lane_i == 2, pos0, pos1)))
    ri_ref[...] = packed.astype(I32)
    rf_ref[...] = jnp.where(lane_i == 0, g0, g1)


def _router_layer(x, ys, ym, lw, mw, tm):
    n = x.shape[0]
    row = lambda w: pl.BlockSpec((tm, w), lambda i: (i, 0))
    return pl.pallas_call(
        functools.partial(_router_body, tm),
        grid=(n // tm,),
        in_specs=[row(D_MODEL), row(SSD_INNER), row(MLA_INNER), _const_spec((SSD_INNER + MLA_INNER, D_MODEL)),
                  _const_spec((1, D_MODEL)), _const_spec((D_MODEL, LANES))],
        out_specs=[row(D_MODEL), _row_tile_spec(tm), row(LANES), row(LANES), _const_spec((SUBLANES, LANES))],
        out_shape=[jax.ShapeDtypeStruct((n, D_MODEL), F32), jax.ShapeDtypeStruct((n,) + ROW_TILE, F32),
                   jax.ShapeDtypeStruct((n, LANES), I32), jax.ShapeDtypeStruct((n, LANES), F32),
                   jax.ShapeDtypeStruct((SUBLANES, LANES), F32)],
        compiler_params=_params(1),
        name="mixout_router",
    )(x, ys, ym, lw["w_out"], lw["norm_ffn"], mw["w_router"])


def _dispatch_body(tm, bounds_ref, dest_ref, h_ref, xs_ref, dest_smem, zero_ref, sem):
    @pl.when(pl.program_id(0) == 0)
    def _():
        zero_ref[...] = jnp.zeros_like(zero_ref)

        def fill(start):
            copy = pltpu.make_async_copy(zero_ref, xs_ref.at[pl.ds(pl.multiple_of(start, MOE_TILE), MOE_TILE)], sem)
            copy.start()
            copy.wait()

        for e in range(N_EXPERTS):
            pl.when(bounds_ref[e + 1] > bounds_ref[e])(lambda e=e: fill(bounds_ref[e + 1] - MOE_TILE))
            unused = bounds_ref[N_EXPERTS] + e * MOE_TILE
            pl.when(unused < xs_ref.shape[0])(lambda unused=unused: fill(unused))

    pltpu.sync_copy(dest_ref.at[0], dest_smem)

    def start(j, c):
        for k in range(TOP_K):
            pltpu.make_async_copy(h_ref.at[j], xs_ref.at[dest_smem[k, j]], sem).start(priority=k)
        return c

    lax.fori_loop(0, tm, start, 0, unroll=8)
    for k in range(TOP_K):
        pltpu.make_async_copy(h_ref, xs_ref.at[pl.ds(0, tm)], sem).wait()


def _dispatch(h, dest, bounds, rows, tm):
    n = h.shape[0]
    return pl.pallas_call(
        functools.partial(_dispatch_body, tm),
        grid_spec=pltpu.PrefetchScalarGridSpec(
            num_scalar_prefetch=1,
            grid=(n // tm,),
            in_specs=[pl.BlockSpec((1, TOP_K, tm), lambda i, b: (i, 0, 0)), _row_tile_spec(tm)],
            out_specs=pl.BlockSpec(memory_space=pl.ANY),
            scratch_shapes=[pltpu.SMEM((TOP_K, tm), I32), pltpu.VMEM((MOE_TILE,) + ROW_TILE, F32),
                            pltpu.SemaphoreType.DMA],
        ),
        out_shape=jax.ShapeDtypeStruct((rows,) + ROW_TILE, F32),
        compiler_params=_params(1),
        name="dispatch",
    )(bounds, dest, h)


def _experts_body(tm, te_ref, nv_ref, x_ref, wg_ref, wu_ref, wd_ref, y_ref):
    valid = nv_ref[pl.program_id(0)]

    @pl.when(valid > 0)
    def _():
        x = _load_row_tiles(x_ref).astype(BF16)
        act = (_silu(_dot(x, wg_ref[...])) * _dot(x, wu_ref[...])).astype(BF16)
        _store_row_tiles(y_ref, _dot(act, wd_ref[...]))

    @pl.when(valid == 0)
    def _():
        y_ref[...] = jnp.zeros_like(y_ref)


def _experts(xs, tile_expert, tile_valid, mw, tm):
    rows = xs.shape[0]
    d_ff = mw["w_gate"].shape[2]
    wspec = lambda a, b: pl.BlockSpec((None, a, b), lambda i, te, nv: (te[i], 0, 0),
                                      pipeline_mode=pl.Buffered(1))
    return pl.pallas_call(
        functools.partial(_experts_body, tm),
        grid_spec=pltpu.PrefetchScalarGridSpec(
            num_scalar_prefetch=2,
            grid=(rows // tm,),
            in_specs=[_row_tile_spec(tm), wspec(D_MODEL, d_ff), wspec(D_MODEL, d_ff), wspec(d_ff, D_MODEL)],
            out_specs=_row_tile_spec(tm),
        ),
        out_shape=jax.ShapeDtypeStruct((rows,) + ROW_TILE, F32),
        compiler_params=_params(1),
        name="experts",
    )(tile_expert, tile_valid, xs, mw["w_gate"], mw["w_up"], mw["w_down"])


def _combine_body(tm, final, dest_ref, dnext_ref, x1_ref, rf_ref, ys_ref, *rest):
    if final:
        fw_ref, o_ref, buf_ref, dest_smem, sems = rest
    else:
        o_ref, buf_ref, dest_smem, sems = rest
    i = pl.program_id(0)
    nsteps = pl.num_programs(0)
    slot = i % 2

    def start_row(j, s):
        for k in range(TOP_K):
            pltpu.make_async_copy(ys_ref.at[dest_smem[k, j]], buf_ref.at[s, k, j], sems.at[s]).start(priority=k)

    def wait_rows(s):
        for k in range(TOP_K):
            pltpu.make_async_copy(ys_ref.at[pl.ds(0, tm)], buf_ref.at[s, k], sems.at[s]).wait()

    @pl.when(i == 0)
    def _():
        pltpu.sync_copy(dest_ref.at[0], dest_smem)
        lax.fori_loop(0, tm, lambda j, c: (start_row(j, 0), c)[1], 0, unroll=8)

    wait_rows(slot)
    pltpu.sync_copy(dnext_ref.at[0], dest_smem)
    gates = rf_ref[...]
    piece = tm // COMBINE_PIECES
    for p in range(COMBINE_PIECES):
        r0 = p * piece
        for j in range(r0, r0 + piece):
            start_row(j, 1 - slot)
        rows = pl.ds(r0, piece)
        out = x1_ref[rows, :] + gates[r0:r0 + piece, 0:1] * _load_row_tiles(buf_ref, slot, 0, rows=rows) + \
            gates[r0:r0 + piece, 1:2] * _load_row_tiles(buf_ref, slot, 1, rows=rows)
        if final:
            out = _rms(out, fw_ref[...])
        o_ref[rows, :] = out

    @pl.when(i == nsteps - 1)
    def _():
        wait_rows(1 - slot)


def _combine(dest, x1, rf, ys, final_w, tm):
    n = x1.shape[0]
    nt = n // tm
    row = lambda w: pl.BlockSpec((tm, w), lambda i: (i, 0))
    in_specs = [pl.BlockSpec((1, TOP_K, tm), lambda i: (i, 0, 0)),
                pl.BlockSpec((1, TOP_K, tm), lambda i: (jnp.minimum(i + 1, nt - 1), 0, 0)),
                row(D_MODEL), row(LANES), pl.BlockSpec(memory_space=pl.ANY)]
    args = [dest, dest, x1, rf, ys]
    if final_w is not None:
        in_specs.append(_const_spec((1, D_MODEL)))
        args.append(final_w)
    return pl.pallas_call(
        functools.partial(_combine_body, tm, final_w is not None),
        grid=(nt,),
        in_specs=in_specs,
        out_specs=row(D_MODEL),
        out_shape=jax.ShapeDtypeStruct((n, D_MODEL), F32),
        scratch_shapes=[pltpu.VMEM((2, TOP_K, tm) + ROW_TILE, F32), pltpu.SMEM((TOP_K, tm), I32),
                        pltpu.SemaphoreType.DMA((2,))],
        compiler_params=_params(1),
        name="combine",
    )(*args)


def _moe_layer(x, ys, ym, lw, mw, final_w):
    n = x.shape[0]
    tm = min(TOKEN_TILE, n)
    x1, h, ri, rf, counts = _router_layer(x, ys, ym, lw, mw, tm)
    counts = counts[0, :N_EXPERTS].astype(I32)
    padded = (counts + MOE_TILE - 1) // MOE_TILE * MOE_TILE
    ends = jnp.cumsum(padded)
    offsets = ends - padded
    rows = (n * TOP_K + N_EXPERTS * (MOE_TILE - 1)) // MOE_TILE * MOE_TILE
    tile_start = jnp.arange(rows // MOE_TILE, dtype=I32) * MOE_TILE
    tile_expert = jnp.minimum(jnp.sum(tile_start[:, None] >= ends[None, :], axis=1), N_EXPERTS - 1).astype(I32)
    group_end = (offsets + counts)[tile_expert]
    tile_valid = jnp.where(tile_start < ends[-1], jnp.clip(group_end - tile_start, 0, MOE_TILE), 0).astype(I32)
    dest =jnp.stack([offsets[ri[:, 0]] + ri[:, 2], offsets[ri[:, 1]] + ri[:, 3]])

    def tiles(t):
        return dest.reshape(TOP_K, n // t, t).transpose(1, 0, 2)

    td = min(DISPATCH_TILE, n)
    tc = min(COMBINE_TILE, n)
    bounds = jnp.concatenate([jnp.zeros((1,), I32), ends.astype(I32)])
    xs = _dispatch(h, tiles(td), bounds, rows, td)
    y_sorted = _experts(xs, tile_expert, tile_valid, mw, MOE_TILE)
    return _combine(tiles(tc), x1, rf, y_sorted, final_w, tc)


def _prep_layer(layer, norm_mix, w_in, conv_w, conv_b, dt_bias, a_log, d_skip, ssd_norm, q_norm, w_q_up,
                kv_norm, w_kv_up, w_out, norm_ffn):
    w = w_in[layer]
    splits = np.cumsum([SSD_INNER, CONV_DIM, SSD_HEADS, Q_LORA, KV_LORA, ROPE_DIM])[:-1].tolist()
    w_z, w_xbc, w_dt, w_cq, w_ckv, w_kr = jnp.split(w, splits, axis=1)
    half = ROPE_DIM // 2
    w_krs = jnp.concatenate([w_kr[:, half:], w_kr[:, :half]], axis=1)
    pad = jnp.zeros((D_MODEL, LANES - SSD_HEADS), F32)
    w_cat = jnp.concatenate([w_z, w_xbc, w_cq, w_ckv, w_kr, w_krs, w_dt, pad], axis=1).astype(BF16)
    wq = w_q_up[layer].reshape(Q_LORA, MLA_HEADS, QK_DIM)
    wq = jnp.concatenate([wq, wq[:, :, NOPE_DIM + half:], wq[:, :, NOPE_DIM:NOPE_DIM + half]], axis=2)
    row = lambda v: v.reshape(1, -1).astype(F32)
    return {
        "norm_mix": row(norm_mix[layer]), "w_in": w_cat,
        "q_norm": row(q_norm[layer]), "w_q": wq.reshape(Q_LORA, MLA_HEADS * Q_HEAD_COLS).astype(BF16),
        "kv_norm": row(kv_norm[layer]), "w_kv": w_kv_up[layer].astype(BF16),
        "dt_bias": jnp.pad(row(dt_bias[layer]), ((0, 0), (0, LANES - SSD_HEADS))),
        "conv_w": conv_w[layer], "conv_b": row(conv_b[layer]), "a_log": row(a_log[layer]),
        "d_skip": row(jnp.repeat(d_skip[layer], SSD_HEADDIM)), "ssd_norm": row(ssd_norm[layer]),
        "w_out": w_out[layer].astype(BF16), "norm_ffn": row(norm_ffn[layer]),
    }


def _rope_tables(pos, tm):
    half = ROPE_DIM // 2
    inv_freq = ROPE_THETA ** (-jnp.arange(half, dtype=F32) / half)
    ang = pos.astype(F32)[:, None] * inv_freq[None, :]
    cos, sin = jnp.cos(ang), jnp.sin(ang)
    cos_tab = jnp.concatenate([cos, cos], axis=1)
    sin_tab = jnp.concatenate([-sin, sin], axis=1)
    reps = max(1, tm // pos.shape[0])
    return jnp.tile(cos_tab, (reps, 1)), jnp.tile(sin_tab, (reps, 1))


def _mixing(x, bsz, length, lw, tables, conv0, ssm0, past_lat, past_kr):
    n = bsz * length
    tm = min(TOKEN_TILE, n)
    z, xbc, dt, c_kv, k_rope, q, k, v = _inproj(x, lw, tables[0], tables[1], tm)
    ys, new_ssm, new_conv = _ssd(xbc, dt, z, lw, bsz, length, conv0, ssm0)
    if past_lat is None:
        ym = _attention_prompt(q, k, v, bsz, length)
    else:
        past = past_lat.shape[1]
        ym = _attention_cache(q, k, v, past_lat.reshape(bsz * past, KV_LORA),
                              past_kr.reshape(bsz * past, ROPE_DIM), lw["w_kv"], bsz, length, past)
    return ys, ym, c_kv, k_rope, new_ssm, new_conv


def kernel(x_prompt, x_sample, cache_kv_latent, cache_k_rope, state_ssm, state_conv, norm_mix, w_in, conv_w,
           conv_b, dt_bias, a_log, d_skip, ssd_norm, q_norm, w_q_up, kv_norm, w_kv_up, w_out, norm_ffn,
           ffn_w_gate, ffn_w_up, ffn_w_down, moe_router, moe_w_gate, moe_w_up, moe_w_down, final_norm):
    bp, sp = x_prompt.shape[:2]
    bs, ss = x_sample.shape[:2]
    depth = w_in.shape[0]
    past = cache_kv_latent.shape[2]
    tabs_p = _rope_tables(jnp.arange(sp, dtype=I32), min(TOKEN_TILE, bp * sp))
    tabs_s = _rope_tables(past + jnp.arange(ss, dtype=I32), min(TOKEN_TILE, bs * ss))
    final_w = final_norm.reshape(1, D_MODEL)
    hp = x_prompt.reshape(bp * sp, D_MODEL)
    hs = x_sample.reshape(bs * ss, D_MODEL)
    outs_p, outs_s = [], []
    for layer in range(depth):
        lw = _prep_layer(layer, norm_mix, w_in, conv_w, conv_b, dt_bias, a_log, d_skip, ssd_norm, q_norm,
                         w_q_up, kv_norm, w_kv_up, w_out, norm_ffn)
        last = final_w if layer == depth - 1 else None
        mix_p = _mixing(hp, bp, sp, lw, tabs_p, None, None, None, None)
        mix_s = _mixing(hs, bs, ss, lw, tabs_s, state_conv[layer], state_ssm[layer],
                        cache_kv_latent[layer], cache_k_rope[layer])
        outs_p.append(mix_p[2:])
        outs_s.append(mix_s[2:])
        idx = layer // 2
        if layer % 2 == 0:
            fw = {"w_gate": ffn_w_gate[idx].astype(BF16), "w_up": ffn_w_up[idx].astype(BF16),
                  "w_down": ffn_w_down[idx].astype(BF16)}
            hp = _dense_layer(hp, mix_p[0], mix_p[1], lw, fw, last, min(TOKEN_TILE, bp * sp))
            hs = _dense_layer(hs, mix_s[0], mix_s[1], lw, fw, last, min(TOKEN_TILE, bs * ss))
        else:
            wr_hi = moe_router[idx].astype(BF16)
            wr_lo = (moe_router[idx] - wr_hi.astype(F32)).astype(BF16)
            mw = {"w_router": jnp.pad(jnp.concatenate([wr_hi, wr_lo], axis=1),
                                      ((0, 0), (0, LANES - 2 * N_EXPERTS))),
                  "w_gate": moe_w_gate[idx].astype(BF16), "w_up": moe_w_up[idx].astype(BF16),
                  "w_down": moe_w_down[idx].astype(BF16)}
            hp = _moe_layer(hp, mix_p[0], mix_p[1], lw, mw, last)
            hs = _moe_layer(hs, mix_s[0], mix_s[1], lw, mw, last)

    def stack(outs, j, shape):
        return jnp.stack([o[j].reshape(shape) for o in outs])

    return (hp.reshape(bp, sp, D_MODEL), hs.reshape(bs, ss, D_MODEL),
            stack(outs_p, 0, (bp, sp, KV_LORA)), stack(outs_p, 1, (bp, sp, ROPE_DIM)),
            stack(outs_p, 2, (bp, SSD_HEADS, SSD_HEADDIM, SSD_STATE)),
            stack(outs_p, 3, (bp, CONV_WIDTH - 1, CONV_DIM)),
            stack(outs_s, 0, (bs, ss, KV_LORA)), stack(outs_s, 1, (bs, ss, ROPE_DIM)),
            stack(outs_s, 2, (bs, SSD_HEADS, SSD_HEADDIM, SSD_STATE)),
            stack(outs_s, 3, (bs, CONV_WIDTH - 1, CONV_DIM)))
```

```python
import functools
import math

import jax
import jax.numpy as jnp
import numpy as np
from jax import lax
from jax.experimental import pallas as pl
from jax.experimental.pallas import tpu as pltpu

F32 = jnp.float32
BF16 = jnp.bfloat16
I32 = jnp.int32

D_MODEL = 1024
CHUNK = 64
LOG2_CHUNK = 6
SSD_HEADS = 16
SSD_HEADDIM = 64
SSD_INNER = SSD_HEADS * SSD_HEADDIM
SSD_GROUPS = 2
HEADS_PER_GROUP = SSD_HEADS // SSD_GROUPS
SSD_STATE = 128
CONV_WIDTH = 4
CONV_DIM = SSD_INNER + 2 * SSD_GROUPS * SSD_STATE
MLA_HEADS = 8
Q_LORA = 384
KV_LORA = 256
NOPE_DIM = 128
ROPE_DIM = 64
V_DIM = 128
QK_DIM = NOPE_DIM + ROPE_DIM
ROPE_THETA = 10000.0
MLA_INNER = MLA_HEADS * V_DIM
N_EXPERTS = 8
TOP_K = 2
EPS = 1e-6

LANES = 128
SUBLANES = 8
MXU_DIM = 256
VMEM_LIMIT = 56 * 1024 * 1024

OFF_Z = 0
OFF_XBC = OFF_Z + SSD_INNER
OFF_CQ = OFF_XBC + CONV_DIM
OFF_CKV = OFF_CQ + Q_LORA
OFF_KR = OFF_CKV + KV_LORA
OFF_KRS = OFF_KR + ROPE_DIM
OFF_DT = OFF_KRS + ROPE_DIM
IN_COLS = OFF_DT + LANES
Q_HEAD_COLS = NOPE_DIM + 2 * ROPE_DIM

TOKEN_TILE = 512
ATT_TILE = 512
ATT_HEADS_PER_STEP = 4
SSD_CHUNKS_PER_STEP = 4
MOE_TILE = 256
DISPATCH_TILE = 512
COMBINE_TILE = 256


def _params(n_axes):
    return pltpu.CompilerParams(dimension_semantics=("arbitrary",) * n_axes,
                                vmem_limit_bytes=VMEM_LIMIT)


def _const_spec(shape):
    zeros = (0,) * len(shape)
    return pl.BlockSpec(shape, lambda *_: zeros)


def _rms(x, w):
    return x * lax.rsqrt(jnp.mean(x * x, axis=-1, keepdims=True) + EPS) * w


def _silu(x):
    return x * (1.0 / (1.0 + jnp.exp(-x)))


def _dot(a, b):
    return jnp.dot(a, b, preferred_element_type=F32)


def _dot_nt(a, b):
    return lax.dot_general(a, b, (((1,), (1,)), ((), ())), preferred_element_type=F32)


def _dot_tn(a, b):
    return lax.dot_general(a, b, (((0,), (0,)), ((), ())), preferred_element_type=F32)


def _split3(x):
    hi = x.astype(BF16)
    rest = x - hi.astype(F32)
    mid = rest.astype(BF16)
    lo = (rest - mid.astype(F32)).astype(BF16)
    return jnp.concatenate([hi, mid, lo], axis=1)


ROW_TILE = (D_MODEL // LANES, LANES)


def _row_tile_spec(tm):
    return pl.BlockSpec((tm,) + ROW_TILE, lambda i, *_: (i, 0, 0))


def _store_row_tiles(ref, x):
    for s in range(ROW_TILE[0]):
        ref[:, s, :] = x[:, s * LANES:(s + 1) * LANES]


def _load_row_tiles(ref, *lead):
    return jnp.concatenate([ref[lead + (slice(None), s, slice(None))] for s in range(ROW_TILE[0])], axis=1)


def _inproj_body(x_ref, nw_ref, win_ref, qnw_ref, wq_ref, kvnw_ref, wkv_ref, cos_ref, sin_ref, dtb_ref,
                 z_ref, xbc_ref, dt_ref, ckv_ref, kr_ref, q_ref, k_ref, v_ref):
    h = _rms(x_ref[...], nw_ref[...]).astype(BF16)
    proj = _dot(h, win_ref[...])
    z_ref[...] = proj[:, OFF_Z:OFF_XBC]
    xbc_ref[...] = proj[:, OFF_XBC:OFF_CQ]
    pre = proj[:, OFF_DT:IN_COLS] + dtb_ref[...]
    dt_ref[...] = jnp.maximum(pre, 0.0) + jnp.log(1.0 + jnp.exp(-jnp.abs(pre)))
    cos = cos_ref[...]
    sin = sin_ref[...]
    k_rope = proj[:, OFF_KR:OFF_KRS] * cos + proj[:, OFF_KRS:OFF_DT] * sin
    kr_ref[...] = k_rope
    c_kv = _rms(proj[:, OFF_CKV:OFF_KR], kvnw_ref[...])
    ckv_ref[...] = c_kv
    qn = _rms(proj[:, OFF_CQ:OFF_CKV], qnw_ref[...]).astype(BF16)
    q_all = _dot(qn, wq_ref[...])
    kv_all = _dot(c_kv.astype(BF16), wkv_ref[...])
    k_rope_b = k_rope.astype(BF16)
    for hd in range(MLA_HEADS):
        qb = q_all[:, hd * Q_HEAD_COLS:(hd + 1) * Q_HEAD_COLS]
        q_rope = qb[:, NOPE_DIM:QK_DIM] * cos + qb[:, QK_DIM:Q_HEAD_COLS] * sin
        q_ref[hd] = jnp.concatenate([qb[:, :NOPE_DIM], q_rope], axis=1).astype(BF16)
        kvb = kv_all[:, hd * (NOPE_DIM + V_DIM):(hd + 1) * (NOPE_DIM + V_DIM)]
        k_ref[hd] = jnp.concatenate([kvb[:, :NOPE_DIM].astype(BF16), k_rope_b], axis=1)
        v_ref[hd] = kvb[:, NOPE_DIM:].astype(BF16)


def _inproj(x, lw, cos_tab, sin_tab, tm):
    n = x.shape[0]
    nt = n // tm
    tab_blocks = cos_tab.shape[0] // tm
    row = lambda w: pl.BlockSpec((tm, w), lambda i: (i, 0))
    head = lambda w: pl.BlockSpec((MLA_HEADS, tm, w), lambda i: (0, i, 0))
    tab = pl.BlockSpec((tm, ROPE_DIM), lambda i: (i % tab_blocks, 0))
    return pl.pallas_call(
        _inproj_body,
        grid=(nt,),
        in_specs=[row(D_MODEL), _const_spec((1, D_MODEL)), _const_spec((D_MODEL, IN_COLS)),
                  _const_spec((1, Q_LORA)), _const_spec((Q_LORA, MLA_HEADS * Q_HEAD_COLS)),
                  _const_spec((1, KV_LORA)), _const_spec((KV_LORA, MLA_HEADS * (NOPE_DIM + V_DIM))),
                  tab, tab, _const_spec((1, LANES))],
        out_specs=[row(SSD_INNER), row(CONV_DIM), row(LANES), row(KV_LORA), row(ROPE_DIM),
                   head(QK_DIM), head(QK_DIM), head(V_DIM)],
        out_shape=[jax.ShapeDtypeStruct((n, SSD_INNER), F32), jax.ShapeDtypeStruct((n, CONV_DIM), F32),
                   jax.ShapeDtypeStruct((n, LANES), F32), jax.ShapeDtypeStruct((n, KV_LORA), F32),
                   jax.ShapeDtypeStruct((n, ROPE_DIM), F32),
                   jax.ShapeDtypeStruct((MLA_HEADS, n, QK_DIM), BF16),
                   jax.ShapeDtypeStruct((MLA_HEADS, n, QK_DIM), BF16),
                   jax.ShapeDtypeStruct((MLA_HEADS, n, V_DIM), BF16)],
        compiler_params=_params(1),
        name="inproj",
    )(x, lw["norm_mix"], lw["w_in"], lw["q_norm"], lw["w_q"], lw["kv_norm"], lw["w_kv"],
      cos_tab, sin_tab, lw["dt_bias"])


def _ssd_body(t, cps, has_init, *refs):
    if has_init:
        (xbc_ref, dt_ref, z_ref, cw_ref, cb_ref, alog_ref, dskip_ref, nw_ref, conv0_ref, ssm0_ref,
         y_ref, ssm_ref, conv_ref, hist_ref) = refs
    else:
        (xbc_ref, dt_ref, z_ref, cw_ref, cb_ref, alog_ref, dskip_ref, nw_ref,
         y_ref, ssm_ref, conv_ref, hist_ref) = refs
    rows = cps * t
    hist_rows = SUBLANES
    first = hist_rows - (CONV_WIDTH - 1)
    group_lanes = HEADS_PER_GROUP * t
    group_cols = HEADS_PER_GROUP * SSD_HEADDIM
    heads_per_block = MXU_DIM // t
    blocks_per_group = group_lanes // MXU_DIM
    log2_t = int(math.log2(t))
    log2_p = int(math.log2(SSD_HEADDIM))

    @pl.when(pl.program_id(1) == 0)
    def _():
        hist_ref[...] = jnp.zeros_like(hist_ref)
        if has_init:
            hist_ref[first:hist_rows, :] = conv0_ref[...]
            ssm_ref[...] = ssm0_ref[...]
        else:
            ssm_ref[...] = jnp.zeros_like(ssm_ref)

    x_raw = xbc_ref[...]
    hist_ref[hist_rows:, :] = x_raw
    padded = hist_ref[...]
    conv = cb_ref[...]
    for k in range(CONV_WIDTH):
        back = CONV_WIDTH - 1 - k
        tap = x_raw if back == 0 else pltpu.roll(padded, back, axis=0)[hist_rows:, :]
        conv = conv + tap * cw_ref[k:k + 1, :]
    hist_ref[first:hist_rows, :] = x_raw[rows - (CONV_WIDTH - 1):rows]
    conv_ref[...] = x_raw[rows - (CONV_WIDTH - 1):rows]
    act = _silu(conv)
    xs = act[:, :SSD_INNER]
    b_mat = act[:, SSD_INNER:SSD_INNER + SSD_GROUPS * SSD_STATE].astype(BF16)
    c_mat = act[:, SSD_INNER + SSD_GROUPS * SSD_STATE:].astype(BF16)

    dt = dt_ref[:, :SSD_HEADS]
    da = dt * (-jnp.exp(alog_ref[...]))
    ri = lax.broadcasted_iota(I32, (rows, rows), 0)
    ci = lax.broadcasted_iota(I32, (rows, rows), 1)
    tril = ((ri >= ci) & ((ri >> log2_t) == (ci >> log2_t))).astype(BF16)
    cs3 = _dot(tril, _split3(da))
    cs = cs3[:, :SSD_HEADS] + cs3[:, SSD_HEADS:2 * SSD_HEADS] + cs3[:, 2 * SSD_HEADS:]

    def expander(width, shift):
        e_rows = lax.broadcasted_iota(I32, (3 * SSD_HEADS, width), 0)
        return ((lax.broadcasted_iota(I32, (3 * SSD_HEADS, width), 1) >> shift) == (e_rows & (SSD_HEADS - 1))
                ).astype(BF16)

    per_p = _dot(_split3(jnp.concatenate([cs, dt], axis=0)), expander(SSD_INNER, log2_p))
    cs_p, dt_p = per_p[:rows], per_p[rows:]
    cs_t = cs_p if t == SSD_HEADDIM else _dot(_split3(cs), expander(SSD_HEADS * t, log2_t))
    xdt = xs * dt_p
    lane = lax.broadcasted_iota(I32, (t, SSD_HEADS * t), 1) & (t - 1)
    rowi = lax.broadcasted_iota(I32, (t, SSD_HEADS * t), 0)
    block_mask = (lax.broadcasted_iota(I32, (MXU_DIM, heads_per_block * SSD_HEADDIM), 0) >> log2_t) == \
        (lax.broadcasted_iota(I32, (MXU_DIM, heads_per_block * SSD_HEADDIM), 1) >> log2_p)

    y_chunks = []
    for ch in range(cps):
        r0, r1 = ch * t, (ch + 1) * t
        cs_t_c, cs_p_c = cs_t[r0:r1], cs_p[r0:r1]
        cs_src = jnp.sum(jnp.where(lane == rowi, cs_t_c, 0.0), axis=0, keepdims=True)
        decay = jnp.exp(jnp.where(rowi >= lane, cs_t_c - cs_src, -jnp.inf))
        exp_cs = jnp.exp(cs_p_c)
        x_end = (xdt[r0:r1] * jnp.exp(cs_p_c[t - 1:t, :] - cs_p_c)).astype(BF16)
        chunk_decay = jnp.exp(cs[r1 - 1:r1, :])
        y_blocks = []
        for g in range(SSD_GROUPS):
            bg = b_mat[r0:r1, g * SSD_STATE:(g + 1) * SSD_STATE]
            cg = c_mat[r0:r1, g * SSD_STATE:(g + 1) * SSD_STATE]
            cb = _dot_nt(cg, jnp.concatenate([bg] * HEADS_PER_GROUP, axis=0))
            m = (cb * decay[:, g * group_lanes:(g + 1) * group_lanes]).astype(BF16)
            h0, h1 = g * HEADS_PER_GROUP, (g + 1) * HEADS_PER_GROUP
            state = ssm_ref[h0:h1].reshape(group_cols, SSD_STATE)
            y_off = _dot_nt(cg, state.astype(BF16)) * exp_cs[:, g * group_cols:(g + 1) * group_cols]
            for kb in range(blocks_per_group):
                c0 = g * group_cols + kb * heads_per_block * SSD_HEADDIM
                c1 = c0 + heads_per_block * SSD_HEADDIM
                xb = xdt[r0:r1, c0:c1]
                bd = jnp.where(block_mask, jnp.concatenate([xb] * heads_per_block, axis=0), 0.0).astype(BF16)
                y_diag = _dot(m[:, kb * MXU_DIM:(kb + 1) * MXU_DIM], bd)
                y_blocks.append(y_diag + y_off[:, c0 - g * group_cols:c1 - g * group_cols])
            local = _dot_tn(x_end[:, g * group_cols:(g + 1) * group_cols], bg)
            for r in range(HEADS_PER_GROUP):
                hd = h0 + r
                ssm_ref[hd] = ssm_ref[hd] * chunk_decay[:, hd:hd + 1] + \
                    local[r * SSD_HEADDIM:(r + 1) * SSD_HEADDIM]
        y_chunks.append(jnp.concatenate(y_blocks, axis=1))
    y = jnp.concatenate(y_chunks, axis=0) + xs * dskip_ref[...]
    y_ref[...] = _rms(y * _silu(z_ref[...]), nw_ref[...]).astype(BF16)


def _ssd(xbc, dt, z, lw, bsz, length, conv0, ssm0):
    t = CHUNK if length % CHUNK == 0 else length
    nc = length // t
    cps = max(c for c in (1, 2, SSD_CHUNKS_PER_STEP) if nc % c == 0)
    steps = nc // cps
    n = bsz * length
    has_init = conv0 is not None
    row = lambda w: pl.BlockSpec((cps * t, w), lambda b, c: (b * steps + c, 0))
    conv_spec = pl.BlockSpec((None, CONV_WIDTH - 1, CONV_DIM), lambda b, c: (b, 0, 0))
    ssm_spec = pl.BlockSpec((None, SSD_HEADS, SSD_HEADDIM, SSD_STATE), lambda b, c: (b, 0, 0, 0))
    in_specs = [row(CONV_DIM), row(LANES), row(SSD_INNER), _const_spec((CONV_WIDTH, CONV_DIM)),
                _const_spec((1, CONV_DIM)), _const_spec((1, SSD_HEADS)), _const_spec((1, SSD_INNER)),
                _const_spec((1, SSD_INNER))]
    args = [xbc, dt, z, lw["conv_w"], lw["conv_b"], lw["a_log"], lw["d_skip"], lw["ssd_norm"]]
    if has_init:
        in_specs += [conv_spec, ssm_spec]
        args += [conv0, ssm0]
    return pl.pallas_call(
        functools.partial(_ssd_body, t, cps, has_init),
        grid=(bsz, steps),
        in_specs=in_specs,
        out_specs=[row(SSD_INNER), ssm_spec, conv_spec],
        out_shape=[jax.ShapeDtypeStruct((n, SSD_INNER), BF16),
                   jax.ShapeDtypeStruct((bsz, SSD_HEADS, SSD_HEADDIM, SSD_STATE), F32),
                   jax.ShapeDtypeStruct((bsz, CONV_WIDTH - 1, CONV_DIM), F32)],
        scratch_shapes=[pltpu.VMEM((SUBLANES + cps * t, CONV_DIM), F32)],
        compiler_params=_params(2),
        name="ssd",
    )(*args)


def _lane_groups(x):
    return [x[:, g * LANES:(g + 1) * LANES] for g in range(x.shape[1] // LANES)]


def _attn_body(tq, nh, q_ref, k_ref, v_ref, o_ref, s_ref, m_ref, l_ref, acc_ref):
    qi = pl.program_id(2)
    heads = range(nh)
    q = [q_ref[h] for h in heads]
    c = (QK_DIM ** -0.5) * math.log2(math.e)

    def keys(ref, h, j):
        return ref[h, pl.ds(pl.multiple_of(j * tq, tq), tq), :]

    def lane_max(m, s):
        for part in _lane_groups(s):
            m = jnp.maximum(m, part)
        return m

    m_ref[...] = jnp.full_like(m_ref, -jnp.inf)

    @pl.loop(0, qi)
    def _(j):
        for h in heads:
            s = _dot_nt(q[h], keys(k_ref, h, j)) * c
            s_ref[h, j] = s
            m_ref[h] = lane_max(m_ref[h], s)

    rows = lax.broadcasted_iota(I32, (tq, tq), 0) >> LOG2_CHUNK
    cols = lax.broadcasted_iota(I32, (tq, tq), 1) >> LOG2_CHUNK
    for h in heads:
        s = jnp.where(cols <= rows, _dot_nt(q[h], keys(k_ref, h, qi)) * c, -jnp.inf)
        s_ref[h, qi] = s
        row_max = jnp.max(lane_max(m_ref[h], s), axis=-1, keepdims=True)
        m_ref[h] = jnp.broadcast_to(row_max, (tq, LANES))
    l_ref[...] = jnp.zeros_like(l_ref)
    acc_ref[...] = jnp.zeros_like(acc_ref)

    @pl.loop(0, qi + 1)
    def _(j):
        for h in heads:
            mc = m_ref[h]
            parts = [jnp.exp2(part - mc) for part in _lane_groups(s_ref[h, j])]
            l_ref[h] = l_ref[h] + functools.reduce(lambda a, b: a + b, parts)
            p = jnp.concatenate(parts, axis=1).astype(BF16)
            acc_ref[h] = acc_ref[h] + _dot(p, keys(v_ref, h, j))

    o_ref[...] = jnp.concatenate(
        [acc_ref[h] / jnp.sum(l_ref[h], axis=-1, keepdims=True) for h in heads], axis=1).astype(BF16)


def _attention_prompt(q, k, v, bsz, length):
    tq = min(ATT_TILE, length)
    nh = ATT_HEADS_PER_STEP
    nq = length // tq
    n = bsz * length
    return pl.pallas_call(
        functools.partial(_attn_body, tq, nh),
        grid=(bsz, MLA_HEADS // nh, nq),
        in_specs=[pl.BlockSpec((nh, tq, QK_DIM), lambda b, h, i: (h, b * nq + i, 0)),
                  pl.BlockSpec((nh, length, QK_DIM), lambda b, h, i: (h, b, 0), pipeline_mode=pl.Buffered(1)),
                  pl.BlockSpec((nh, length, V_DIM), lambda b, h, i: (h, b, 0), pipeline_mode=pl.Buffered(1))],
        out_specs=pl.BlockSpec((tq, nh * V_DIM), lambda b, h, i: (b * nq + i, h)),
        out_shape=jax.ShapeDtypeStruct((n, MLA_INNER), BF16),
        scratch_shapes=[pltpu.VMEM((nh, nq, tq, tq), F32), pltpu.VMEM((nh, tq, LANES), F32),
                        pltpu.VMEM((nh, tq, LANES), F32), pltpu.VMEM((nh, tq, V_DIM), F32)],
        compiler_params=_params(3),
        name="attn_prompt",
    )(q, k, v)


def _attn_cache_body(past, length, q_ref, kn_ref, vn_ref, lat_ref, kr_ref, wkv_ref, o_ref):
    scale = QK_DIM ** -0.5
    q_chunk = (past + np.arange(length)) // CHUNK
    past_visible = (np.arange(past) // CHUNK)[None, :] <= q_chunk[:, None]
    new_visible = q_chunk[None, :] <= q_chunk[:, None]
    kv_past = _dot(lat_ref[...].astype(BF16), wkv_ref[...])
    kr_past = kr_ref[...].astype(BF16)
    outs = []
    for hd in range(MLA_HEADS):
        q = q_ref[hd]
        kvb = kv_past[:, hd * (NOPE_DIM + V_DIM):(hd + 1) * (NOPE_DIM + V_DIM)]
        k_past = jnp.concatenate([kvb[:, :NOPE_DIM].astype(BF16), kr_past], axis=1)
        v_past = kvb[:, NOPE_DIM:].astype(BF16)
        s_p = _dot_nt(q, k_past) * scale
        s_n = _dot_nt(q, kn_ref[hd]) * scale
        if not past_visible.all():
            rows = (past + lax.broadcasted_iota(I32, (length, past), 0)) >> LOG2_CHUNK
            cols = lax.broadcasted_iota(I32, (length, past), 1) >> LOG2_CHUNK
            s_p = jnp.where(cols <= rows, s_p, -jnp.inf)
        if not new_visible.all():
            rows = (past + lax.broadcasted_iota(I32, (length, length), 0)) >> LOG2_CHUNK
            cols = (past + lax.broadcasted_iota(I32, (length, length), 1)) >> LOG2_CHUNK
            s_n = jnp.where(cols <= rows, s_n, -jnp.inf)
        m = jnp.maximum(jnp.max(s_p, axis=-1, keepdims=True), jnp.max(s_n, axis=-1, keepdims=True))
        p_p = jnp.exp(s_p - m)
        p_n = jnp.exp(s_n - m)
        denom = jnp.sum(p_p, axis=-1, keepdims=True) + jnp.sum(p_n, axis=-1, keepdims=True)
        o = _dot(p_p.astype(BF16), v_past) + _dot(p_n.astype(BF16), vn_ref[hd])
        outs.append(o / denom)
    o_ref[...] = jnp.concatenate(outs, axis=1).astype(BF16)


def _attention_cache(q, k_new, v_new, lat_past, kr_past, w_kv, bsz, length, past):
    n = bsz * length
    new = lambda w: pl.BlockSpec((MLA_HEADS, length, w), lambda b: (0, b, 0))
    old = lambda w: pl.BlockSpec((past, w), lambda b: (b, 0))
    return pl.pallas_call(
        functools.partial(_attn_cache_body, past, length),
        grid=(bsz,),
        in_specs=[new(QK_DIM), new(QK_DIM), new(V_DIM), old(KV_LORA), old(ROPE_DIM), _const_spec(w_kv.shape)],
        out_specs=pl.BlockSpec((length, MLA_INNER), lambda b: (b, 0)),
        out_shape=jax.ShapeDtypeStruct((n, MLA_INNER), BF16),
        compiler_params=_params(1),
        name="attn_cache",
    )(q, k_new, v_new, lat_past, kr_past, w_kv)


def _mix_out(x_ref, ys_ref, ym_ref, wo_ref):
    return x_ref[...] + _dot(ys_ref[...], wo_ref[:SSD_INNER, :]) + _dot(ym_ref[...], wo_ref[SSD_INNER:, :])


def _dense_body(final, x_ref, ys_ref, ym_ref, wo_ref, nw_ref, wg_ref, wu_ref, wd_ref, *rest):
    x1 = _mix_out(x_ref, ys_ref, ym_ref, wo_ref)
    h = _rms(x1, nw_ref[...]).astype(BF16)
    act = (_silu(_dot(h, wg_ref[...])) * _dot(h, wu_ref[...])).astype(BF16)
    x2 = x1 + _dot(act, wd_ref[...])
    if final:
        fw_ref, o_ref = rest
        o_ref[...] = _rms(x2, fw_ref[...])
    else:
        (o_ref,) = rest
        o_ref[...] = x2


def _resident(shape):
    zeros = (0,) * len(shape)
    return pl.BlockSpec(shape, lambda *_: zeros, pipeline_mode=pl.Buffered(1))


def _dense_layer(x, ys, ym, lw, fw, final_w, tm):
    n = x.shape[0]
    row = lambda w: pl.BlockSpec((tm, w), lambda i: (i, 0))
    d_ff = fw["w_gate"].shape[1]
    in_specs = [row(D_MODEL), row(SSD_INNER), row(MLA_INNER), _resident((SSD_INNER + MLA_INNER, D_MODEL)),
                _const_spec((1, D_MODEL)), _resident((D_MODEL, d_ff)), _resident((D_MODEL, d_ff)),
                _resident((d_ff, D_MODEL))]
    args = [x, ys, ym, lw["w_out"], lw["norm_ffn"], fw["w_gate"], fw["w_up"], fw["w_down"]]
    if final_w is not None:
        in_specs.append(_const_spec((1, D_MODEL)))
        args.append(final_w)
    return pl.pallas_call(
        functools.partial(_dense_body, final_w is not None),
        grid=(n // tm,),
        in_specs=in_specs,
        out_specs=row(D_MODEL),
        out_shape=jax.ShapeDtypeStruct((n, D_MODEL), F32),
        compiler_params=_params(1),
        name="mixout_dense",
    )(*args)


def _router_body(tm, x_ref, ys_ref, ym_ref, wo_ref, nw_ref, wr_ref,
                 x1_ref, h_ref, ri_ref, rf_ref, cnt_ref):
    @pl.when(pl.program_id(0) == 0)
    def _():
        cnt_ref[...] = jnp.zeros_like(cnt_ref)

    x1 = _mix_out(x_ref, ys_ref, ym_ref, wo_ref)
    x1_ref[...] = x1
    h = _rms(x1, nw_ref[...])
    _store_row_tiles(h_ref, h)
    lane_i = lax.broadcasted_iota(I32, (tm, LANES), 1)
    lane = lane_i.astype(F32)
    h_hi = h.astype(BF16)
    h_lo = (h - h_hi.astype(F32)).astype(BF16)
    hi_terms = _dot(h_hi, wr_ref[...])
    logits = hi_terms + pltpu.roll(hi_terms, LANES - N_EXPERTS, axis=1) + _dot(h_lo, wr_ref[...])
    logits = jnp.where(lane_i < N_EXPERTS, logits, -jnp.inf)
    top0 = jnp.max(logits, axis=-1, keepdims=True)
    e0 = jnp.min(jnp.where(logits == top0, lane, float(LANES)), axis=-1, keepdims=True)
    rest = jnp.where(lane == e0, -jnp.inf, logits)
    top1 = jnp.max(rest, axis=-1, keepdims=True)
    e1 = jnp.min(jnp.where(rest == top1, lane, float(LANES)), axis=-1, keepdims=True)
    w1 = jnp.exp(top1 - top0)
    g0 = 1.0 / (1.0 + w1)
    g1 = w1 / (1.0 + w1)
    onehot = jnp.where((lane == e0) | (lane == e1), 1.0, 0.0)
    strict = (lax.broadcasted_iota(I32, (tm, tm), 0) > lax.broadcasted_iota(I32, (tm, tm), 1)).astype(BF16)
    before = _dot(strict, onehot.astype(BF16)) + cnt_ref[0:1, :]
    pos0 = jnp.sum(jnp.where(lane == e0, before, 0.0), axis=-1, keepdims=True)
    pos1 = jnp.sum(jnp.where(lane == e1, before, 0.0), axis=-1, keepdims=True)
    cnt_ref[...] = cnt_ref[...] + jnp.sum(onehot, axis=0, keepdims=True)
    packed = jnp.where(lane_i == 0, e0, jnp.where(lane_i == 1, e1, jnp.where(lane_i == 2, pos0, pos1)))
    ri_ref[...] = packed.astype(I32)
    rf_ref[...] = jnp.where(lane_i == 0, g0, g1)


def _router_layer(x, ys, ym, lw, mw, tm):
    n = x.shape[0]
    row = lambda w: pl.BlockSpec((tm, w), lambda i: (i, 0))
    return pl.pallas_call(
        functools.partial(_router_body, tm),
        grid=(n // tm,),
        in_specs=[row(D_MODEL), row(SSD_INNER), row(MLA_INNER), _const_spec((SSD_INNER + MLA_INNER, D_MODEL)),
                  _const_spec((1, D_MODEL)), _const_spec((D_MODEL, LANES))],
        out_specs=[row(D_MODEL), _row_tile_spec(tm), row(LANES), row(LANES), _const_spec((SUBLANES, LANES))],
        out_shape=[jax.ShapeDtypeStruct((n, D_MODEL), F32), jax.ShapeDtypeStruct((n,) + ROW_TILE, F32),
                   jax.ShapeDtypeStruct((n, LANES), I32), jax.ShapeDtypeStruct((n, LANES), F32),
                   jax.ShapeDtypeStruct((SUBLANES, LANES), F32)],
        compiler_params=_params(1),
        name="mixout_router",
    )(x, ys, ym, lw["w_out"], lw["norm_ffn"], mw["w_router"])


def _dispatch_body(tm, first, bounds_ref, fill_ref, dest_ref, h_ref, *rest):
    xs_ref, dest_smem, zero_ref, sem = rest[-4:]

    def zero_padding():
        zero_ref[...] = jnp.zeros_like(zero_ref)

        def fill(start):
            copy = pltpu.make_async_copy(zero_ref, xs_ref.at[pl.ds(pl.multiple_of(start, MOE_TILE), MOE_TILE)], sem)
            copy.start()
            copy.wait()

        for e in range(N_EXPERTS):
            @pl.loop(fill_ref[e], bounds_ref[e + 1], step=MOE_TILE)
            def _(start):
                fill(start)

            unused = bounds_ref[N_EXPERTS] + e * MOE_TILE
            pl.when(unused < xs_ref.shape[0])(lambda unused=unused: fill(unused))

    if first:
        pl.when(pl.program_id(0) == 0)(zero_padding)

    pltpu.sync_copy(dest_ref.at[0], dest_smem)

    def start(j, c):
        for k in range(TOP_K):
            pltpu.make_async_copy(h_ref.at[j], xs_ref.at[dest_smem[k, j]], sem).start(priority=k)
        return c

    lax.fori_loop(0, tm, start, 0, unroll=8)
    for k in range(TOP_K):
        pltpu.make_async_copy(h_ref, xs_ref.at[pl.ds(0, tm)], sem).wait()


def _dispatch(h, dest, bounds, fill_from, rows, tm, xs_so_far):
    n = h.shape[0]
    first = xs_so_far is None
    in_specs = [pl.BlockSpec((1, TOP_K, tm), lambda i, *_: (i, 0, 0)), _row_tile_spec(tm)]
    args = [bounds, fill_from, dest, h]
    if not first:
        in_specs.append(pl.BlockSpec(memory_space=pl.ANY))
        args.append(xs_so_far)
    return pl.pallas_call(
        functools.partial(_dispatch_body, tm, first),
        grid_spec=pltpu.PrefetchScalarGridSpec(
            num_scalar_prefetch=2,
            grid=(n // tm,),
            in_specs=in_specs,
            out_specs=pl.BlockSpec(memory_space=pl.ANY),
            scratch_shapes=[pltpu.SMEM((TOP_K, tm), I32), pltpu.VMEM((MOE_TILE,) + ROW_TILE, F32),
                            pltpu.SemaphoreType.DMA],
        ),
        out_shape=jax.ShapeDtypeStruct((rows,) + ROW_TILE, F32),
        input_output_aliases={} if first else {len(args) - 1: 0},
        compiler_params=_params(1),
        name="dispatch",
    )(*args)


def _experts_body(tm, te_ref, nv_ref, x_ref, wg_ref, wu_ref, wd_ref, y_ref):
    valid = nv_ref[pl.program_id(0)]

    @pl.when(valid > 0)
    def _():
        x = _load_row_tiles(x_ref).astype(BF16)
        act = (_silu(_dot(x, wg_ref[...])) * _dot(x, wu_ref[...])).astype(BF16)
        _store_row_tiles(y_ref, _dot(act, wd_ref[...]))

    @pl.when(valid == 0)
    def _():
        y_ref[...] = jnp.zeros_like(y_ref)


def _experts(xs, tile_expert, tile_valid, mw, tm):
    rows = xs.shape[0]
    d_ff = mw["w_gate"].shape[2]
    wspec = lambda a, b: pl.BlockSpec((None, a, b), lambda i, te, nv: (te[i], 0, 0),
                                      pipeline_mode=pl.Buffered(1))
    return pl.pallas_call(
        functools.partial(_experts_body, tm),
        grid_spec=pltpu.PrefetchScalarGridSpec(
            num_scalar_prefetch=2,
            grid=(rows // tm,),
            in_specs=[_row_tile_spec(tm), wspec(D_MODEL, d_ff), wspec(D_MODEL, d_ff), wspec(d_ff, D_MODEL)],
            out_specs=_row_tile_spec(tm),
        ),
        out_shape=jax.ShapeDtypeStruct((rows,) + ROW_TILE, F32),
        compiler_params=_params(1),
        name="experts",
    )(tile_expert, tile_valid, xs, mw["w_gate"], mw["w_up"], mw["w_down"])


def _combine_body(tm, final, dest_ref, dnext_ref, x1_ref, rf_ref, ys_ref, *rest):
    if final:
        fw_ref, o_ref, buf_ref, dest_smem, sems = rest
    else:
        o_ref, buf_ref, dest_smem, sems = rest
    i = pl.program_id(0)
    nsteps = pl.num_programs(0)
    slot = i % 2

    def start_row(j, s):
        for k in range(TOP_K):
            pltpu.make_async_copy(ys_ref.at[dest_smem[k, j]], buf_ref.at[s, k, j], sems.at[s]).start(priority=k)

    def wait_rows(s):
        for k in range(TOP_K):
            pltpu.make_async_copy(ys_ref.at[pl.ds(0, tm)], buf_ref.at[s, k], sems.at[s]).wait()

    def gather(src_ref, s):
        pltpu.sync_copy(src_ref.at[0], dest_smem)
        lax.fori_loop(0, tm, lambda j, c: (start_row(j, s), c)[1], 0, unroll=8)

    @pl.when(i == 0)
    def _():
        gather(dest_ref, 0)

    wait_rows(slot)

    @pl.when(i + 1 < nsteps)
    def _():
        gather(dnext_ref, 1 - slot)

    gates = rf_ref[...]
    out = x1_ref[...] + gates[:, 0:1] * _load_row_tiles(buf_ref, slot, 0) + \
        gates[:, 1:2] * _load_row_tiles(buf_ref, slot, 1)
    if final:
        out = _rms(out, fw_ref[...])
    o_ref[...] = out


def _combine(dest, x1, rf, ys, final_w, tm):
    n = x1.shape[0]
    nt = n // tm
    row = lambda w: pl.BlockSpec((tm, w), lambda i: (i, 0))
    in_specs = [pl.BlockSpec((1, TOP_K, tm), lambda i: (i, 0, 0)),
                pl.BlockSpec((1, TOP_K, tm), lambda i: (jnp.minimum(i + 1, nt - 1), 0, 0)),
                row(D_MODEL), row(LANES), pl.BlockSpec(memory_space=pl.ANY)]
    args = [dest, dest, x1, rf, ys]
    if final_w is not None:
        in_specs.append(_const_spec((1, D_MODEL)))
        args.append(final_w)
    return pl.pallas_call(
        functools.partial(_combine_body, tm, final_w is not None),
        grid=(nt,),
        in_specs=in_specs,
        out_specs=row(D_MODEL),
        out_shape=jax.ShapeDtypeStruct((n, D_MODEL), F32),
        scratch_shapes=[pltpu.VMEM((2, TOP_K, tm) + ROW_TILE, F32), pltpu.SMEM((TOP_K, tm), I32),
                        pltpu.SemaphoreType.DMA((2,))],
        compiler_params=_params(1),
        name="combine",
    )(*args)


def _moe_layer(streams, lw, mw, final_w):
    routed = [_router_layer(x, ys, ym, lw, mw, min(TOKEN_TILE, x.shape[0])) for x, ys, ym in streams]
    stream_counts = [r[4][0, :N_EXPERTS].astype(I32) for r in routed]
    counts = functools.reduce(lambda a, b: a + b, stream_counts)
    padded = (counts + MOE_TILE - 1) // MOE_TILE * MOE_TILE
    ends = jnp.cumsum(padded)
    offsets = ends - padded
    n_total = sum(x.shape[0] for x, _, _ in streams)
    rows = (n_total * TOP_K + N_EXPERTS * (MOE_TILE - 1)) // MOE_TILE * MOE_TILE
    tile_start = jnp.arange(rows // MOE_TILE, dtype=I32) * MOE_TILE
    tile_expert = jnp.minimum(jnp.sum(tile_start[:, None] >= ends[None, :], axis=1), N_EXPERTS - 1).astype(I32)
    group_end = (offsets + counts)[tile_expert]
    tile_valid = jnp.where(tile_start < ends[-1], jnp.clip(group_end - tile_start, 0, MOE_TILE), 0).astype(I32)
    bounds = jnp.concatenate([jnp.zeros((1,), I32), ends.astype(I32)])

    def tiles(dest, t):
        return dest.reshape(TOP_K, dest.shape[1] // t, t).transpose(1, 0, 2)

    xs, dests, start = None, [], offsets
    for (x1, h, ri, rf, _), own in zip(routed, stream_counts):
        n = x1.shape[0]
        dest = jnp.stack([start[ri[:, 0]] + ri[:, 2], start[ri[:, 1]] + ri[:, 3]])
        dests.append(dest)
        td = min(DISPATCH_TILE, n)
        start = start + own
        xs = _dispatch(h, tiles(dest, td), bounds, start // MOE_TILE * MOE_TILE, rows, td, xs)
    y_sorted = _experts(xs, tile_expert, tile_valid, mw, MOE_TILE)
    outs = []
    for (x1, h, ri, rf, _), dest in zip(routed, dests):
        tc = min(COMBINE_TILE, x1.shape[0])
        outs.append(_combine(tiles(dest, tc), x1, rf, y_sorted, final_w, tc))
    return outs


def _prep_layer(layer, norm_mix, w_in, conv_w, conv_b, dt_bias, a_log, d_skip, ssd_norm, q_norm, w_q_up,
                kv_norm, w_kv_up, w_out, norm_ffn):
    w = w_in[layer]
    splits = np.cumsum([SSD_INNER, CONV_DIM, SSD_HEADS, Q_LORA, KV_LORA, ROPE_DIM])[:-1].tolist()
    w_z, w_xbc, w_dt, w_cq, w_ckv, w_kr = jnp.split(w, splits, axis=1)
    half = ROPE_DIM // 2
    w_krs = jnp.concatenate([w_kr[:, half:], w_kr[:, :half]], axis=1)
    pad = jnp.zeros((D_MODEL, LANES - SSD_HEADS), F32)
    w_cat = jnp.concatenate([w_z, w_xbc, w_cq, w_ckv, w_kr, w_krs, w_dt, pad], axis=1).astype(BF16)
    wq = w_q_up[layer].reshape(Q_LORA, MLA_HEADS, QK_DIM)
    wq = jnp.concatenate([wq, wq[:, :, NOPE_DIM + half:], wq[:, :, NOPE_DIM:NOPE_DIM + half]], axis=2)
    row = lambda v: v.reshape(1, -1).astype(F32)
    return {
        "norm_mix": row(norm_mix[layer]), "w_in": w_cat,
        "q_norm": row(q_norm[layer]), "w_q": wq.reshape(Q_LORA, MLA_HEADS * Q_HEAD_COLS).astype(BF16),
        "kv_norm": row(kv_norm[layer]), "w_kv": w_kv_up[layer].astype(BF16),
        "dt_bias": jnp.pad(row(dt_bias[layer]), ((0, 0), (0, LANES - SSD_HEADS))),
        "conv_w": conv_w[layer], "conv_b": row(conv_b[layer]), "a_log": row(a_log[layer]),
        "d_skip": row(jnp.repeat(d_skip[layer], SSD_HEADDIM)), "ssd_norm": row(ssd_norm[layer]),
        "w_out": w_out[layer].astype(BF16), "norm_ffn": row(norm_ffn[layer]),
    }


def _rope_tables(pos, tm):
    half = ROPE_DIM // 2
    inv_freq = ROPE_THETA ** (-jnp.arange(half, dtype=F32) / half)
    ang = pos.astype(F32)[:, None] * inv_freq[None, :]
    cos, sin = jnp.cos(ang), jnp.sin(ang)
    cos_tab = jnp.concatenate([cos, cos], axis=1)
    sin_tab = jnp.concatenate([-sin, sin], axis=1)
    reps = max(1, tm // pos.shape[0])
    return jnp.tile(cos_tab, (reps, 1)), jnp.tile(sin_tab, (reps, 1))


def _mixing(x, bsz, length, lw, tables, conv0, ssm0, past_lat, past_kr):
    n = bsz * length
    tm = min(TOKEN_TILE, n)
    z, xbc, dt, c_kv, k_rope, q, k, v = _inproj(x, lw, tables[0], tables[1], tm)
    ys, new_ssm, new_conv = _ssd(xbc, dt, z, lw, bsz, length, conv0, ssm0)
    if past_lat is None:
        ym = _attention_prompt(q, k, v, bsz, length)
    else:
        past = past_lat.shape[1]
        ym = _attention_cache(q, k, v, past_lat.reshape(bsz * past, KV_LORA),
                              past_kr.reshape(bsz * past, ROPE_DIM), lw["w_kv"], bsz, length, past)
    return ys, ym, c_kv, k_rope, new_ssm, new_conv


def kernel(x_prompt, x_sample, cache_kv_latent, cache_k_rope, state_ssm, state_conv, norm_mix, w_in, conv_w,
           conv_b, dt_bias, a_log, d_skip, ssd_norm, q_norm, w_q_up, kv_norm, w_kv_up, w_out, norm_ffn,
           ffn_w_gate, ffn_w_up, ffn_w_down, moe_router, moe_w_gate, moe_w_up, moe_w_down, final_norm):
    bp, sp = x_prompt.shape[:2]
    bs, ss = x_sample.shape[:2]
    depth = w_in.shape[0]
    past = cache_kv_latent.shape[2]
    tabs_p = _rope_tables(jnp.arange(sp, dtype=I32), min(TOKEN_TILE, bp * sp))
    tabs_s = _rope_tables(past + jnp.arange(ss, dtype=I32), min(TOKEN_TILE, bs * ss))
    final_w = final_norm.reshape(1, D_MODEL)
    hp = x_prompt.reshape(bp * sp, D_MODEL)
    hs = x_sample.reshape(bs * ss, D_MODEL)
    outs_p, outs_s = [], []
    for layer in range(depth):
        lw = _prep_layer(layer, norm_mix, w_in, conv_w, conv_b, dt_bias, a_log, d_skip, ssd_norm, q_norm,
                         w_q_up, kv_norm, w_kv_up, w_out, norm_ffn)
        last = final_w if layer == depth - 1 else None
        mix_p = _mixing(hp, bp, sp, lw, tabs_p, None, None, None, None)
        mix_s = _mixing(hs, bs, ss, lw, tabs_s, state_conv[layer], state_ssm[layer],
                        cache_kv_latent[layer], cache_k_rope[layer])
        outs_p.append(mix_p[2:])
        outs_s.append(mix_s[2:])
        idx = layer // 2
        if layer % 2 == 0:
            fw = {"w_gate": ffn_w_gate[idx].astype(BF16), "w_up": ffn_w_up[idx].astype(BF16),
                  "w_down": ffn_w_down[idx].astype(BF16)}
            hp = _dense_layer(hp, mix_p[0], mix_p[1], lw, fw, last, min(TOKEN_TILE, bp * sp))
            hs = _dense_layer(hs, mix_s[0], mix_s[1], lw, fw, last, min(TOKEN_TILE, bs * ss))
        else:
            wr_hi = moe_router[idx].astype(BF16)
            wr_lo = (moe_router[idx] - wr_hi.astype(F32)).astype(BF16)
            mw = {"w_router": jnp.pad(jnp.concatenate([wr_hi, wr_lo], axis=1),
                                      ((0, 0), (0, LANES - 2 * N_EXPERTS))),
                  "w_gate": moe_w_gate[idx].astype(BF16), "w_up": moe_w_up[idx].astype(BF16),
                  "w_down": moe_w_down[idx].astype(BF16)}
            hp, hs = _moe_layer([(hp, mix_p[0], mix_p[1]), (hs, mix_s[0], mix_s[1])], lw, mw, last)

    def stack(outs, j, shape):
        return jnp.stack([o[j].reshape(shape) for o in outs])

    return (hp.reshape(bp, sp, D_MODEL), hs.reshape(bs, ss, D_MODEL),
            stack(outs_p, 0, (bp, sp, KV_LORA)), stack(outs_p, 1, (bp, sp, ROPE_DIM)),
            stack(outs_p, 2, (bp, SSD_HEADS, SSD_HEADDIM, SSD_STATE)),
            stack(outs_p, 3, (bp, CONV_WIDTH - 1, CONV_DIM)),
            stack(outs_s, 0, (bs, ss, KV_LORA)), stack(outs_s, 1, (bs, ss, ROPE_DIM)),
            stack(outs_s, 2, (bs, SSD_HEADS, SSD_HEADDIM, SSD_STATE)),
            stack(outs_s, 3, (bs, CONV_WIDTH - 1, CONV_DIM)))
```

```python
import functools
import math

import jax
import jax.numpy as jnp
import numpy as np
from jax import lax
from jax.experimental import pallas as pl
from jax.experimental.pallas import tpu as pltpu

F32 = jnp.float32
BF16 = jnp.bfloat16
I32 = jnp.int32

D_MODEL = 1024
CHUNK = 64
LOG2_CHUNK = 6
SSD_HEADS = 16
SSD_HEADDIM = 64
SSD_INNER = SSD_HEADS * SSD_HEADDIM
SSD_GROUPS = 2
HEADS_PER_GROUP = SSD_HEADS // SSD_GROUPS
SSD_STATE = 128
CONV_WIDTH = 4
CONV_DIM = SSD_INNER + 2 * SSD_GROUPS * SSD_STATE
MLA_HEADS = 8
Q_LORA = 384
KV_LORA = 256
NOPE_DIM = 128
ROPE_DIM = 64
V_DIM = 128
QK_DIM = NOPE_DIM + ROPE_DIM
ROPE_THETA = 10000.0
MLA_INNER = MLA_HEADS * V_DIM
N_EXPERTS = 8
TOP_K = 2
EPS = 1e-6

LANES = 128
SUBLANES = 8
MXU_DIM = 256
VMEM_LIMIT = 56 * 1024 * 1024

OFF_Z = 0
OFF_XBC = OFF_Z + SSD_INNER
OFF_CQ = OFF_XBC + CONV_DIM
OFF_CKV = OFF_CQ + Q_LORA
OFF_KR = OFF_CKV + KV_LORA
OFF_KRS = OFF_KR + ROPE_DIM
OFF_DT = OFF_KRS + ROPE_DIM
IN_COLS = OFF_DT + LANES
Q_HEAD_COLS = NOPE_DIM + 2 * ROPE_DIM

TOKEN_TILE = 512
ATT_TILE = 512
ATT_HEADS_PER_STEP = 4
SSD_CHUNKS_PER_STEP = 4
MOE_TILE = 256
DISPATCH_TILE = 1024
COMBINE_TILE = 512


def _params(n_axes):
    return pltpu.CompilerParams(dimension_semantics=("arbitrary",) * n_axes,
                                vmem_limit_bytes=VMEM_LIMIT)


def _const_spec(shape):
    zeros = (0,) * len(shape)
    return pl.BlockSpec(shape, lambda *_: zeros)


def _rms(x, w):
    return x * lax.rsqrt(jnp.mean(x * x, axis=-1, keepdims=True) + EPS) * w


def _silu(x):
    return x * (1.0 / (1.0 + jnp.exp(-x)))


def _dot(a, b):
    return jnp.dot(a, b, preferred_element_type=F32)


def _dot_nt(a, b):
    return lax.dot_general(a, b, (((1,), (1,)), ((), ())), preferred_element_type=F32)


def _dot_tn(a, b):
    return lax.dot_general(a, b, (((0,), (0,)), ((), ())), preferred_element_type=F32)


def _split3(x):
    hi = x.astype(BF16)
    rest = x - hi.astype(F32)
    mid = rest.astype(BF16)
    lo = (rest - mid.astype(F32)).astype(BF16)
    return jnp.concatenate([hi, mid, lo], axis=1)


ROW_TILE = (D_MODEL // LANES, LANES)


def _row_tile_spec(tm):
    return pl.BlockSpec((tm,) + ROW_TILE, lambda i, *_: (i, 0, 0))


def _store_row_tiles(ref, x):
    for s in range(ROW_TILE[0]):
        ref[:, s, :] = x[:, s * LANES:(s + 1) * LANES]


def _load_row_tiles(ref, *lead):
    return jnp.concatenate([ref[lead + (slice(None), s, slice(None))] for s in range(ROW_TILE[0])], axis=1)


def _inproj_body(x_ref, nw_ref, win_ref, qnw_ref, wq_ref, kvnw_ref, wkv_ref, cos_ref, sin_ref, dtb_ref,
                 z_ref, xbc_ref, dt_ref, ckv_ref, kr_ref, q_ref, k_ref, v_ref):
    h = _rms(x_ref[...], nw_ref[...]).astype(BF16)
    proj = _dot(h, win_ref[...])
    z_ref[...] = proj[:, OFF_Z:OFF_XBC]
    xbc_ref[...] = proj[:, OFF_XBC:OFF_CQ]
    pre = proj[:, OFF_DT:IN_COLS] + dtb_ref[...]
    dt_ref[...] = jnp.maximum(pre, 0.0) + jnp.log(1.0 + jnp.exp(-jnp.abs(pre)))
    cos = cos_ref[...]
    sin = sin_ref[...]
    k_rope = proj[:, OFF_KR:OFF_KRS] * cos + proj[:, OFF_KRS:OFF_DT] * sin
    kr_ref[...] = k_rope
    c_kv = _rms(proj[:, OFF_CKV:OFF_KR], kvnw_ref[...])
    ckv_ref[...] = c_kv
    qn = _rms(proj[:, OFF_CQ:OFF_CKV], qnw_ref[...]).astype(BF16)
    q_all = _dot(qn, wq_ref[...])
    kv_all = _dot(c_kv.astype(BF16), wkv_ref[...])
    k_rope_b = k_rope.astype(BF16)
    for hd in range(MLA_HEADS):
        qb = q_all[:, hd * Q_HEAD_COLS:(hd + 1) * Q_HEAD_COLS]
        q_rope = qb[:, NOPE_DIM:QK_DIM] * cos + qb[:, QK_DIM:Q_HEAD_COLS] * sin
        q_ref[hd] = jnp.concatenate([qb[:, :NOPE_DIM], q_rope], axis=1).astype(BF16)
        kvb = kv_all[:, hd * (NOPE_DIM + V_DIM):(hd + 1) * (NOPE_DIM + V_DIM)]
        k_ref[hd] = jnp.concatenate([kvb[:, :NOPE_DIM].astype(BF16), k_rope_b], axis=1)
        v_ref[hd] = kvb[:, NOPE_DIM:].astype(BF16)


def _inproj(x, lw, cos_tab, sin_tab, tm):
    n = x.shape[0]
    nt = n // tm
    tab_blocks = cos_tab.shape[0] // tm
    row = lambda w: pl.BlockSpec((tm, w), lambda i: (i, 0))
    head = lambda w: pl.BlockSpec((MLA_HEADS, tm, w), lambda i: (0, i, 0))
    tab = pl.BlockSpec((tm, ROPE_DIM), lambda i: (i % tab_blocks, 0))
    return pl.pallas_call(
        _inproj_body,
        grid=(nt,),
        in_specs=[row(D_MODEL), _const_spec((1, D_MODEL)), _const_spec((D_MODEL, IN_COLS)),
                  _const_spec((1, Q_LORA)), _const_spec((Q_LORA, MLA_HEADS * Q_HEAD_COLS)),
                  _const_spec((1, KV_LORA)), _const_spec((KV_LORA, MLA_HEADS * (NOPE_DIM + V_DIM))),
                  tab, tab, _const_spec((1, LANES))],
        out_specs=[row(SSD_INNER), row(CONV_DIM), row(LANES), row(KV_LORA), row(ROPE_DIM),
                   head(QK_DIM), head(QK_DIM), head(V_DIM)],
        out_shape=[jax.ShapeDtypeStruct((n, SSD_INNER), F32), jax.ShapeDtypeStruct((n, CONV_DIM), F32),
                   jax.ShapeDtypeStruct((n, LANES), F32), jax.ShapeDtypeStruct((n, KV_LORA), F32),
                   jax.ShapeDtypeStruct((n, ROPE_DIM), F32),
                   jax.ShapeDtypeStruct((MLA_HEADS, n, QK_DIM), BF16),
                   jax.ShapeDtypeStruct((MLA_HEADS, n, QK_DIM), BF16),
                   jax.ShapeDtypeStruct((MLA_HEADS, n, V_DIM), BF16)],
        compiler_params=_params(1),
        name="inproj",
    )(x, lw["norm_mix"], lw["w_in"], lw["q_norm"], lw["w_q"], lw["kv_norm"], lw["w_kv"],
      cos_tab, sin_tab, lw["dt_bias"])


def _ssd_body(t, cps, has_init, *refs):
    if has_init:
        (xbc_ref, dt_ref, z_ref, cw_ref, cb_ref, alog_ref, dskip_ref, nw_ref, conv0_ref, ssm0_ref,
         y_ref, ssm_ref, conv_ref, hist_ref) = refs
    else:
        (xbc_ref, dt_ref, z_ref, cw_ref, cb_ref, alog_ref, dskip_ref, nw_ref,
         y_ref, ssm_ref, conv_ref, hist_ref) = refs
    rows = cps * t
    hist_rows = SUBLANES
    first = hist_rows - (CONV_WIDTH - 1)
    group_lanes = HEADS_PER_GROUP * t
    group_cols = HEADS_PER_GROUP * SSD_HEADDIM
    heads_per_block = MXU_DIM // t
    blocks_per_group = group_lanes // MXU_DIM
    log2_t = int(math.log2(t))
    log2_p = int(math.log2(SSD_HEADDIM))

    @pl.when(pl.program_id(1) == 0)
    def _():
        hist_ref[...] = jnp.zeros_like(hist_ref)
        if has_init:
            hist_ref[first:hist_rows, :] = conv0_ref[...]
            ssm_ref[...] = ssm0_ref[...]
        else:
            ssm_ref[...] = jnp.zeros_like(ssm_ref)

    x_raw = xbc_ref[...]
    hist_ref[hist_rows:, :] = x_raw
    padded = hist_ref[...]
    conv = cb_ref[...]
    for k in range(CONV_WIDTH):
        back = CONV_WIDTH - 1 - k
        tap = x_raw if back == 0 else pltpu.roll(padded, back, axis=0)[hist_rows:, :]
        conv = conv + tap * cw_ref[k:k + 1, :]
    hist_ref[first:hist_rows, :] = x_raw[rows - (CONV_WIDTH - 1):rows]
    conv_ref[...] = x_raw[rows - (CONV_WIDTH - 1):rows]
    act = _silu(conv)
    xs = act[:, :SSD_INNER]
    b_mat = act[:, SSD_INNER:SSD_INNER + SSD_GROUPS * SSD_STATE].astype(BF16)
    c_mat = act[:, SSD_INNER + SSD_GROUPS * SSD_STATE:].astype(BF16)

    dt = dt_ref[:, :SSD_HEADS]
    da = dt * (-jnp.exp(alog_ref[...]))
    ri = lax.broadcasted_iota(I32, (rows, rows), 0)
    ci = lax.broadcasted_iota(I32, (rows, rows), 1)
    tril = ((ri >= ci) & ((ri >> log2_t) == (ci >> log2_t))).astype(BF16)
    cs3 = _dot(tril, _split3(da))
    cs = cs3[:, :SSD_HEADS] + cs3[:, SSD_HEADS:2 * SSD_HEADS] + cs3[:, 2 * SSD_HEADS:]

    def expander(width, shift):
        e_rows = lax.broadcasted_iota(I32, (3 * SSD_HEADS, width), 0)
        return ((lax.broadcasted_iota(I32, (3 * SSD_HEADS, width), 1) >> shift) == (e_rows & (SSD_HEADS - 1))
                ).astype(BF16)

    per_p = _dot(_split3(jnp.concatenate([cs, dt], axis=0)), expander(SSD_INNER, log2_p))
    cs_p, dt_p = per_p[:rows], per_p[rows:]
    cs_t = cs_p if t == SSD_HEADDIM else _dot(_split3(cs), expander(SSD_HEADS * t, log2_t))
    xdt = xs * dt_p
    lane = lax.broadcasted_iota(I32, (t, SSD_HEADS * t), 1) & (t - 1)
    rowi = lax.broadcasted_iota(I32, (t, SSD_HEADS * t), 0)
    block_mask = (lax.broadcasted_iota(I32, (MXU_DIM, heads_per_block * SSD_HEADDIM), 0) >> log2_t) == \
        (lax.broadcasted_iota(I32, (MXU_DIM, heads_per_block * SSD_HEADDIM), 1) >> log2_p)

    y_chunks = []
    for ch in range(cps):
        r0, r1 = ch * t, (ch + 1) * t
        cs_t_c, cs_p_c = cs_t[r0:r1], cs_p[r0:r1]
        cs_src = jnp.sum(jnp.where(lane == rowi, cs_t_c, 0.0), axis=0, keepdims=True)
        decay = jnp.exp(jnp.where(rowi >= lane, cs_t_c - cs_src, -jnp.inf))
        exp_cs = jnp.exp(cs_p_c)
        x_end = (xdt[r0:r1] * jnp.exp(cs_p_c[t - 1:t, :] - cs_p_c)).astype(BF16)
        chunk_decay = jnp.exp(cs[r1 - 1:r1, :])
        y_blocks = []
        for g in range(SSD_GROUPS):
            bg = b_mat[r0:r1, g * SSD_STATE:(g + 1) * SSD_STATE]
            cg = c_mat[r0:r1, g * SSD_STATE:(g + 1) * SSD_STATE]
            cb = _dot_nt(cg, jnp.concatenate([bg] * HEADS_PER_GROUP, axis=0))
            m = (cb * decay[:, g * group_lanes:(g + 1) * group_lanes]).astype(BF16)
            h0, h1 = g * HEADS_PER_GROUP, (g + 1) * HEADS_PER_GROUP
            state = ssm_ref[h0:h1].reshape(group_cols, SSD_STATE)
            y_off = _dot_nt(cg, state.astype(BF16)) * exp_cs[:, g * group_cols:(g + 1) * group_cols]
            for kb in range(blocks_per_group):
                c0 = g * group_cols + kb * heads_per_block * SSD_HEADDIM
                c1 = c0 + heads_per_block * SSD_HEADDIM
                xb = xdt[r0:r1, c0:c1]
                bd = jnp.where(block_mask, jnp.concatenate([xb] * heads_per_block, axis=0), 0.0).astype(BF16)
                y_diag = _dot(m[:, kb * MXU_DIM:(kb + 1) * MXU_DIM], bd)
                y_blocks.append(y_diag + y_off[:, c0 - g * group_cols:c1 - g * group_cols])
            local = _dot_tn(x_end[:, g * group_cols:(g + 1) * group_cols], bg)
            for r in range(HEADS_PER_GROUP):
                hd = h0 + r
                ssm_ref[hd] = ssm_ref[hd] * chunk_decay[:, hd:hd + 1] + \
                    local[r * SSD_HEADDIM:(r + 1) * SSD_HEADDIM]
        y_chunks.append(jnp.concatenate(y_blocks, axis=1))
    y = jnp.concatenate(y_chunks, axis=0) + xs * dskip_ref[...]
    y_ref[...] = _rms(y * _silu(z_ref[...]), nw_ref[...]).astype(BF16)


def _ssd(xbc, dt, z, lw, bsz, length, conv0, ssm0):
    t = CHUNK if length % CHUNK == 0 else length
    nc = length // t
    cps = max(c for c in (1, 2, SSD_CHUNKS_PER_STEP) if nc % c == 0)
    steps = nc // cps
    n = bsz * length
    has_init = conv0 is not None
    row = lambda w: pl.BlockSpec((cps * t, w), lambda b, c: (b * steps + c, 0))
    conv_spec = pl.BlockSpec((None, CONV_WIDTH - 1, CONV_DIM), lambda b, c: (b, 0, 0))
    ssm_spec = pl.BlockSpec((None, SSD_HEADS, SSD_HEADDIM, SSD_STATE), lambda b, c: (b, 0, 0, 0))
    in_specs = [row(CONV_DIM), row(LANES), row(SSD_INNER), _const_spec((CONV_WIDTH, CONV_DIM)),
                _const_spec((1, CONV_DIM)), _const_spec((1, SSD_HEADS)), _const_spec((1, SSD_INNER)),
                _const_spec((1, SSD_INNER))]
    args = [xbc, dt, z, lw["conv_w"], lw["conv_b"], lw["a_log"], lw["d_skip"], lw["ssd_norm"]]
    if has_init:
        in_specs += [conv_spec, ssm_spec]
        args += [conv0, ssm0]
    return pl.pallas_call(
        functools.partial(_ssd_body, t, cps, has_init),
        grid=(bsz, steps),
        in_specs=in_specs,
        out_specs=[row(SSD_INNER), ssm_spec, conv_spec],
        out_shape=[jax.ShapeDtypeStruct((n, SSD_INNER), BF16),
                   jax.ShapeDtypeStruct((bsz, SSD_HEADS, SSD_HEADDIM, SSD_STATE), F32),
                   jax.ShapeDtypeStruct((bsz, CONV_WIDTH - 1, CONV_DIM), F32)],
        scratch_shapes=[pltpu.VMEM((SUBLANES + cps * t, CONV_DIM), F32)],
        compiler_params=_params(2),
        name="ssd",
    )(*args)


def _lane_groups(x):
    return [x[:, g * LANES:(g + 1) * LANES] for g in range(x.shape[1] // LANES)]


def _attn_body(tq, nh, q_ref, k_ref, v_ref, o_ref, s_ref, m_ref, l_ref, acc_ref):
    qi = pl.program_id(2)
    heads = range(nh)
    q = [q_ref[h] for h in heads]
    c = (QK_DIM ** -0.5) * math.log2(math.e)

    def keys(ref, h, j):
        return ref[h, pl.ds(pl.multiple_of(j * tq, tq), tq), :]

    def lane_max(m, s):
        for part in _lane_groups(s):
            m = jnp.maximum(m, part)
        return m

    m_ref[...] = jnp.full_like(m_ref, -jnp.inf)

    @pl.loop(0, qi)
    def _(j):
        for h in heads:
            s = _dot_nt(q[h], keys(k_ref, h, j)) * c
            s_ref[h, j] = s
            m_ref[h] = lane_max(m_ref[h], s)

    rows = lax.broadcasted_iota(I32, (tq, tq), 0) >> LOG2_CHUNK
    cols = lax.broadcasted_iota(I32, (tq, tq), 1) >> LOG2_CHUNK
    for h in heads:
        s = jnp.where(cols <= rows, _dot_nt(q[h], keys(k_ref, h, qi)) * c, -jnp.inf)
        s_ref[h, qi] = s
        row_max = jnp.max(lane_max(m_ref[h], s), axis=-1, keepdims=True)
        m_ref[h] = jnp.broadcast_to(row_max, (tq, LANES))
    l_ref[...] = jnp.zeros_like(l_ref)
    acc_ref[...] = jnp.zeros_like(acc_ref)

    @pl.loop(0, qi + 1)
    def _(j):
        for h in heads:
            mc = m_ref[h]
            parts = [jnp.exp2(part - mc) for part in _lane_groups(s_ref[h, j])]
            l_ref[h] = l_ref[h] + functools.reduce(lambda a, b: a + b, parts)
            p = jnp.concatenate(parts, axis=1).astype(BF16)
            acc_ref[h] = acc_ref[h] + _dot(p, keys(v_ref, h, j))

    o_ref[...] = jnp.concatenate(
        [acc_ref[h] / jnp.sum(l_ref[h], axis=-1, keepdims=True) for h in heads], axis=1).astype(BF16)


def _attention_prompt(q, k, v, bsz, length):
    tq = min(ATT_TILE, length)
    nh = ATT_HEADS_PER_STEP
    nq = length // tq
    n = bsz * length
    return pl.pallas_call(
        functools.partial(_attn_body, tq, nh),
        grid=(bsz, MLA_HEADS // nh, nq),
        in_specs=[pl.BlockSpec((nh, tq, QK_DIM), lambda b, h, i: (h, b * nq + i, 0)),
                  pl.BlockSpec((nh, length, QK_DIM), lambda b, h, i: (h, b, 0), pipeline_mode=pl.Buffered(1)),
                  pl.BlockSpec((nh, length, V_DIM), lambda b, h, i: (h, b, 0), pipeline_mode=pl.Buffered(1))],
        out_specs=pl.BlockSpec((tq, nh * V_DIM), lambda b, h, i: (b * nq + i, h)),
        out_shape=jax.ShapeDtypeStruct((n, MLA_INNER), BF16),
        scratch_shapes=[pltpu.VMEM((nh, nq, tq, tq), F32), pltpu.VMEM((nh, tq, LANES), F32),
                        pltpu.VMEM((nh, tq, LANES), F32), pltpu.VMEM((nh, tq, V_DIM), F32)],
        compiler_params=_params(3),
        name="attn_prompt",
    )(q, k, v)


def _attn_cache_body(past, length, q_ref, kn_ref, vn_ref, lat_ref, kr_ref, wkv_ref, o_ref):
    scale = QK_DIM ** -0.5
    q_chunk = (past + np.arange(length)) // CHUNK
    past_visible = (np.arange(past) // CHUNK)[None, :] <= q_chunk[:, None]
    new_visible = q_chunk[None, :] <= q_chunk[:, None]
    kv_past = _dot(lat_ref[...].astype(BF16), wkv_ref[...])
    kr_past = kr_ref[...].astype(BF16)
    outs = []
    for hd in range(MLA_HEADS):
        q = q_ref[hd]
        kvb = kv_past[:, hd * (NOPE_DIM + V_DIM):(hd + 1) * (NOPE_DIM + V_DIM)]
        k_past = jnp.concatenate([kvb[:, :NOPE_DIM].astype(BF16), kr_past], axis=1)
        v_past = kvb[:, NOPE_DIM:].astype(BF16)
        s_p = _dot_nt(q, k_past) * scale
        s_n = _dot_nt(q, kn_ref[hd]) * scale
        if not past_visible.all():
            rows = (past + lax.broadcasted_iota(I32, (length, past), 0)) >> LOG2_CHUNK
            cols = lax.broadcasted_iota(I32, (length, past), 1) >> LOG2_CHUNK
            s_p = jnp.where(cols <= rows, s_p, -jnp.inf)
        if not new_visible.all():
            rows = (past + lax.broadcasted_iota(I32, (length, length), 0)) >> LOG2_CHUNK
            cols = (past + lax.broadcasted_iota(I32, (length, length), 1)) >> LOG2_CHUNK
            s_n = jnp.where(cols <= rows, s_n, -jnp.inf)
        m = jnp.maximum(jnp.max(s_p, axis=-1, keepdims=True), jnp.max(s_n, axis=-1, keepdims=True))
        p_p = jnp.exp(s_p - m)
        p_n = jnp.exp(s_n - m)
        denom = jnp.sum(p_p, axis=-1, keepdims=True) + jnp.sum(p_n, axis=-1, keepdims=True)
        o = _dot(p_p.astype(BF16), v_past) + _dot(p_n.astype(BF16), vn_ref[hd])
        outs.append(o / denom)
    o_ref[...] = jnp.concatenate(outs, axis=1).astype(BF16)


def _attention_cache(q, k_new, v_new, lat_past, kr_past, w_kv, bsz, length, past):
    n = bsz * length
    new = lambda w: pl.BlockSpec((MLA_HEADS, length, w), lambda b: (0, b, 0))
    old = lambda w: pl.BlockSpec((past, w), lambda b: (b, 0))
    return pl.pallas_call(
        functools.partial(_attn_cache_body, past, length),
        grid=(bsz,),
        in_specs=[new(QK_DIM), new(QK_DIM), new(V_DIM), old(KV_LORA), old(ROPE_DIM), _const_spec(w_kv.shape)],
        out_specs=pl.BlockSpec((length, MLA_INNER), lambda b: (b, 0)),
        out_shape=jax.ShapeDtypeStruct((n, MLA_INNER), BF16),
        compiler_params=_params(1),
        name="attn_cache",
    )(q, k_new, v_new, lat_past, kr_past, w_kv)


def _mix_out(x_ref, ys_ref, ym_ref, wo_ref):
    return x_ref[...] + _dot(ys_ref[...], wo_ref[:SSD_INNER, :]) + _dot(ym_ref[...], wo_ref[SSD_INNER:, :])


def _dense_body(final, x_ref, ys_ref, ym_ref, wo_ref, nw_ref, wg_ref, wu_ref, wd_ref, *rest):
    x1 = _mix_out(x_ref, ys_ref, ym_ref, wo_ref)
    h = _rms(x1, nw_ref[...]).astype(BF16)
    act = (_silu(_dot(h, wg_ref[...])) * _dot(h, wu_ref[...])).astype(BF16)
    x2 = x1 + _dot(act, wd_ref[...])
    if final:
        fw_ref, o_ref = rest
        o_ref[...] = _rms(x2, fw_ref[...])
    else:
        (o_ref,) = rest
        o_ref[...] = x2


def _resident(shape):
    zeros = (0,) * len(shape)
    return pl.BlockSpec(shape, lambda *_: zeros, pipeline_mode=pl.Buffered(1))


def _dense_layer(x, ys, ym, lw, fw, final_w, tm):
    n = x.shape[0]
    row = lambda w: pl.BlockSpec((tm, w), lambda i: (i, 0))
    d_ff = fw["w_gate"].shape[1]
    in_specs = [row(D_MODEL), row(SSD_INNER), row(MLA_INNER), _resident((SSD_INNER + MLA_INNER, D_MODEL)),
                _const_spec((1, D_MODEL)), _resident((D_MODEL, d_ff)), _resident((D_MODEL, d_ff)),
                _resident((d_ff, D_MODEL))]
    args = [x, ys, ym, lw["w_out"], lw["norm_ffn"], fw["w_gate"], fw["w_up"], fw["w_down"]]
    if final_w is not None:
        in_specs.append(_const_spec((1, D_MODEL)))
        args.append(final_w)
    return pl.pallas_call(
        functools.partial(_dense_body, final_w is not None),
        grid=(n // tm,),
        in_specs=in_specs,
        out_specs=row(D_MODEL),
        out_shape=jax.ShapeDtypeStruct((n, D_MODEL), F32),
        compiler_params=_params(1),
        name="mixout_dense",
    )(*args)


def _router_body(tm, x_ref, ys_ref, ym_ref, wo_ref, nw_ref, wr_ref,
                 x1_ref, h_ref, ri_ref, rf_ref, cnt_ref):
    @pl.when(pl.program_id(0) == 0)
    def _():
        cnt_ref[...] = jnp.zeros_like(cnt_ref)

    x1 = _mix_out(x_ref, ys_ref, ym_ref, wo_ref)
    x1_ref[...] = x1
    h = _rms(x1, nw_ref[...])
    _store_row_tiles(h_ref, h)
    lane_i = lax.broadcasted_iota(I32, (tm, LANES), 1)
    lane = lane_i.astype(F32)
    h_hi = h.astype(BF16)
    h_lo = (h - h_hi.astype(F32)).astype(BF16)
    hi_terms = _dot(h_hi, wr_ref[...])
    logits = hi_terms + pltpu.roll(hi_terms, LANES - N_EXPERTS, axis=1) + _dot(h_lo, wr_ref[...])
    logits = jnp.where(lane_i < N_EXPERTS, logits, -jnp.inf)
    top0 = jnp.max(logits, axis=-1, keepdims=True)
    e0 = jnp.min(jnp.where(logits == top0, lane, float(LANES)), axis=-1, keepdims=True)
    rest = jnp.where(lane == e0, -jnp.inf, logits)
    top1 = jnp.max(rest, axis=-1, keepdims=True)
    e1 = jnp.min(jnp.where(rest == top1, lane, float(LANES)), axis=-1, keepdims=True)
    w1 = jnp.exp(top1 - top0)
    g0 = 1.0 / (1.0 + w1)
    g1 = w1 / (1.0 + w1)
    onehot = jnp.where((lane == e0) | (lane == e1), 1.0, 0.0)
    strict = (lax.broadcasted_iota(I32, (tm, tm), 0) > lax.broadcasted_iota(I32, (tm, tm), 1)).astype(BF16)
    before = _dot(strict, onehot.astype(BF16)) + cnt_ref[0:1, :]
    pos0 = jnp.sum(jnp.where(lane == e0, before, 0.0), axis=-1, keepdims=True)
    pos1 = jnp.sum(jnp.where(lane == e1, before, 0.0), axis=-1, keepdims=True)
    cnt_ref[...] = cnt_ref[...] + jnp.sum(onehot, axis=0, keepdims=True)
    packed = jnp.where(lane_i == 0, e0, jnp.where(lane_i == 1, e1, jnp.where(lane_i == 2, pos0, pos1)))
    ri_ref[...] = packed.T[:SUBLANES, :].astype(I32)
    rf_ref[...] = jnp.where(lane_i == 0, g0, g1)


def _router_layer(x, ys, ym, lw, mw, tm):
    n = x.shape[0]
    row = lambda w: pl.BlockSpec((tm, w), lambda i: (i, 0))
    return pl.pallas_call(
        functools.partial(_router_body, tm),
        grid=(n // tm,),
        in_specs=[row(D_MODEL), row(SSD_INNER), row(MLA_INNER), _const_spec((SSD_INNER + MLA_INNER, D_MODEL)),
                  _const_spec((1, D_MODEL)), _const_spec((D_MODEL, LANES))],
        out_specs=[row(D_MODEL), _row_tile_spec(tm), pl.BlockSpec((SUBLANES, tm), lambda i: (0, i)), row(LANES),
                   _const_spec((SUBLANES, LANES))],
        out_shape=[jax.ShapeDtypeStruct((n, D_MODEL), F32), jax.ShapeDtypeStruct((n,) + ROW_TILE, F32),
                   jax.ShapeDtypeStruct((SUBLANES, n), I32), jax.ShapeDtypeStruct((n, LANES), F32),
                   jax.ShapeDtypeStruct((SUBLANES, LANES), F32)],
        compiler_params=_params(1),
        name="mixout_router",
    )(x, ys, ym, lw["w_out"], lw["norm_ffn"], mw["w_router"])


def _dispatch_body(tm, first, bounds_ref, fill_ref, dest_ref, h_ref, *rest):
    xs_ref, dest_smem, zero_ref, sem = rest[-4:]

    def zero_padding():
        zero_ref[...] = jnp.zeros_like(zero_ref)

        def fill(start):
            copy = pltpu.make_async_copy(zero_ref, xs_ref.at[pl.ds(pl.multiple_of(start, MOE_TILE), MOE_TILE)], sem)
            copy.start()
            copy.wait()

        for e in range(N_EXPERTS):
            @pl.loop(fill_ref[e], bounds_ref[e + 1], step=MOE_TILE)
            def _(start):
                fill(start)

            unused = bounds_ref[N_EXPERTS] + e * MOE_TILE
            pl.when(unused < xs_ref.shape[0])(lambda unused=unused: fill(unused))

    if first:
        pl.when(pl.program_id(0) == 0)(zero_padding)

    pltpu.sync_copy(dest_ref.at[0], dest_smem)

    def start(j, c):
        for k in range(TOP_K):
            pltpu.make_async_copy(h_ref.at[j], xs_ref.at[dest_smem[k, j]], sem).start(priority=k)
        return c

    lax.fori_loop(0, tm, start, 0, unroll=8)
    for k in range(TOP_K):
        pltpu.make_async_copy(h_ref, xs_ref.at[pl.ds(0, tm)], sem).wait()


def _dispatch(h, dest, bounds, fill_from, rows, tm, xs_so_far):
    n = h.shape[0]
    first = xs_so_far is None
    in_specs = [pl.BlockSpec((1, TOP_K, tm), lambda i, *_: (i, 0, 0)), _row_tile_spec(tm)]
    args = [bounds, fill_from, dest, h]
    if not first:
        in_specs.append(pl.BlockSpec(memory_space=pl.ANY))
        args.append(xs_so_far)
    return pl.pallas_call(
        functools.partial(_dispatch_body, tm, first),
        grid_spec=pltpu.PrefetchScalarGridSpec(
            num_scalar_prefetch=2,
            grid=(n // tm,),
            in_specs=in_specs,
            out_specs=pl.BlockSpec(memory_space=pl.ANY),
            scratch_shapes=[pltpu.SMEM((TOP_K, tm), I32), pltpu.VMEM((MOE_TILE,) + ROW_TILE, F32),
                            pltpu.SemaphoreType.DMA],
        ),
        out_shape=jax.ShapeDtypeStruct((rows,) + ROW_TILE, F32),
        input_output_aliases={} if first else {len(args) - 1: 0},
        compiler_params=_params(1),
        name="dispatch",
    )(*args)


def _experts_body(tm, te_ref, nv_ref, x_ref, wg_ref, wu_ref, wd_ref, y_ref):
    valid = nv_ref[pl.program_id(0)]

    @pl.when(valid > 0)
    def _():
        x = _load_row_tiles(x_ref).astype(BF16)
        act = (_silu(_dot(x, wg_ref[...])) * _dot(x, wu_ref[...])).astype(BF16)
        _store_row_tiles(y_ref, _dot(act, wd_ref[...]))

    @pl.when(valid == 0)
    def _():
        y_ref[...] = jnp.zeros_like(y_ref)


def _experts(xs, tile_expert, tile_valid, mw, tm):
    rows = xs.shape[0]
    d_ff = mw["w_gate"].shape[2]
    wspec = lambda a, b: pl.BlockSpec((None, a, b), lambda i, te, nv: (te[i], 0, 0),
                                      pipeline_mode=pl.Buffered(1))
    return pl.pallas_call(
        functools.partial(_experts_body, tm),
        grid_spec=pltpu.PrefetchScalarGridSpec(
            num_scalar_prefetch=2,
            grid=(rows // tm,),
            in_specs=[_row_tile_spec(tm), wspec(D_MODEL, d_ff), wspec(D_MODEL, d_ff), wspec(d_ff, D_MODEL)],
            out_specs=_row_tile_spec(tm),
        ),
        out_shape=jax.ShapeDtypeStruct((rows,) + ROW_TILE, F32),
        compiler_params=_params(1),
        name="experts",
    )(tile_expert, tile_valid, xs, mw["w_gate"], mw["w_up"], mw["w_down"])


def _combine_body(tm, final, dest_ref, dnext_ref, x1_ref, rf_ref, ys_ref, *rest):
    if final:
        fw_ref, o_ref, buf_ref, dest_smem, sems = rest
    else:
        o_ref, buf_ref, dest_smem, sems = rest
    i = pl.program_id(0)
    nsteps = pl.num_programs(0)
    slot = i % 2

    def start_row(j, s):
        for k in range(TOP_K):
            pltpu.make_async_copy(ys_ref.at[dest_smem[k, j]], buf_ref.at[s, k, j], sems.at[s]).start(priority=k)

    def wait_rows(s):
        for k in range(TOP_K):
            pltpu.make_async_copy(ys_ref.at[pl.ds(0, tm)], buf_ref.at[s, k], sems.at[s]).wait()

    def gather(src_ref, s):
        pltpu.sync_copy(src_ref.at[0], dest_smem)
        lax.fori_loop(0, tm, lambda j, c: (start_row(j, s), c)[1], 0, unroll=8)

    @pl.when(i == 0)
    def _():
        gather(dest_ref, 0)

    wait_rows(slot)

    @pl.when(i + 1 < nsteps)
    def _():
        gather(dnext_ref, 1 - slot)

    gates = rf_ref[...]
    out = x1_ref[...] + gates[:, 0:1] * _load_row_tiles(buf_ref, slot, 0) + \
        gates[:, 1:2] * _load_row_tiles(buf_ref, slot, 1)
    if final:
        out = _rms(out, fw_ref[...])
    o_ref[...] = out


def _combine(dest, x1, rf, ys, final_w, tm):
    n = x1.shape[0]
    nt = n // tm
    row = lambda w: pl.BlockSpec((tm, w), lambda i: (i, 0))
    in_specs = [pl.BlockSpec((1, TOP_K, tm), lambda i: (i, 0, 0)),
                pl.BlockSpec((1, TOP_K, tm), lambda i: (jnp.minimum(i + 1, nt - 1), 0, 0)),
                row(D_MODEL), row(LANES), pl.BlockSpec(memory_space=pl.ANY)]
    args = [dest, dest, x1, rf, ys]
    if final_w is not None:
        in_specs.append(_const_spec((1, D_MODEL)))
        args.append(final_w)
    return pl.pallas_call(
        functools.partial(_combine_body, tm, final_w is not None),
        grid=(nt,),
        in_specs=in_specs,
        out_specs=row(D_MODEL),
        out_shape=jax.ShapeDtypeStruct((n, D_MODEL), F32),
        scratch_shapes=[pltpu.VMEM((2, TOP_K, tm) + ROW_TILE, F32), pltpu.SMEM((TOP_K, tm), I32),
                        pltpu.SemaphoreType.DMA((2,))],
        compiler_params=_params(1),
        name="combine",
    )(*args)


def _moe_layer(streams, lw, mw, final_w):
    routed = [_router_layer(x, ys, ym, lw, mw, min(TOKEN_TILE, x.shape[0])) for x, ys, ym in streams]
    stream_counts = [r[4][0, :N_EXPERTS].astype(I32) for r in routed]
    counts = functools.reduce(lambda a, b: a + b, stream_counts)
    padded = (counts + MOE_TILE - 1) // MOE_TILE * MOE_TILE
    ends = jnp.cumsum(padded)
    offsets = ends - padded
    n_total = sum(x.shape[0] for x, _, _ in streams)
    rows = (n_total * TOP_K + N_EXPERTS * (MOE_TILE - 1)) // MOE_TILE * MOE_TILE
    tile_start = jnp.arange(rows // MOE_TILE, dtype=I32) * MOE_TILE
    tile_expert = jnp.minimum(jnp.sum(tile_start[:, None] >= ends[None, :], axis=1), N_EXPERTS - 1).astype(I32)
    group_end = (offsets + counts)[tile_expert]
    tile_valid = jnp.where(tile_start < ends[-1], jnp.clip(group_end - tile_start, 0, MOE_TILE), 0).astype(I32)
    bounds = jnp.concatenate([jnp.zeros((1,), I32), ends.astype(I32)])

    def tiles(dest, t):
        return dest.reshape(TOP_K, dest.shape[1] // t, t).transpose(1, 0, 2)

    xs, dests, start = None, [], offsets
    for (x1, h, ri, rf, _), own in zip(routed, stream_counts):
        n = x1.shape[0]
        dest = jnp.stack([start[ri[0]] + ri[2], start[ri[1]] + ri[3]])
        dests.append(dest)
        td = min(DISPATCH_TILE, n)
        start = start + own
        xs = _dispatch(h, tiles(dest, td), bounds, start // MOE_TILE * MOE_TILE, rows, td, xs)
    y_sorted = _experts(xs, tile_expert, tile_valid, mw, MOE_TILE)
    outs = []
    for (x1, h, ri, rf, _), dest in zip(routed, dests):
        tc = min(COMBINE_TILE, x1.shape[0])
        outs.append(_combine(tiles(dest, tc), x1, rf, y_sorted, final_w, tc))
    return outs


def _prep_layer(layer, norm_mix, w_in, conv_w, conv_b, dt_bias, a_log, d_skip, ssd_norm, q_norm, w_q_up,
                kv_norm, w_kv_up, w_out, norm_ffn):
    w = w_in[layer]
    splits = np.cumsum([SSD_INNER, CONV_DIM, SSD_HEADS, Q_LORA, KV_LORA, ROPE_DIM])[:-1].tolist()
    w_z, w_xbc, w_dt, w_cq, w_ckv, w_kr = jnp.split(w, splits, axis=1)
    half = ROPE_DIM // 2
    w_krs = jnp.concatenate([w_kr[:, half:], w_kr[:, :half]], axis=1)
    pad = jnp.zeros((D_MODEL, LANES - SSD_HEADS), F32)
    w_cat = jnp.concatenate([w_z, w_xbc, w_cq, w_ckv, w_kr, w_krs, w_dt, pad], axis=1).astype(BF16)
    wq = w_q_up[layer].reshape(Q_LORA, MLA_HEADS, QK_DIM)
    wq = jnp.concatenate([wq, wq[:, :, NOPE_DIM + half:], wq[:, :, NOPE_DIM:NOPE_DIM + half]], axis=2)
    row = lambda v: v.reshape(1, -1).astype(F32)
    return {
        "norm_mix": row(norm_mix[layer]), "w_in": w_cat,
        "q_norm": row(q_norm[layer]), "w_q": wq.reshape(Q_LORA, MLA_HEADS * Q_HEAD_COLS).astype(BF16),
        "kv_norm": row(kv_norm[layer]), "w_kv": w_kv_up[layer].astype(BF16),
        "dt_bias": jnp.pad(row(dt_bias[layer]), ((0, 0), (0, LANES - SSD_HEADS))),
        "conv_w": conv_w[layer], "conv_b": row(conv_b[layer]), "a_log": row(a_log[layer]),
        "d_skip": row(jnp.repeat(d_skip[layer], SSD_HEADDIM)), "ssd_norm": row(ssd_norm[layer]),
        "w_out": w_out[layer].astype(BF16), "norm_ffn": row(norm_ffn[layer]),
    }


def _rope_tables(pos, tm):
    half = ROPE_DIM // 2
    inv_freq = ROPE_THETA ** (-jnp.arange(half, dtype=F32) / half)
    ang = pos.astype(F32)[:, None] * inv_freq[None, :]
    cos, sin = jnp.cos(ang), jnp.sin(ang)
    cos_tab = jnp.concatenate([cos, cos], axis=1)
    sin_tab = jnp.concatenate([-sin, sin], axis=1)
    reps = max(1, tm // pos.shape[0])
    return jnp.tile(cos_tab, (reps, 1)), jnp.tile(sin_tab, (reps, 1))


def _mixing(x, bsz, length, lw, tables, conv0, ssm0, past_lat, past_kr):
    n = bsz * length
    tm = min(TOKEN_TILE, n)
    z, xbc, dt, c_kv, k_rope, q, k, v = _inproj(x, lw, tables[0], tables[1], tm)
    ys, new_ssm, new_conv = _ssd(xbc, dt, z, lw, bsz, length, conv0, ssm0)
    if past_lat is None:
        ym = _attention_prompt(q, k, v, bsz, length)
    else:
        past = past_lat.shape[1]
        ym = _attention_cache(q, k, v, past_lat.reshape(bsz * past, KV_LORA),
                              past_kr.reshape(bsz * past, ROPE_DIM), lw["w_kv"], bsz, length, past)
    return ys, ym, c_kv, k_rope, new_ssm, new_conv


def kernel(x_prompt, x_sample, cache_kv_latent, cache_k_rope, state_ssm, state_conv, norm_mix, w_in, conv_w,
           conv_b, dt_bias, a_log, d_skip, ssd_norm, q_norm, w_q_up, kv_norm, w_kv_up, w_out, norm_ffn,
           ffn_w_gate, ffn_w_up, ffn_w_down, moe_router, moe_w_gate, moe_w_up, moe_w_down, final_norm):
    bp, sp = x_prompt.shape[:2]
    bs, ss = x_sample.shape[:2]
    depth = w_in.shape[0]
    past = cache_kv_latent.shape[2]
    tabs_p = _rope_tables(jnp.arange(sp, dtype=I32), min(TOKEN_TILE, bp * sp))
    tabs_s = _rope_tables(past + jnp.arange(ss, dtype=I32), min(TOKEN_TILE, bs * ss))
    final_w = final_norm.reshape(1, D_MODEL)
    hp = x_prompt.reshape(bp * sp, D_MODEL)
    hs = x_sample.reshape(bs * ss, D_MODEL)
    outs_p, outs_s = [], []
    for layer in range(depth):
        lw = _prep_layer(layer, norm_mix, w_in, conv_w, conv_b, dt_bias, a_log, d_skip, ssd_norm, q_norm,
                         w_q_up, kv_norm, w_kv_up, w_out, norm_ffn)
        last = final_w if layer == depth - 1 else None
        mix_p = _mixing(hp, bp, sp, lw, tabs_p, None, None, None, None)
        mix_s = _mixing(hs, bs, ss, lw, tabs_s, state_conv[layer], state_ssm[layer],
                        cache_kv_latent[layer], cache_k_rope[layer])
        outs_p.append(mix_p[2:])
        outs_s.append(mix_s[2:])
        idx = layer // 2
        if layer % 2 == 0:
            fw = {"w_gate": ffn_w_gate[idx].astype(BF16), "w_up": ffn_w_up[idx].astype(BF16),
                  "w_down": ffn_w_down[idx].astype(BF16)}
            hp = _dense_layer(hp, mix_p[0], mix_p[1], lw, fw, last, min(TOKEN_TILE, bp * sp))
            hs = _dense_layer(hs, mix_s[0], mix_s[1], lw, fw, last, min(TOKEN_TILE, bs * ss))
        else:
            wr_hi = moe_router[idx].astype(BF16)
            wr_lo = (moe_router[idx] - wr_hi.astype(F32)).astype(BF16)
            mw = {"w_router": jnp.pad(jnp.concatenate([wr_hi, wr_lo], axis=1),
                                      ((0, 0), (0, LANES - 2 * N_EXPERTS))),
                  "w_gate": moe_w_gate[idx].astype(BF16), "w_up": moe_w_up[idx].astype(BF16),
                  "w_down": moe_w_down[idx].astype(BF16)}
            hp, hs = _moe_layer([(hp, mix_p[0], mix_p[1]), (hs, mix_s[0], mix_s[1])], lw, mw, last)

    def stack(outs, j, shape):
        return jnp.stack([o[j].reshape(shape) for o in outs])

    return (hp.reshape(bp, sp, D_MODEL), hs.reshape(bs, ss, D_MODEL),
            stack(outs_p, 0, (bp, sp, KV_LORA)), stack(outs_p, 1, (bp, sp, ROPE_DIM)),
            stack(outs_p, 2, (bp, SSD_HEADS, SSD_HEADDIM, SSD_STATE)),
            stack(outs_p, 3, (bp, CONV_WIDTH - 1, CONV_DIM)),
            stack(outs_s, 0, (bs, ss, KV_LORA)), stack(outs_s, 1, (bs, ss, ROPE_DIM)),
            stack(outs_s, 2, (bs, SSD_HEADS, SSD_HEADDIM, SSD_STATE)),
            stack(outs_s, 3, (bs, CONV_WIDTH - 1, CONV_DIM)))
```

```python
import functools
import math

import jax
import jax.numpy as jnp
import numpy as np
from jax import lax
from jax.experimental import pallas as pl
from jax.experimental.pallas import tpu as pltpu

F32 = jnp.float32
BF16 = jnp.bfloat16
I32 = jnp.int32

D_MODEL = 1024
CHUNK = 64
LOG2_CHUNK = 6
SSD_HEADS = 16
SSD_HEADDIM = 64
SSD_INNER = SSD_HEADS * SSD_HEADDIM
SSD_GROUPS = 2
HEADS_PER_GROUP = SSD_HEADS // SSD_GROUPS
SSD_STATE = 128
CONV_WIDTH = 4
CONV_DIM = SSD_INNER + 2 * SSD_GROUPS * SSD_STATE
MLA_HEADS = 8
Q_LORA = 384
KV_LORA = 256
NOPE_DIM = 128
ROPE_DIM = 64
V_DIM = 128
QK_DIM = NOPE_DIM + ROPE_DIM
ROPE_THETA = 10000.0
MLA_INNER = MLA_HEADS * V_DIM
N_EXPERTS = 8
TOP_K = 2
EPS = 1e-6

LANES = 128
SUBLANES = 8
MXU_DIM = 256
VMEM_LIMIT = 56 * 1024 * 1024

OFF_Z = 0
OFF_XBC = OFF_Z + SSD_INNER
OFF_CQ = OFF_XBC + CONV_DIM
OFF_CKV = OFF_CQ + Q_LORA
OFF_KR = OFF_CKV + KV_LORA
OFF_KRS = OFF_KR + ROPE_DIM
OFF_DT = OFF_KRS + ROPE_DIM
IN_COLS = OFF_DT + LANES
Q_HEAD_COLS = NOPE_DIM + 2 * ROPE_DIM

TOKEN_TILE = 512
ATT_TILE = 512
ATT_HEADS_PER_STEP = 4
SSD_CHUNKS_PER_STEP = 4
MOE_TILE = 256
DISPATCH_TILE = 1024
COMBINE_TILE = 512


def _params(n_axes):
    return pltpu.CompilerParams(dimension_semantics=("arbitrary",) * n_axes,
                                vmem_limit_bytes=VMEM_LIMIT)


def _const_spec(shape):
    zeros = (0,) * len(shape)
    return pl.BlockSpec(shape, lambda *_: zeros)


def _rms(x, w):
    return x * lax.rsqrt(jnp.mean(x * x, axis=-1, keepdims=True) + EPS) * w


def _silu(x):
    return x * (1.0 / (1.0 + jnp.exp(-x)))


def _dot(a, b):
    return jnp.dot(a, b, preferred_element_type=F32)


def _dot_nt(a, b):
    return lax.dot_general(a, b, (((1,), (1,)), ((), ())), preferred_element_type=F32)


def _dot_tn(a, b):
    return lax.dot_general(a, b, (((0,), (0,)), ((), ())), preferred_element_type=F32)


def _split3(x):
    hi = x.astype(BF16)
    rest = x - hi.astype(F32)
    mid = rest.astype(BF16)
    lo = (rest - mid.astype(F32)).astype(BF16)
    return jnp.concatenate([hi, mid, lo], axis=1)


ROW_TILE = (D_MODEL // LANES, LANES)


def _row_tile_spec(tm):
    return pl.BlockSpec((tm,) + ROW_TILE, lambda i, *_: (i, 0, 0))


def _store_row_tiles(ref, x):
    for s in range(ROW_TILE[0]):
        ref[:, s, :] = x[:, s * LANES:(s + 1) * LANES]


def _load_row_tiles(ref, *lead):
    return jnp.concatenate([ref[lead + (slice(None), s, slice(None))] for s in range(ROW_TILE[0])], axis=1)


def _inproj_body(x_ref, nw_ref, win_ref, qnw_ref, wq_ref, kvnw_ref, wkv_ref, cos_ref, sin_ref, dtb_ref,
                 z_ref, xbc_ref, dt_ref, ckv_ref, kr_ref, q_ref, k_ref, v_ref):
    h = _rms(x_ref[...], nw_ref[...]).astype(BF16)
    proj = _dot(h, win_ref[...])
    z_ref[...] = proj[:, OFF_Z:OFF_XBC]
    xbc_ref[...] = proj[:, OFF_XBC:OFF_CQ]
    pre = proj[:, OFF_DT:IN_COLS] + dtb_ref[...]
    dt_ref[...] = jnp.maximum(pre, 0.0) + jnp.log(1.0 + jnp.exp(-jnp.abs(pre)))
    cos = cos_ref[...]
    sin = sin_ref[...]
    k_rope = proj[:, OFF_KR:OFF_KRS] * cos + proj[:, OFF_KRS:OFF_DT] * sin
    kr_ref[...] = k_rope
    c_kv = _rms(proj[:, OFF_CKV:OFF_KR], kvnw_ref[...])
    ckv_ref[...] = c_kv
    qn = _rms(proj[:, OFF_CQ:OFF_CKV], qnw_ref[...]).astype(BF16)
    q_all = _dot(qn, wq_ref[...])
    kv_all = _dot(c_kv.astype(BF16), wkv_ref[...])
    k_rope_b = k_rope.astype(BF16)
    for hd in range(MLA_HEADS):
        qb = q_all[:, hd * Q_HEAD_COLS:(hd + 1) * Q_HEAD_COLS]
        q_rope = qb[:, NOPE_DIM:QK_DIM] * cos + qb[:, QK_DIM:Q_HEAD_COLS] * sin
        q_ref[hd] = jnp.concatenate([qb[:, :NOPE_DIM], q_rope], axis=1).astype(BF16)
        kvb = kv_all[:, hd * (NOPE_DIM + V_DIM):(hd + 1) * (NOPE_DIM + V_DIM)]
        k_ref[hd] = jnp.concatenate([kvb[:, :NOPE_DIM].astype(BF16), k_rope_b], axis=1)
        v_ref[hd] = kvb[:, NOPE_DIM:].astype(BF16)


def _inproj(x, lw, cos_tab, sin_tab, tm):
    n = x.shape[0]
    nt = n // tm
    tab_blocks = cos_tab.shape[0] // tm
    row = lambda w: pl.BlockSpec((tm, w), lambda i: (i, 0))
    head = lambda w: pl.BlockSpec((MLA_HEADS, tm, w), lambda i: (0, i, 0))
    tab = pl.BlockSpec((tm, ROPE_DIM), lambda i: (i % tab_blocks, 0))
    return pl.pallas_call(
        _inproj_body,
        grid=(nt,),
        in_specs=[row(D_MODEL), _const_spec((1, D_MODEL)), _const_spec((D_MODEL, IN_COLS)),
                  _const_spec((1, Q_LORA)), _const_spec((Q_LORA, MLA_HEADS * Q_HEAD_COLS)),
                  _const_spec((1, KV_LORA)), _const_spec((KV_LORA, MLA_HEADS * (NOPE_DIM + V_DIM))),
                  tab, tab, _const_spec((1, LANES))],
        out_specs=[row(SSD_INNER), row(CONV_DIM), row(LANES), row(KV_LORA), row(ROPE_DIM),
                   head(QK_DIM), head(QK_DIM), head(V_DIM)],
        out_shape=[jax.ShapeDtypeStruct((n, SSD_INNER), F32), jax.ShapeDtypeStruct((n, CONV_DIM), F32),
                   jax.ShapeDtypeStruct((n, LANES), F32), jax.ShapeDtypeStruct((n, KV_LORA), F32),
                   jax.ShapeDtypeStruct((n, ROPE_DIM), F32),
                   jax.ShapeDtypeStruct((MLA_HEADS, n, QK_DIM), BF16),
                   jax.ShapeDtypeStruct((MLA_HEADS, n, QK_DIM), BF16),
                   jax.ShapeDtypeStruct((MLA_HEADS, n, V_DIM), BF16)],
        compiler_params=_params(1),
        name="inproj",
    )(x, lw["norm_mix"], lw["w_in"], lw["q_norm"], lw["w_q"], lw["kv_norm"], lw["w_kv"],
      cos_tab, sin_tab, lw["dt_bias"])


def _ssd_body(t, cps, has_init, *refs):
    if has_init:
        (xbc_ref, dt_ref, z_ref, cw_ref, cb_ref, alog_ref, dskip_ref, nw_ref, conv0_ref, ssm0_ref,
         y_ref, ssm_ref, conv_ref, hist_ref) = refs
    else:
        (xbc_ref, dt_ref, z_ref, cw_ref, cb_ref, alog_ref, dskip_ref, nw_ref,
         y_ref, ssm_ref, conv_ref, hist_ref) = refs
    rows = cps * t
    hist_rows = SUBLANES
    first = hist_rows - (CONV_WIDTH - 1)
    group_lanes = HEADS_PER_GROUP * t
    group_cols = HEADS_PER_GROUP * SSD_HEADDIM
    heads_per_block = MXU_DIM // t
    blocks_per_group = group_lanes // MXU_DIM
    log2_t = int(math.log2(t))
    log2_p = int(math.log2(SSD_HEADDIM))

    @pl.when(pl.program_id(1) == 0)
    def _():
        hist_ref[...] = jnp.zeros_like(hist_ref)
        if has_init:
            hist_ref[first:hist_rows, :] = conv0_ref[...]
            ssm_ref[...] = ssm0_ref[...]
        else:
            ssm_ref[...] = jnp.zeros_like(ssm_ref)

    x_raw = xbc_ref[...]
    hist_ref[hist_rows:, :] = x_raw
    padded = hist_ref[...]
    conv = cb_ref[...]
    for k in range(CONV_WIDTH):
        back = CONV_WIDTH - 1 - k
        tap = x_raw if back == 0 else pltpu.roll(padded, back, axis=0)[hist_rows:, :]
        conv = conv + tap * cw_ref[k:k + 1, :]
    hist_ref[first:hist_rows, :] = x_raw[rows - (CONV_WIDTH - 1):rows]
    conv_ref[...] = x_raw[rows - (CONV_WIDTH - 1):rows]
    act = _silu(conv)
    xs = act[:, :SSD_INNER]
    b_mat = act[:, SSD_INNER:SSD_INNER + SSD_GROUPS * SSD_STATE].astype(BF16)
    c_mat = act[:, SSD_INNER + SSD_GROUPS * SSD_STATE:].astype(BF16)

    dt = dt_ref[:, :SSD_HEADS]
    da = dt * (-jnp.exp(alog_ref[...]))
    ri = lax.broadcasted_iota(I32, (rows, rows), 0)
    ci = lax.broadcasted_iota(I32, (rows, rows), 1)
    tril = ((ri >= ci) & ((ri >> log2_t) == (ci >> log2_t))).astype(BF16)
    cs3 = _dot(tril, _split3(da))
    cs = cs3[:, :SSD_HEADS] + cs3[:, SSD_HEADS:2 * SSD_HEADS] + cs3[:, 2 * SSD_HEADS:]

    def expander(width, shift):
        e_rows = lax.broadcasted_iota(I32, (3 * SSD_HEADS, width), 0)
        return ((lax.broadcasted_iota(I32, (3 * SSD_HEADS, width), 1) >> shift) == (e_rows & (SSD_HEADS - 1))
                ).astype(BF16)

    per_p = _dot(_split3(jnp.concatenate([cs, dt], axis=0)), expander(SSD_INNER, log2_p))
    cs_p, dt_p = per_p[:rows], per_p[rows:]
    cs_t = cs_p if t == SSD_HEADDIM else _dot(_split3(cs), expander(SSD_HEADS * t, log2_t))
    xdt = xs * dt_p
    lane = lax.broadcasted_iota(I32, (t, SSD_HEADS * t), 1) & (t - 1)
    rowi = lax.broadcasted_iota(I32, (t, SSD_HEADS * t), 0)
    block_mask = (lax.broadcasted_iota(I32, (MXU_DIM, heads_per_block * SSD_HEADDIM), 0) >> log2_t) == \
        (lax.broadcasted_iota(I32, (MXU_DIM, heads_per_block * SSD_HEADDIM), 1) >> log2_p)

    y_chunks = []
    for ch in range(cps):
        r0, r1 = ch * t, (ch + 1) * t
        cs_t_c, cs_p_c = cs_t[r0:r1], cs_p[r0:r1]
        cs_src = jnp.sum(jnp.where(lane == rowi, cs_t_c, 0.0), axis=0, keepdims=True)
        decay = jnp.exp(jnp.where(rowi >= lane, cs_t_c - cs_src, -jnp.inf))
        exp_cs = jnp.exp(cs_p_c)
        x_end = (xdt[r0:r1] * jnp.exp(cs_p_c[t - 1:t, :] - cs_p_c)).astype(BF16)
        chunk_decay = jnp.exp(cs[r1 - 1:r1, :])
        y_blocks = []
        for g in range(SSD_GROUPS):
            bg = b_mat[r0:r1, g * SSD_STATE:(g + 1) * SSD_STATE]
            cg = c_mat[r0:r1, g * SSD_STATE:(g + 1) * SSD_STATE]
            cb = _dot_nt(cg, jnp.concatenate([bg] * HEADS_PER_GROUP, axis=0))
            m = (cb * decay[:, g * group_lanes:(g + 1) * group_lanes]).astype(BF16)
            h0, h1 = g * HEADS_PER_GROUP, (g + 1) * HEADS_PER_GROUP
            state = ssm_ref[h0:h1].reshape(group_cols, SSD_STATE)
            y_off = _dot_nt(cg, state.astype(BF16)) * exp_cs[:, g * group_cols:(g + 1) * group_cols]
            for kb in range(blocks_per_group):
                c0 = g * group_cols + kb * heads_per_block * SSD_HEADDIM
                c1 = c0 + heads_per_block * SSD_HEADDIM
                xb = xdt[r0:r1, c0:c1]
                bd = jnp.where(block_mask, jnp.concatenate([xb] * heads_per_block, axis=0), 0.0).astype(BF16)
                y_diag = _dot(m[:, kb * MXU_DIM:(kb + 1) * MXU_DIM], bd)
                y_blocks.append(y_diag + y_off[:, c0 - g * group_cols:c1 - g * group_cols])
            local = _dot_tn(x_end[:, g * group_cols:(g + 1) * group_cols], bg)
            for r in range(HEADS_PER_GROUP):
                hd = h0 + r
                ssm_ref[hd] = ssm_ref[hd] * chunk_decay[:, hd:hd + 1] + \
                    local[r * SSD_HEADDIM:(r + 1) * SSD_HEADDIM]
        y_chunks.append(jnp.concatenate(y_blocks, axis=1))
    y = jnp.concatenate(y_chunks, axis=0) + xs * dskip_ref[...]
    y_ref[...] = _rms(y * _silu(z_ref[...]), nw_ref[...]).astype(BF16)


def _ssd(xbc, dt, z, lw, bsz, length, layer, conv0, ssm0):
    t = CHUNK if length % CHUNK == 0 else length
    nc = length // t
    cps = max(c for c in (1, 2, SSD_CHUNKS_PER_STEP) if nc % c == 0)
    steps = nc // cps
    n = bsz * length
    has_init = conv0 is not None
    row = lambda w: pl.BlockSpec((cps * t, w), lambda b, c: (b * steps + c, 0))
    conv_spec = pl.BlockSpec((None, CONV_WIDTH - 1, CONV_DIM), lambda b, c: (b, 0, 0))
    ssm_spec = pl.BlockSpec((None, SSD_HEADS, SSD_HEADDIM, SSD_STATE), lambda b, c: (b, 0, 0, 0))
    in_specs = [row(CONV_DIM), row(LANES), row(SSD_INNER), _const_spec((CONV_WIDTH, CONV_DIM)),
                _const_spec((1, CONV_DIM)), _const_spec((1, SSD_HEADS)), _const_spec((1, SSD_INNER)),
                _const_spec((1, SSD_INNER))]
    args = [xbc, dt, z, lw["conv_w"], lw["conv_b"], lw["a_log"], lw["d_skip"], lw["ssd_norm"]]
    if has_init:
        in_specs += [pl.BlockSpec((None, None, CONV_WIDTH - 1, CONV_DIM), lambda b, c: (layer, b, 0, 0)),
                     pl.BlockSpec((None, None, SSD_HEADS, SSD_HEADDIM, SSD_STATE),
                                  lambda b, c: (layer, b, 0, 0, 0))]
        args += [conv0, ssm0]
    return pl.pallas_call(
        functools.partial(_ssd_body, t, cps, has_init),
        grid=(bsz, steps),
        in_specs=in_specs,
        out_specs=[row(SSD_INNER), ssm_spec, conv_spec],
        out_shape=[jax.ShapeDtypeStruct((n, SSD_INNER), BF16),
                   jax.ShapeDtypeStruct((bsz, SSD_HEADS, SSD_HEADDIM, SSD_STATE), F32),
                   jax.ShapeDtypeStruct((bsz, CONV_WIDTH - 1, CONV_DIM), F32)],
        scratch_shapes=[pltpu.VMEM((SUBLANES + cps * t, CONV_DIM), F32)],
        compiler_params=_params(2),
        name="ssd",
    )(*args)


def _lane_groups(x):
    return [x[:, g * LANES:(g + 1) * LANES] for g in range(x.shape[1] // LANES)]


def _attn_body(tq, nh, q_ref, k_ref, v_ref, o_ref, s_ref, m_ref, l_ref, acc_ref):
    qi = pl.program_id(2)
    heads = range(nh)
    q = [q_ref[h] for h in heads]
    c = (QK_DIM ** -0.5) * math.log2(math.e)

    def keys(ref, h, j):
        return ref[h, pl.ds(pl.multiple_of(j * tq, tq), tq), :]

    def lane_max(m, s):
        for part in _lane_groups(s):
            m = jnp.maximum(m, part)
        return m

    m_ref[...] = jnp.full_like(m_ref, -jnp.inf)

    @pl.loop(0, qi)
    def _(j):
        for h in heads:
            s = _dot_nt(q[h], keys(k_ref, h, j)) * c
            s_ref[h, j] = s
            m_ref[h] = lane_max(m_ref[h], s)

    rows = lax.broadcasted_iota(I32, (tq, tq), 0) >> LOG2_CHUNK
    cols = lax.broadcasted_iota(I32, (tq, tq), 1) >> LOG2_CHUNK
    for h in heads:
        s = jnp.where(cols <= rows, _dot_nt(q[h], keys(k_ref, h, qi)) * c, -jnp.inf)
        s_ref[h, qi] = s
        row_max = jnp.max(lane_max(m_ref[h], s), axis=-1, keepdims=True)
        m_ref[h] = jnp.broadcast_to(row_max, (tq, LANES))
    l_ref[...] = jnp.zeros_like(l_ref)
    acc_ref[...] = jnp.zeros_like(acc_ref)

    @pl.loop(0, qi + 1)
    def _(j):
        for h in heads:
            mc = m_ref[h]
            parts = [jnp.exp2(part - mc) for part in _lane_groups(s_ref[h, j])]
            l_ref[h] = l_ref[h] + functools.reduce(lambda a, b: a + b, parts)
            p = jnp.concatenate(parts, axis=1).astype(BF16)
            acc_ref[h] = acc_ref[h] + _dot(p, keys(v_ref, h, j))

    o_ref[...] = jnp.concatenate(
        [acc_ref[h] / jnp.sum(l_ref[h], axis=-1, keepdims=True) for h in heads], axis=1).astype(BF16)


def _attention_prompt(q, k, v, bsz, length):
    tq = min(ATT_TILE, length)
    nh = ATT_HEADS_PER_STEP
    nq = length // tq
    n = bsz * length
    return pl.pallas_call(
        functools.partial(_attn_body, tq, nh),
        grid=(bsz, MLA_HEADS // nh, nq),
        in_specs=[pl.BlockSpec((nh, tq, QK_DIM), lambda b, h, i: (h, b * nq + i, 0)),
                  pl.BlockSpec((nh, length, QK_DIM), lambda b, h, i: (h, b, 0), pipeline_mode=pl.Buffered(1)),
                  pl.BlockSpec((nh, length, V_DIM), lambda b, h, i: (h, b, 0), pipeline_mode=pl.Buffered(1))],
        out_specs=pl.BlockSpec((tq, nh * V_DIM), lambda b, h, i: (b * nq + i, h)),
        out_shape=jax.ShapeDtypeStruct((n, MLA_INNER), BF16),
        scratch_shapes=[pltpu.VMEM((nh, nq, tq, tq), F32), pltpu.VMEM((nh, tq, LANES), F32),
                        pltpu.VMEM((nh, tq, LANES), F32), pltpu.VMEM((nh, tq, V_DIM), F32)],
        compiler_params=_params(3),
        name="attn_prompt",
    )(q, k, v)


def _attn_cache_body(past, length, q_ref, kn_ref, vn_ref, lat_ref, kr_ref, wkv_ref, o_ref):
    scale = QK_DIM ** -0.5
    q_chunk = (past + np.arange(length)) // CHUNK
    past_visible = (np.arange(past) // CHUNK)[None, :] <= q_chunk[:, None]
    new_visible = q_chunk[None, :] <= q_chunk[:, None]
    kv_past = _dot(lat_ref[...].astype(BF16), wkv_ref[...])
    kr_past = kr_ref[...].astype(BF16)
    outs = []
    for hd in range(MLA_HEADS):
        q = q_ref[hd]
        kvb = kv_past[:, hd * (NOPE_DIM + V_DIM):(hd + 1) * (NOPE_DIM + V_DIM)]
        k_past = jnp.concatenate([kvb[:, :NOPE_DIM].astype(BF16), kr_past], axis=1)
        v_past = kvb[:, NOPE_DIM:].astype(BF16)
        s_p = _dot_nt(q, k_past) * scale
        s_n = _dot_nt(q, kn_ref[hd]) * scale
        if not past_visible.all():
            rows = (past + lax.broadcasted_iota(I32, (length, past), 0)) >> LOG2_CHUNK
            cols = lax.broadcasted_iota(I32, (length, past), 1) >> LOG2_CHUNK
            s_p = jnp.where(cols <= rows, s_p, -jnp.inf)
        if not new_visible.all():
            rows = (past + lax.broadcasted_iota(I32, (length, length), 0)) >> LOG2_CHUNK
            cols = (past + lax.broadcasted_iota(I32, (length, length), 1)) >> LOG2_CHUNK
            s_n = jnp.where(cols <= rows, s_n, -jnp.inf)
        m = jnp.maximum(jnp.max(s_p, axis=-1, keepdims=True), jnp.max(s_n, axis=-1, keepdims=True))
        p_p = jnp.exp(s_p - m)
        p_n = jnp.exp(s_n - m)
        denom = jnp.sum(p_p, axis=-1, keepdims=True) + jnp.sum(p_n, axis=-1, keepdims=True)
        o = _dot(p_p.astype(BF16), v_past) + _dot(p_n.astype(BF16), vn_ref[hd])
        outs.append(o / denom)
    o_ref[...] = jnp.concatenate(outs, axis=1).astype(BF16)


def _attention_cache(q, k_new, v_new, lat_past, kr_past, w_kv, bsz, length, past, layer):
    n = bsz * length
    new = lambda w: pl.BlockSpec((MLA_HEADS, length, w), lambda b: (0, b, 0))
    old = lambda w: pl.BlockSpec((past, w), lambda b: (layer * bsz + b, 0))
    return pl.pallas_call(
        functools.partial(_attn_cache_body, past, length),
        grid=(bsz,),
        in_specs=[new(QK_DIM), new(QK_DIM), new(V_DIM), old(KV_LORA), old(ROPE_DIM), _const_spec(w_kv.shape)],
        out_specs=pl.BlockSpec((length, MLA_INNER), lambda b: (b, 0)),
        out_shape=jax.ShapeDtypeStruct((n, MLA_INNER), BF16),
        compiler_params=_params(1),
        name="attn_cache",
    )(q, k_new, v_new, lat_past, kr_past, w_kv)


def _mix_out(x_ref, ys_ref, ym_ref, wo_ref):
    return x_ref[...] + _dot(ys_ref[...], wo_ref[:SSD_INNER, :]) + _dot(ym_ref[...], wo_ref[SSD_INNER:, :])


def _dense_body(final, x_ref, ys_ref, ym_ref, wo_ref, nw_ref, wg_ref, wu_ref, wd_ref, *rest):
    x1 = _mix_out(x_ref, ys_ref, ym_ref, wo_ref)
    h = _rms(x1, nw_ref[...]).astype(BF16)
    act = (_silu(_dot(h, wg_ref[...])) * _dot(h, wu_ref[...])).astype(BF16)
    x2 = x1 + _dot(act, wd_ref[...])
    if final:
        fw_ref, o_ref = rest
        o_ref[...] = _rms(x2, fw_ref[...])
    else:
        (o_ref,) = rest
        o_ref[...] = x2


def _resident(shape):
    zeros = (0,) * len(shape)
    return pl.BlockSpec(shape, lambda *_: zeros, pipeline_mode=pl.Buffered(1))


def _dense_layer(x, ys, ym, lw, fw, final_w, tm):
    n = x.shape[0]
    row = lambda w: pl.BlockSpec((tm, w), lambda i: (i, 0))
    d_ff = fw["w_gate"].shape[1]
    in_specs = [row(D_MODEL), row(SSD_INNER), row(MLA_INNER), _resident((SSD_INNER + MLA_INNER, D_MODEL)),
                _const_spec((1, D_MODEL)), _resident((D_MODEL, d_ff)), _resident((D_MODEL, d_ff)),
                _resident((d_ff, D_MODEL))]
    args = [x, ys, ym, lw["w_out"], lw["norm_ffn"], fw["w_gate"], fw["w_up"], fw["w_down"]]
    if final_w is not None:
        in_specs.append(_const_spec((1, D_MODEL)))
        args.append(final_w)
    return pl.pallas_call(
        functools.partial(_dense_body, final_w is not None),
        grid=(n // tm,),
        in_specs=in_specs,
        out_specs=row(D_MODEL),
        out_shape=jax.ShapeDtypeStruct((n, D_MODEL), F32),
        compiler_params=_params(1),
        name="mixout_dense",
    )(*args)


def _router_body(tm, x_ref, ys_ref, ym_ref, wo_ref, nw_ref, wr_ref,
                 x1_ref, h_ref, ri_ref, rf_ref, cnt_ref):
    @pl.when(pl.program_id(0) == 0)
    def _():
        cnt_ref[...] = jnp.zeros_like(cnt_ref)

    x1 = _mix_out(x_ref, ys_ref, ym_ref, wo_ref)
    x1_ref[...] = x1
    h = _rms(x1, nw_ref[...])
    _store_row_tiles(h_ref, h)
    lane_i = lax.broadcasted_iota(I32, (tm, LANES), 1)
    lane = lane_i.astype(F32)
    h_hi = h.astype(BF16)
    h_lo = (h - h_hi.astype(F32)).astype(BF16)
    hi_terms = _dot(h_hi, wr_ref[...])
    logits = hi_terms + pltpu.roll(hi_terms, LANES - N_EXPERTS, axis=1) + _dot(h_lo, wr_ref[...])
    logits = jnp.where(lane_i < N_EXPERTS, logits, -jnp.inf)
    top0 = jnp.max(logits, axis=-1, keepdims=True)
    e0 = jnp.min(jnp.where(logits == top0, lane, float(LANES)), axis=-1, keepdims=True)
    rest = jnp.where(lane == e0, -jnp.inf, logits)
    top1 = jnp.max(rest, axis=-1, keepdims=True)
    e1 = jnp.min(jnp.where(rest == top1, lane, float(LANES)), axis=-1, keepdims=True)
    w1 = jnp.exp(top1 - top0)
    g0 = 1.0 / (1.0 + w1)
    g1 = w1 / (1.0 + w1)
    onehot = jnp.where((lane == e0) | (lane == e1), 1.0, 0.0)
    strict = (lax.broadcasted_iota(I32, (tm, tm), 0) > lax.broadcasted_iota(I32, (tm, tm), 1)).astype(BF16)
    before = _dot(strict, onehot.astype(BF16)) + cnt_ref[0:1, :]
    pos0 = jnp.sum(jnp.where(lane == e0, before, 0.0), axis=-1, keepdims=True)
    pos1 = jnp.sum(jnp.where(lane == e1, before, 0.0), axis=-1, keepdims=True)
    cnt_ref[...] = cnt_ref[...] + jnp.sum(onehot, axis=0, keepdims=True)
    packed = jnp.where(lane_i == 0, e0, jnp.where(lane_i == 1, e1, jnp.where(lane_i == 2, pos0, pos1)))
    ri_ref[...] = packed.T[:SUBLANES, :].astype(I32)
    rf_ref[...] = jnp.where(lane_i == 0, g0, g1)


def _router_layer(x, ys, ym, lw, mw, tm):
    n = x.shape[0]
    row = lambda w: pl.BlockSpec((tm, w), lambda i: (i, 0))
    return pl.pallas_call(
        functools.partial(_router_body, tm),
        grid=(n // tm,),
        in_specs=[row(D_MODEL), row(SSD_INNER), row(MLA_INNER), _const_spec((SSD_INNER + MLA_INNER, D_MODEL)),
                  _const_spec((1, D_MODEL)), _const_spec((D_MODEL, LANES))],
        out_specs=[row(D_MODEL), _row_tile_spec(tm), pl.BlockSpec((SUBLANES, tm), lambda i: (0, i)), row(LANES),
                   _const_spec((SUBLANES, LANES))],
        out_shape=[jax.ShapeDtypeStruct((n, D_MODEL), F32), jax.ShapeDtypeStruct((n,) + ROW_TILE, F32),
                   jax.ShapeDtypeStruct((SUBLANES, n), I32), jax.ShapeDtypeStruct((n, LANES), F32),
                   jax.ShapeDtypeStruct((SUBLANES, LANES), F32)],
        compiler_params=_params(1),
        name="mixout_router",
    )(x, ys, ym, lw["w_out"], lw["norm_ffn"], mw["w_router"])


def _dispatch_body(tm, first, bounds_ref, fill_ref, dest_ref, h_ref, *rest):
    xs_ref, dest_smem, zero_ref, sem = rest[-4:]

    def zero_padding():
        zero_ref[...] = jnp.zeros_like(zero_ref)

        def fill(start):
            copy = pltpu.make_async_copy(zero_ref, xs_ref.at[pl.ds(pl.multiple_of(start, MOE_TILE), MOE_TILE)], sem)
            copy.start()
            copy.wait()

        for e in range(N_EXPERTS):
            @pl.loop(fill_ref[e], bounds_ref[e + 1], step=MOE_TILE)
            def _(start):
                fill(start)

            unused = bounds_ref[N_EXPERTS] + e * MOE_TILE
            pl.when(unused < xs_ref.shape[0])(lambda unused=unused: fill(unused))

    if first:
        pl.when(pl.program_id(0) == 0)(zero_padding)

    pltpu.sync_copy(dest_ref.at[0], dest_smem)

    def start(j, c):
        for k in range(TOP_K):
            pltpu.make_async_copy(h_ref.at[j], xs_ref.at[dest_smem[k, j]], sem).start(priority=k)
        return c

    lax.fori_loop(0, tm, start, 0, unroll=8)
    for k in range(TOP_K):
        pltpu.make_async_copy(h_ref, xs_ref.at[pl.ds(0, tm)], sem).wait()


def _dispatch(h, dest, bounds, fill_from, rows, tm, xs_so_far):
    n = h.shape[0]
    first = xs_so_far is None
    in_specs = [pl.BlockSpec((1, TOP_K, tm), lambda i, *_: (i, 0, 0)), _row_tile_spec(tm)]
    args = [bounds, fill_from, dest, h]
    if not first:
        in_specs.append(pl.BlockSpec(memory_space=pl.ANY))
        args.append(xs_so_far)
    return pl.pallas_call(
        functools.partial(_dispatch_body, tm, first),
        grid_spec=pltpu.PrefetchScalarGridSpec(
            num_scalar_prefetch=2,
            grid=(n // tm,),
            in_specs=in_specs,
            out_specs=pl.BlockSpec(memory_space=pl.ANY),
            scratch_shapes=[pltpu.SMEM((TOP_K, tm), I32), pltpu.VMEM((MOE_TILE,) + ROW_TILE, F32),
                            pltpu.SemaphoreType.DMA],
        ),
        out_shape=jax.ShapeDtypeStruct((rows,) + ROW_TILE, F32),
        input_output_aliases={} if first else {len(args) - 1: 0},
        compiler_params=_params(1),
        name="dispatch",
    )(*args)


def _experts_body(tm, te_ref, nv_ref, x_ref, wg_ref, wu_ref, wd_ref, y_ref):
    valid = nv_ref[pl.program_id(0)]

    @pl.when(valid > 0)
    def _():
        x = _load_row_tiles(x_ref).astype(BF16)
        act = (_silu(_dot(x, wg_ref[...])) * _dot(x, wu_ref[...])).astype(BF16)
        _store_row_tiles(y_ref, _dot(act, wd_ref[...]))

    @pl.when(valid == 0)
    def _():
        y_ref[...] = jnp.zeros_like(y_ref)


def _experts(xs, tile_expert, tile_valid, mw, tm):
    rows = xs.shape[0]
    d_ff = mw["w_gate"].shape[2]
    wspec = lambda a, b: pl.BlockSpec((None, a, b), lambda i, te, nv: (te[i], 0, 0),
                                      pipeline_mode=pl.Buffered(1))
    return pl.pallas_call(
        functools.partial(_experts_body, tm),
        grid_spec=pltpu.PrefetchScalarGridSpec(
            num_scalar_prefetch=2,
            grid=(rows // tm,),
            in_specs=[_row_tile_spec(tm), wspec(D_MODEL, d_ff), wspec(D_MODEL, d_ff), wspec(d_ff, D_MODEL)],
            out_specs=_row_tile_spec(tm),
        ),
        out_shape=jax.ShapeDtypeStruct((rows,) + ROW_TILE, F32),
        compiler_params=_params(1),
        name="experts",
    )(tile_expert, tile_valid, xs, mw["w_gate"], mw["w_up"], mw["w_down"])


def _combine_body(tm, final, dest_ref, dnext_ref, x1_ref, rf_ref, ys_ref, *rest):
    if final:
        fw_ref, o_ref, buf_ref, dest_smem, sems = rest
    else:
        o_ref, buf_ref, dest_smem, sems = rest
    i = pl.program_id(0)
    nsteps = pl.num_programs(0)
    slot = i % 2

    def start_row(j, s):
        for k in range(TOP_K):
            pltpu.make_async_copy(ys_ref.at[dest_smem[k, j]], buf_ref.at[s, k, j], sems.at[s]).start(priority=k)

    def wait_rows(s):
        for k in range(TOP_K):
            pltpu.make_async_copy(ys_ref.at[pl.ds(0, tm)], buf_ref.at[s, k], sems.at[s]).wait()

    def gather(src_ref, s):
        pltpu.sync_copy(src_ref.at[0], dest_smem)
        lax.fori_loop(0, tm, lambda j, c: (start_row(j, s), c)[1], 0, unroll=8)

    @pl.when(i == 0)
    def _():
        gather(dest_ref, 0)

    wait_rows(slot)

    @pl.when(i + 1 < nsteps)
    def _():
        gather(dnext_ref, 1 - slot)

    gates = rf_ref[...]
    out = x1_ref[...] + gates[:, 0:1] * _load_row_tiles(buf_ref, slot, 0) + \
        gates[:, 1:2] * _load_row_tiles(buf_ref, slot, 1)
    if final:
        out = _rms(out, fw_ref[...])
    o_ref[...] = out


def _combine(dest, x1, rf, ys, final_w, tm):
    n = x1.shape[0]
    nt = n // tm
    row = lambda w: pl.BlockSpec((tm, w), lambda i: (i, 0))
    in_specs = [pl.BlockSpec((1, TOP_K, tm), lambda i: (i, 0, 0)),
                pl.BlockSpec((1, TOP_K, tm), lambda i: (jnp.minimum(i + 1, nt - 1), 0, 0)),
                row(D_MODEL), row(LANES), pl.BlockSpec(memory_space=pl.ANY)]
    args = [dest, dest, x1, rf, ys]
    if final_w is not None:
        in_specs.append(_const_spec((1, D_MODEL)))
        args.append(final_w)
    return pl.pallas_call(
        functools.partial(_combine_body, tm, final_w is not None),
        grid=(nt,),
        in_specs=in_specs,
        out_specs=row(D_MODEL),
        out_shape=jax.ShapeDtypeStruct((n, D_MODEL), F32),
        scratch_shapes=[pltpu.VMEM((2, TOP_K, tm) + ROW_TILE, F32), pltpu.SMEM((TOP_K, tm), I32),
                        pltpu.SemaphoreType.DMA((2,))],
        compiler_params=_params(1),
        name="combine",
    )(*args)


def _moe_layer(streams, lw, mw, final_w):
    routed = [_router_layer(x, ys, ym, lw, mw, min(TOKEN_TILE, x.shape[0])) for x, ys, ym in streams]
    stream_counts = [r[4][0, :N_EXPERTS].astype(I32) for r in routed]
    counts = functools.reduce(lambda a, b: a + b, stream_counts)
    padded = (counts + MOE_TILE - 1) // MOE_TILE * MOE_TILE
    ends = jnp.cumsum(padded)
    offsets = ends - padded
    n_total = sum(x.shape[0] for x, _, _ in streams)
    rows = (n_total * TOP_K + N_EXPERTS * (MOE_TILE - 1)) // MOE_TILE * MOE_TILE
    tile_start = jnp.arange(rows // MOE_TILE, dtype=I32) * MOE_TILE
    tile_expert = jnp.minimum(jnp.sum(tile_start[:, None] >= ends[None, :], axis=1), N_EXPERTS - 1).astype(I32)
    group_end = (offsets + counts)[tile_expert]
    tile_valid = jnp.where(tile_start < ends[-1], jnp.clip(group_end - tile_start, 0, MOE_TILE), 0).astype(I32)
    bounds = jnp.concatenate([jnp.zeros((1,), I32), ends.astype(I32)])

    def tiles(dest, t):
        return dest.reshape(TOP_K, dest.shape[1] // t, t).transpose(1, 0, 2)

    xs, dests, start = None, [], offsets
    for (x1, h, ri, rf, _), own in zip(routed, stream_counts):
        n = x1.shape[0]
        dest = jnp.stack([start[ri[0]] + ri[2], start[ri[1]] + ri[3]])
        dests.append(dest)
        td = min(DISPATCH_TILE, n)
        start = start + own
        xs = _dispatch(h, tiles(dest, td), bounds, start // MOE_TILE * MOE_TILE, rows, td, xs)
    y_sorted = _experts(xs, tile_expert, tile_valid, mw, MOE_TILE)
    outs = []
    for (x1, h, ri, rf, _), dest in zip(routed, dests):
        tc = min(COMBINE_TILE, x1.shape[0])
        outs.append(_combine(tiles(dest, tc), x1, rf, y_sorted, final_w, tc))
    return outs


def _prep_layer(layer, norm_mix, w_in, conv_w, conv_b, dt_bias, a_log, d_skip, ssd_norm, q_norm, w_q_up,
                kv_norm, w_kv_up, w_out, norm_ffn):
    w = w_in[layer]
    splits = np.cumsum([SSD_INNER, CONV_DIM, SSD_HEADS, Q_LORA, KV_LORA, ROPE_DIM])[:-1].tolist()
    w_z, w_xbc, w_dt, w_cq, w_ckv, w_kr = jnp.split(w, splits, axis=1)
    half = ROPE_DIM // 2
    w_krs = jnp.concatenate([w_kr[:, half:], w_kr[:, :half]], axis=1)
    pad = jnp.zeros((D_MODEL, LANES - SSD_HEADS), F32)
    w_cat = jnp.concatenate([w_z, w_xbc, w_cq, w_ckv, w_kr, w_krs, w_dt, pad], axis=1).astype(BF16)
    wq = w_q_up[layer].reshape(Q_LORA, MLA_HEADS, QK_DIM)
    wq = jnp.concatenate([wq, wq[:, :, NOPE_DIM + half:], wq[:, :, NOPE_DIM:NOPE_DIM + half]], axis=2)
    row = lambda v: v.reshape(1, -1).astype(F32)
    return {
        "norm_mix": row(norm_mix[layer]), "w_in": w_cat,
        "q_norm": row(q_norm[layer]), "w_q": wq.reshape(Q_LORA, MLA_HEADS * Q_HEAD_COLS).astype(BF16),
        "kv_norm": row(kv_norm[layer]), "w_kv": w_kv_up[layer].astype(BF16),
        "dt_bias": jnp.pad(row(dt_bias[layer]), ((0, 0), (0, LANES - SSD_HEADS))),
        "conv_w": conv_w[layer], "conv_b": row(conv_b[layer]), "a_log": row(a_log[layer]),
        "d_skip": row(jnp.repeat(d_skip[layer], SSD_HEADDIM)), "ssd_norm": row(ssd_norm[layer]),
        "w_out": w_out[layer].astype(BF16), "norm_ffn": row(norm_ffn[layer]),
    }


def _rope_tables(pos, tm):
    half = ROPE_DIM // 2
    inv_freq = ROPE_THETA ** (-jnp.arange(half, dtype=F32) / half)
    ang = pos.astype(F32)[:, None] * inv_freq[None, :]
    cos, sin = jnp.cos(ang), jnp.sin(ang)
    cos_tab = jnp.concatenate([cos, cos], axis=1)
    sin_tab = jnp.concatenate([-sin, sin], axis=1)
    reps = max(1, tm // pos.shape[0])
    return jnp.tile(cos_tab, (reps, 1)), jnp.tile(sin_tab, (reps, 1))


def _mixing(x, bsz, length, lw, tables, layer, conv0, ssm0, past_lat, past_kr):
    n = bsz * length
    tm = min(TOKEN_TILE, n)
    z, xbc, dt, c_kv, k_rope, q, k, v = _inproj(x, lw, tables[0], tables[1], tm)
    ys, new_ssm, new_conv = _ssd(xbc, dt, z, lw, bsz, length, layer, conv0, ssm0)
    if past_lat is None:
        ym = _attention_prompt(q, k, v, bsz, length)
    else:
        past = past_lat.shape[2]
        ym = _attention_cache(q, k, v, past_lat.reshape(-1, KV_LORA), past_kr.reshape(-1, ROPE_DIM),
                              lw["w_kv"], bsz, length, past, layer)
    return ys, ym, c_kv, k_rope, new_ssm, new_conv


def kernel(x_prompt, x_sample, cache_kv_latent, cache_k_rope, state_ssm, state_conv, norm_mix, w_in, conv_w,
           conv_b, dt_bias, a_log, d_skip, ssd_norm, q_norm, w_q_up, kv_norm, w_kv_up, w_out, norm_ffn,
           ffn_w_gate, ffn_w_up, ffn_w_down, moe_router, moe_w_gate, moe_w_up, moe_w_down, final_norm):
    bp, sp = x_prompt.shape[:2]
    bs, ss = x_sample.shape[:2]
    depth = w_in.shape[0]
    past = cache_kv_latent.shape[2]
    tabs_p = _rope_tables(jnp.arange(sp, dtype=I32), min(TOKEN_TILE, bp * sp))
    tabs_s = _rope_tables(past + jnp.arange(ss, dtype=I32), min(TOKEN_TILE, bs * ss))
    final_w = final_norm.reshape(1, D_MODEL)
    hp = x_prompt.reshape(bp * sp, D_MODEL)
    hs = x_sample.reshape(bs * ss, D_MODEL)
    outs_p, outs_s = [], []
    for layer in range(depth):
        lw = _prep_layer(layer, norm_mix, w_in, conv_w, conv_b, dt_bias, a_log, d_skip, ssd_norm, q_norm,
                         w_q_up, kv_norm, w_kv_up, w_out, norm_ffn)
        last = final_w if layer == depth - 1 else None
        mix_p = _mixing(hp, bp, sp, lw, tabs_p, layer, None, None, None, None)
        mix_s = _mixing(hs, bs, ss, lw, tabs_s, layer, state_conv, state_ssm, cache_kv_latent, cache_k_rope)
        outs_p.append(mix_p[2:])
        outs_s.append(mix_s[2:])
        idx = layer // 2
        if layer % 2 == 0:
            fw = {"w_gate": ffn_w_gate[idx].astype(BF16), "w_up": ffn_w_up[idx].astype(BF16),
                  "w_down": ffn_w_down[idx].astype(BF16)}
            hp = _dense_layer(hp, mix_p[0], mix_p[1], lw, fw, last, min(TOKEN_TILE, bp * sp))
            hs = _dense_layer(hs, mix_s[0], mix_s[1], lw, fw, last, min(TOKEN_TILE, bs * ss))
        else:
            wr_hi = moe_router[idx].astype(BF16)
            wr_lo = (moe_router[idx] - wr_hi.astype(F32)).astype(BF16)
            mw = {"w_router": jnp.pad(jnp.concatenate([wr_hi, wr_lo], axis=1),
                                      ((0, 0), (0, LANES - 2 * N_EXPERTS))),
                  "w_gate": moe_w_gate[idx].astype(BF16), "w_up": moe_w_up[idx].astype(BF16),
                  "w_down": moe_w_down[idx].astype(BF16)}
            hp, hs = _moe_layer([(hp, mix_p[0], mix_p[1]), (hs, mix_s[0], mix_s[1])], lw, mw, last)

    def stack(outs, j, shape):
        return jnp.stack([o[j].reshape(shape) for o in outs])

    return (hp.reshape(bp, sp, D_MODEL), hs.reshape(bs, ss, D_MODEL),
            stack(outs_p, 0, (bp, sp, KV_LORA)), stack(outs_p, 1, (bp, sp, ROPE_DIM)),
            stack(outs_p, 2, (bp, SSD_HEADS, SSD_HEADDIM, SSD_STATE)),
            stack(outs_p, 3, (bp, CONV_WIDTH - 1, CONV_DIM)),
            stack(outs_s, 0, (bs, ss, KV_LORA)), stack(outs_s, 1, (bs, ss, ROPE_DIM)),
            stack(outs_s, 2, (bs, SSD_HEADS, SSD_HEADDIM, SSD_STATE)),
            stack(outs_s, 3, (bs, CONV_WIDTH - 1, CONV_DIM)))
```

```python
import functools
import math

import jax
import jax.numpy as jnp
import numpy as np
from jax import lax
from jax.experimental import pallas as pl
from jax.experimental.pallas import tpu as pltpu

F32 = jnp.float32
BF16 = jnp.bfloat16
I32 = jnp.int32

D_MODEL = 1024
CHUNK = 64
LOG2_CHUNK = 6
SSD_HEADS = 16
SSD_HEADDIM = 64
SSD_INNER = SSD_HEADS * SSD_HEADDIM
SSD_GROUPS = 2
HEADS_PER_GROUP = SSD_HEADS // SSD_GROUPS
SSD_STATE = 128
CONV_WIDTH = 4
CONV_DIM = SSD_INNER + 2 * SSD_GROUPS * SSD_STATE
MLA_HEADS = 8
Q_LORA = 384
KV_LORA = 256
NOPE_DIM = 128
ROPE_DIM = 64
V_DIM = 128
QK_DIM = NOPE_DIM + ROPE_DIM
ROPE_THETA = 10000.0
MLA_INNER = MLA_HEADS * V_DIM
N_EXPERTS = 8
TOP_K = 2
EPS = 1e-6

LANES = 128
SUBLANES = 8
MXU_DIM = 256
VMEM_LIMIT = 56 * 1024 * 1024

OFF_Z = 0
OFF_XBC = OFF_Z + SSD_INNER
OFF_CQ = OFF_XBC + CONV_DIM
OFF_CKV = OFF_CQ + Q_LORA
OFF_KR = OFF_CKV + KV_LORA
OFF_KRS = OFF_KR + ROPE_DIM
OFF_DT = OFF_KRS + ROPE_DIM
IN_COLS = OFF_DT + LANES
Q_HEAD_COLS = NOPE_DIM + 2 * ROPE_DIM

TOKEN_TILE = 512
ATT_TILE = 512
ATT_HEADS_PER_STEP = 4
SSD_CHUNKS_PER_STEP = 8
MOE_TILE = 256
DISPATCH_TILE = 2048
COMBINE_TILE = 1024


def _params(n_axes):
    return pltpu.CompilerParams(dimension_semantics=("arbitrary",) * n_axes,
                                vmem_limit_bytes=VMEM_LIMIT)


def _const_spec(shape):
    zeros = (0,) * len(shape)
    return pl.BlockSpec(shape, lambda *_: zeros)


def _rms(x, w):
    return x * lax.rsqrt(jnp.mean(x * x, axis=-1, keepdims=True) + EPS) * w


def _silu(x):
    return x * (1.0 / (1.0 + jnp.exp(-x)))


def _dot(a, b):
    return jnp.dot(a, b, preferred_element_type=F32)


def _dot_nt(a, b):
    return lax.dot_general(a, b, (((1,), (1,)), ((), ())), preferred_element_type=F32)


def _dot_tn(a, b):
    return lax.dot_general(a, b, (((0,), (0,)), ((), ())), preferred_element_type=F32)


def _split3(x):
    hi = x.astype(BF16)
    rest = x - hi.astype(F32)
    mid = rest.astype(BF16)
    lo = (rest - mid.astype(F32)).astype(BF16)
    return jnp.concatenate([hi, mid, lo], axis=1)


ROW_TILE = (D_MODEL // LANES, LANES)


def _row_tile_spec(tm):
    return pl.BlockSpec((tm,) + ROW_TILE, lambda i, *_: (i, 0, 0))


def _store_row_tiles(ref, x):
    for s in range(ROW_TILE[0]):
        ref[:, s, :] = x[:, s * LANES:(s + 1) * LANES]


def _load_row_tiles(ref, *lead):
    return jnp.concatenate([ref[lead + (slice(None), s, slice(None))] for s in range(ROW_TILE[0])], axis=1)


def _inproj_body(x_ref, nw_ref, win_ref, qnw_ref, wq_ref, kvnw_ref, wkv_ref, cos_ref, sin_ref, dtb_ref,
                 z_ref, xbc_ref, dt_ref, ckv_ref, kr_ref, q_ref, k_ref, v_ref):
    h = _rms(x_ref[...], nw_ref[...]).astype(BF16)
    proj = _dot(h, win_ref[...])
    z_ref[...] = proj[:, OFF_Z:OFF_XBC]
    xbc_ref[...] = proj[:, OFF_XBC:OFF_CQ]
    pre = proj[:, OFF_DT:IN_COLS] + dtb_ref[...]
    dt_ref[...] = jnp.maximum(pre, 0.0) + jnp.log(1.0 + jnp.exp(-jnp.abs(pre)))
    cos = cos_ref[...]
    sin = sin_ref[...]
    k_rope = proj[:, OFF_KR:OFF_KRS] * cos + proj[:, OFF_KRS:OFF_DT] * sin
    kr_ref[...] = k_rope
    c_kv = _rms(proj[:, OFF_CKV:OFF_KR], kvnw_ref[...])
    ckv_ref[...] = c_kv
    qn = _rms(proj[:, OFF_CQ:OFF_CKV], qnw_ref[...]).astype(BF16)
    q_all = _dot(qn, wq_ref[...])
    kv_all = _dot(c_kv.astype(BF16), wkv_ref[...])
    k_rope_b = k_rope.astype(BF16)
    for hd in range(MLA_HEADS):
        qb = q_all[:, hd * Q_HEAD_COLS:(hd + 1) * Q_HEAD_COLS]
        q_rope = qb[:, NOPE_DIM:QK_DIM] * cos + qb[:, QK_DIM:Q_HEAD_COLS] * sin
        q_ref[hd] = jnp.concatenate([qb[:, :NOPE_DIM], q_rope], axis=1).astype(BF16)
        kvb = kv_all[:, hd * (NOPE_DIM + V_DIM):(hd + 1) * (NOPE_DIM + V_DIM)]
        k_ref[hd] = jnp.concatenate([kvb[:, :NOPE_DIM].astype(BF16), k_rope_b], axis=1)
        v_ref[hd] = kvb[:, NOPE_DIM:].astype(BF16)


def _inproj(x, lw, cos_tab, sin_tab, tm):
    n = x.shape[0]
    nt = n // tm
    tab_blocks = cos_tab.shape[0] // tm
    row = lambda w: pl.BlockSpec((tm, w), lambda i: (i, 0))
    head = lambda w: pl.BlockSpec((MLA_HEADS, tm, w), lambda i: (0, i, 0))
    tab = pl.BlockSpec((tm, ROPE_DIM), lambda i: (i % tab_blocks, 0))
    return pl.pallas_call(
        _inproj_body,
        grid=(nt,),
        in_specs=[row(D_MODEL), _const_spec((1, D_MODEL)), _const_spec((D_MODEL, IN_COLS)),
                  _const_spec((1, Q_LORA)), _const_spec((Q_LORA, MLA_HEADS * Q_HEAD_COLS)),
                  _const_spec((1, KV_LORA)), _const_spec((KV_LORA, MLA_HEADS * (NOPE_DIM + V_DIM))),
                  tab, tab, _const_spec((1, LANES))],
        out_specs=[row(SSD_INNER), row(CONV_DIM), row(LANES), row(KV_LORA), row(ROPE_DIM),
                   head(QK_DIM), head(QK_DIM), head(V_DIM)],
        out_shape=[jax.ShapeDtypeStruct((n, SSD_INNER), F32), jax.ShapeDtypeStruct((n, CONV_DIM), F32),
                   jax.ShapeDtypeStruct((n, LANES), F32), jax.ShapeDtypeStruct((n, KV_LORA), F32),
                   jax.ShapeDtypeStruct((n, ROPE_DIM), F32),
                   jax.ShapeDtypeStruct((MLA_HEADS, n, QK_DIM), BF16),
                   jax.ShapeDtypeStruct((MLA_HEADS, n, QK_DIM), BF16),
                   jax.ShapeDtypeStruct((MLA_HEADS, n, V_DIM), BF16)],
        compiler_params=_params(1),
        name="inproj",
    )(x, lw["norm_mix"], lw["w_in"], lw["q_norm"], lw["w_q"], lw["kv_norm"], lw["w_kv"],
      cos_tab, sin_tab, lw["dt_bias"])


def _ssd_body(t, cps, has_init, *refs):
    if has_init:
        (xbc_ref, dt_ref, z_ref, cw_ref, cb_ref, alog_ref, dskip_ref, nw_ref, conv0_ref, ssm0_ref,
         y_ref, ssm_ref, conv_ref, hist_ref) = refs
    else:
        (xbc_ref, dt_ref, z_ref, cw_ref, cb_ref, alog_ref, dskip_ref, nw_ref,
         y_ref, ssm_ref, conv_ref, hist_ref) = refs
    rows = cps * t
    hist_rows = SUBLANES
    first = hist_rows - (CONV_WIDTH - 1)
    group_lanes = HEADS_PER_GROUP * t
    group_cols = HEADS_PER_GROUP * SSD_HEADDIM
    heads_per_block = MXU_DIM // t
    blocks_per_group = group_lanes // MXU_DIM
    log2_t = int(math.log2(t))
    log2_p = int(math.log2(SSD_HEADDIM))

    @pl.when(pl.program_id(1) == 0)
    def _():
        hist_ref[...] = jnp.zeros_like(hist_ref)
        if has_init:
            hist_ref[first:hist_rows, :] = conv0_ref[...]
            ssm_ref[...] = ssm0_ref[...]
        else:
            ssm_ref[...] = jnp.zeros_like(ssm_ref)

    x_raw = xbc_ref[...]
    hist_ref[hist_rows:, :] = x_raw
    padded = hist_ref[...]
    conv = cb_ref[...]
    for k in range(CONV_WIDTH):
        back = CONV_WIDTH - 1 - k
        tap = x_raw if back == 0 else pltpu.roll(padded, back, axis=0)[hist_rows:, :]
        conv = conv + tap * cw_ref[k:k + 1, :]
    hist_ref[first:hist_rows, :] = x_raw[rows - (CONV_WIDTH - 1):rows]
    conv_ref[...] = x_raw[rows - (CONV_WIDTH - 1):rows]
    act = _silu(conv)
    xs = act[:, :SSD_INNER]
    b_mat = act[:, SSD_INNER:SSD_INNER + SSD_GROUPS * SSD_STATE].astype(BF16)
    c_mat = act[:, SSD_INNER + SSD_GROUPS * SSD_STATE:].astype(BF16)

    dt = dt_ref[:, :SSD_HEADS]
    da = dt * (-jnp.exp(alog_ref[...]))
    ri = lax.broadcasted_iota(I32, (rows, rows), 0)
    ci = lax.broadcasted_iota(I32, (rows, rows), 1)
    tril = ((ri >= ci) & ((ri >> log2_t) == (ci >> log2_t))).astype(BF16)
    cs3 = _dot(tril, _split3(da))
    cs = cs3[:, :SSD_HEADS] + cs3[:, SSD_HEADS:2 * SSD_HEADS] + cs3[:, 2 * SSD_HEADS:]

    def expander(width, shift):
        e_rows = lax.broadcasted_iota(I32, (3 * SSD_HEADS, width), 0)
        return ((lax.broadcasted_iota(I32, (3 * SSD_HEADS, width), 1) >> shift) == (e_rows & (SSD_HEADS - 1))
                ).astype(BF16)

    per_p = _dot(_split3(jnp.concatenate([cs, dt], axis=0)), expander(SSD_INNER, log2_p))
    cs_p, dt_p = per_p[:rows], per_p[rows:]
    cs_t = cs_p if t == SSD_HEADDIM else _dot(_split3(cs), expander(SSD_HEADS * t, log2_t))
    xdt = xs * dt_p
    lane = lax.broadcasted_iota(I32, (t, SSD_HEADS * t), 1) & (t - 1)
    rowi = lax.broadcasted_iota(I32, (t, SSD_HEADS * t), 0)
    block_mask = (lax.broadcasted_iota(I32, (MXU_DIM, heads_per_block * SSD_HEADDIM), 0) >> log2_t) == \
        (lax.broadcasted_iota(I32, (MXU_DIM, heads_per_block * SSD_HEADDIM), 1) >> log2_p)

    y_chunks = []
    for ch in range(cps):
        r0, r1 = ch * t, (ch + 1) * t
        cs_t_c, cs_p_c = cs_t[r0:r1], cs_p[r0:r1]
        cs_src = jnp.sum(jnp.where(lane == rowi, cs_t_c, 0.0), axis=0, keepdims=True)
        decay = jnp.exp(jnp.where(rowi >= lane, cs_t_c - cs_src, -jnp.inf))
        exp_cs = jnp.exp(cs_p_c)
        x_end = (xdt[r0:r1] * jnp.exp(cs_p_c[t - 1:t, :] - cs_p_c)).astype(BF16)
        chunk_decay = jnp.exp(cs[r1 - 1:r1, :])
        y_blocks = []
        for g in range(SSD_GROUPS):
            bg = b_mat[r0:r1, g * SSD_STATE:(g + 1) * SSD_STATE]
            cg = c_mat[r0:r1, g * SSD_STATE:(g + 1) * SSD_STATE]
            cb = _dot_nt(cg, jnp.concatenate([bg] * HEADS_PER_GROUP, axis=0))
            m = (cb * decay[:, g * group_lanes:(g + 1) * group_lanes]).astype(BF16)
            h0, h1 = g * HEADS_PER_GROUP, (g + 1) * HEADS_PER_GROUP
            state = ssm_ref[h0:h1].reshape(group_cols, SSD_STATE)
            y_off = _dot_nt(cg, state.astype(BF16)) * exp_cs[:, g * group_cols:(g + 1) * group_cols]
            for kb in range(blocks_per_group):
                c0 = g * group_cols + kb * heads_per_block * SSD_HEADDIM
                c1 = c0 + heads_per_block * SSD_HEADDIM
                xb = xdt[r0:r1, c0:c1]
                bd = jnp.where(block_mask, jnp.concatenate([xb] * heads_per_block, axis=0), 0.0).astype(BF16)
                y_diag = _dot(m[:, kb * MXU_DIM:(kb + 1) * MXU_DIM], bd)
                y_blocks.append(y_diag + y_off[:, c0 - g * group_cols:c1 - g * group_cols])
            local = _dot_tn(x_end[:, g * group_cols:(g + 1) * group_cols], bg)
            for r in range(HEADS_PER_GROUP):
                hd = h0 + r
                ssm_ref[hd] = ssm_ref[hd] * chunk_decay[:, hd:hd + 1] + \
                    local[r * SSD_HEADDIM:(r + 1) * SSD_HEADDIM]
        y_chunks.append(jnp.concatenate(y_blocks, axis=1))
    y = jnp.concatenate(y_chunks, axis=0) + xs * dskip_ref[...]
    y_ref[...] = _rms(y * _silu(z_ref[...]), nw_ref[...]).astype(BF16)


def _ssd(xbc, dt, z, lw, bsz, length, layer, conv0, ssm0):
    t = CHUNK if length % CHUNK == 0 else length
    nc = length // t
    cps = max(c for c in (1, 2, SSD_CHUNKS_PER_STEP) if nc % c == 0)
    steps = nc // cps
    n = bsz * length
    has_init = conv0 is not None
    row = lambda w: pl.BlockSpec((cps * t, w), lambda b, c: (b * steps + c, 0))
    conv_spec = pl.BlockSpec((None, CONV_WIDTH - 1, CONV_DIM), lambda b, c: (b, 0, 0))
    ssm_spec = pl.BlockSpec((None, SSD_HEADS, SSD_HEADDIM, SSD_STATE), lambda b, c: (b, 0, 0, 0))
    in_specs = [row(CONV_DIM), row(LANES), row(SSD_INNER), _const_spec((CONV_WIDTH, CONV_DIM)),
                _const_spec((1, CONV_DIM)), _const_spec((1, SSD_HEADS)), _const_spec((1, SSD_INNER)),
                _const_spec((1, SSD_INNER))]
    args = [xbc, dt, z, lw["conv_w"], lw["conv_b"], lw["a_log"], lw["d_skip"], lw["ssd_norm"]]
    if has_init:
        in_specs += [pl.BlockSpec((None, None, CONV_WIDTH - 1, CONV_DIM), lambda b, c: (layer, b, 0, 0)),
                     pl.BlockSpec((None, None, SSD_HEADS, SSD_HEADDIM, SSD_STATE),
                                  lambda b, c: (layer, b, 0, 0, 0))]
        args += [conv0, ssm0]
    return pl.pallas_call(
        functools.partial(_ssd_body, t, cps, has_init),
        grid=(bsz, steps),
        in_specs=in_specs,
        out_specs=[row(SSD_INNER), ssm_spec, conv_spec],
        out_shape=[jax.ShapeDtypeStruct((n, SSD_INNER), BF16),
                   jax.ShapeDtypeStruct((bsz, SSD_HEADS, SSD_HEADDIM, SSD_STATE), F32),
                   jax.ShapeDtypeStruct((bsz, CONV_WIDTH - 1, CONV_DIM), F32)],
        scratch_shapes=[pltpu.VMEM((SUBLANES + cps * t, CONV_DIM), F32)],
        compiler_params=_params(2),
        name="ssd",
    )(*args)


def _lane_groups(x):
    return [x[:, g * LANES:(g + 1) * LANES] for g in range(x.shape[1] // LANES)]


def _attn_body(tq, nh, q_ref, k_ref, v_ref, o_ref, s_ref, m_ref, l_ref, acc_ref):
    qi = pl.program_id(2)
    heads = range(nh)
    q = [q_ref[h] for h in heads]
    c = (QK_DIM ** -0.5) * math.log2(math.e)

    def keys(ref, h, j):
        return ref[h, pl.ds(pl.multiple_of(j * tq, tq), tq), :]

    def lane_max(m, s):
        for part in _lane_groups(s):
            m = jnp.maximum(m, part)
        return m

    m_ref[...] = jnp.full_like(m_ref, -jnp.inf)

    @pl.loop(0, qi)
    def _(j):
        for h in heads:
            s = _dot_nt(q[h], keys(k_ref, h, j)) * c
            s_ref[h, j] = s
            m_ref[h] = lane_max(m_ref[h], s)

    rows = lax.broadcasted_iota(I32, (tq, tq), 0) >> LOG2_CHUNK
    cols = lax.broadcasted_iota(I32, (tq, tq), 1) >> LOG2_CHUNK
    for h in heads:
        s = jnp.where(cols <= rows, _dot_nt(q[h], keys(k_ref, h, qi)) * c, -jnp.inf)
        s_ref[h, qi] = s
        row_max = jnp.max(lane_max(m_ref[h], s), axis=-1, keepdims=True)
        m_ref[h] = jnp.broadcast_to(row_max, (tq, LANES))
    l_ref[...] = jnp.zeros_like(l_ref)
    acc_ref[...] = jnp.zeros_like(acc_ref)

    @pl.loop(0, qi + 1)
    def _(j):
        for h in heads:
            mc = m_ref[h]
            parts = [jnp.exp2(part - mc) for part in _lane_groups(s_ref[h, j])]
            l_ref[h] = l_ref[h] + functools.reduce(lambda a, b: a + b, parts)
            p = jnp.concatenate(parts, axis=1).astype(BF16)
            acc_ref[h] = acc_ref[h] + _dot(p, keys(v_ref, h, j))

    o_ref[...] = jnp.concatenate(
        [acc_ref[h] / jnp.sum(l_ref[h], axis=-1, keepdims=True) for h in heads], axis=1).astype(BF16)


def _attention_prompt(q, k, v, bsz, length):
    tq = min(ATT_TILE, length)
    nh = ATT_HEADS_PER_STEP
    nq = length // tq
    n = bsz * length
    return pl.pallas_call(
        functools.partial(_attn_body, tq, nh),
        grid=(bsz, MLA_HEADS // nh, nq),
        in_specs=[pl.BlockSpec((nh, tq, QK_DIM), lambda b, h, i: (h, b * nq + i, 0)),
                  pl.BlockSpec((nh, length, QK_DIM), lambda b, h, i: (h, b, 0), pipeline_mode=pl.Buffered(1)),
                  pl.BlockSpec((nh, length, V_DIM), lambda b, h, i: (h, b, 0), pipeline_mode=pl.Buffered(1))],
        out_specs=pl.BlockSpec((tq, nh * V_DIM), lambda b, h, i: (b * nq + i, h)),
        out_shape=jax.ShapeDtypeStruct((n, MLA_INNER), BF16),
        scratch_shapes=[pltpu.VMEM((nh, nq, tq, tq), F32), pltpu.VMEM((nh, tq, LANES), F32),
                        pltpu.VMEM((nh, tq, LANES), F32), pltpu.VMEM((nh, tq, V_DIM), F32)],
        compiler_params=_params(3),
        name="attn_prompt",
    )(q, k, v)


def _attn_cache_body(past, length, q_ref, kn_ref, vn_ref, lat_ref, kr_ref, wkv_ref, o_ref):
    scale = QK_DIM ** -0.5
    q_chunk = (past + np.arange(length)) // CHUNK
    past_visible = (np.arange(past) // CHUNK)[None, :] <= q_chunk[:, None]
    new_visible = q_chunk[None, :] <= q_chunk[:, None]
    kv_past = _dot(lat_ref[...].astype(BF16), wkv_ref[...])
    kr_past = kr_ref[...].astype(BF16)
    outs = []
    for hd in range(MLA_HEADS):
        q = q_ref[hd]
        kvb = kv_past[:, hd * (NOPE_DIM + V_DIM):(hd + 1) * (NOPE_DIM + V_DIM)]
        k_past = jnp.concatenate([kvb[:, :NOPE_DIM].astype(BF16), kr_past], axis=1)
        v_past = kvb[:, NOPE_DIM:].astype(BF16)
        s_p = _dot_nt(q, k_past) * scale
        s_n = _dot_nt(q, kn_ref[hd]) * scale
        if not past_visible.all():
            rows = (past + lax.broadcasted_iota(I32, (length, past), 0)) >> LOG2_CHUNK
            cols = lax.broadcasted_iota(I32, (length, past), 1) >> LOG2_CHUNK
            s_p = jnp.where(cols <= rows, s_p, -jnp.inf)
        if not new_visible.all():
            rows = (past + lax.broadcasted_iota(I32, (length, length), 0)) >> LOG2_CHUNK
            cols = (past + lax.broadcasted_iota(I32, (length, length), 1)) >> LOG2_CHUNK
            s_n = jnp.where(cols <= rows, s_n, -jnp.inf)
        m = jnp.maximum(jnp.max(s_p, axis=-1, keepdims=True), jnp.max(s_n, axis=-1, keepdims=True))
        p_p = jnp.exp(s_p - m)
        p_n = jnp.exp(s_n - m)
        denom = jnp.sum(p_p, axis=-1, keepdims=True) + jnp.sum(p_n, axis=-1, keepdims=True)
        o = _dot(p_p.astype(BF16), v_past) + _dot(p_n.astype(BF16), vn_ref[hd])
        outs.append(o / denom)
    o_ref[...] = jnp.concatenate(outs, axis=1).astype(BF16)


def _attention_cache(q, k_new, v_new, lat_past, kr_past, w_kv, bsz, length, past, layer):
    n = bsz * length
    new = lambda w: pl.BlockSpec((MLA_HEADS, length, w), lambda b: (0, b, 0))
    old = lambda w: pl.BlockSpec((past, w), lambda b: (layer * bsz + b, 0))
    return pl.pallas_call(
        functools.partial(_attn_cache_body, past, length),
        grid=(bsz,),
        in_specs=[new(QK_DIM), new(QK_DIM), new(V_DIM), old(KV_LORA), old(ROPE_DIM), _const_spec(w_kv.shape)],
        out_specs=pl.BlockSpec((length, MLA_INNER), lambda b: (b, 0)),
        out_shape=jax.ShapeDtypeStruct((n, MLA_INNER), BF16),
        compiler_params=_params(1),
        name="attn_cache",
    )(q, k_new, v_new, lat_past, kr_past, w_kv)


def _mix_out(x_ref, ys_ref, ym_ref, wo_ref):
    return x_ref[...] + _dot(ys_ref[...], wo_ref[:SSD_INNER, :]) + _dot(ym_ref[...], wo_ref[SSD_INNER:, :])


def _dense_body(final, x_ref, ys_ref, ym_ref, wo_ref, nw_ref, wg_ref, wu_ref, wd_ref, *rest):
    x1 = _mix_out(x_ref, ys_ref, ym_ref, wo_ref)
    h = _rms(x1, nw_ref[...]).astype(BF16)
    act = (_silu(_dot(h, wg_ref[...])) * _dot(h, wu_ref[...])).astype(BF16)
    x2 = x1 + _dot(act, wd_ref[...])
    if final:
        fw_ref, o_ref = rest
        o_ref[...] = _rms(x2, fw_ref[...])
    else:
        (o_ref,) = rest
        o_ref[...] = x2


def _resident(shape):
    zeros = (0,) * len(shape)
    return pl.BlockSpec(shape, lambda *_: zeros, pipeline_mode=pl.Buffered(1))


def _dense_layer(x, ys, ym, lw, fw, final_w, tm):
    n = x.shape[0]
    row = lambda w: pl.BlockSpec((tm, w), lambda i: (i, 0))
    d_ff = fw["w_gate"].shape[1]
    in_specs = [row(D_MODEL), row(SSD_INNER), row(MLA_INNER), _resident((SSD_INNER + MLA_INNER, D_MODEL)),
                _const_spec((1, D_MODEL)), _resident((D_MODEL, d_ff)), _resident((D_MODEL, d_ff)),
                _resident((d_ff, D_MODEL))]
    args = [x, ys, ym, lw["w_out"], lw["norm_ffn"], fw["w_gate"], fw["w_up"], fw["w_down"]]
    if final_w is not None:
        in_specs.append(_const_spec((1, D_MODEL)))
        args.append(final_w)
    return pl.pallas_call(
        functools.partial(_dense_body, final_w is not None),
        grid=(n // tm,),
        in_specs=in_specs,
        out_specs=row(D_MODEL),
        out_shape=jax.ShapeDtypeStruct((n, D_MODEL), F32),
        compiler_params=_params(1),
        name="mixout_dense",
    )(*args)


def _router_body(tm, x_ref, ys_ref, ym_ref, wo_ref, nw_ref, wr_ref,
                 x1_ref, h_ref, ri_ref, rf_ref, cnt_ref):
    @pl.when(pl.program_id(0) == 0)
    def _():
        cnt_ref[...] = jnp.zeros_like(cnt_ref)

    x1 = _mix_out(x_ref, ys_ref, ym_ref, wo_ref)
    x1_ref[...] = x1
    h = _rms(x1, nw_ref[...])
    _store_row_tiles(h_ref, h)
    lane_i = lax.broadcasted_iota(I32, (tm, LANES), 1)
    lane = lane_i.astype(F32)
    h_hi = h.astype(BF16)
    h_lo = (h - h_hi.astype(F32)).astype(BF16)
    hi_terms = _dot(h_hi, wr_ref[...])
    logits = hi_terms + pltpu.roll(hi_terms, LANES - N_EXPERTS, axis=1) + _dot(h_lo, wr_ref[...])
    logits = jnp.where(lane_i < N_EXPERTS, logits, -jnp.inf)
    top0 = jnp.max(logits, axis=-1, keepdims=True)
    e0 = jnp.min(jnp.where(logits == top0, lane, float(LANES)), axis=-1, keepdims=True)
    rest = jnp.where(lane == e0, -jnp.inf, logits)
    top1 = jnp.max(rest, axis=-1, keepdims=True)
    e1 = jnp.min(jnp.where(rest == top1, lane, float(LANES)), axis=-1, keepdims=True)
    w1 = jnp.exp(top1 - top0)
    g0 = 1.0 / (1.0 + w1)
    g1 = w1 / (1.0 + w1)
    onehot = jnp.where((lane == e0) | (lane == e1), 1.0, 0.0)
    strict = (lax.broadcasted_iota(I32, (tm, tm), 0) > lax.broadcasted_iota(I32, (tm, tm), 1)).astype(BF16)
    before = _dot(strict, onehot.astype(BF16)) + cnt_ref[0:1, :]
    pos0 = jnp.sum(jnp.where(lane == e0, before, 0.0), axis=-1, keepdims=True)
    pos1 = jnp.sum(jnp.where(lane == e1, before, 0.0), axis=-1, keepdims=True)
    cnt_ref[...] = cnt_ref[...] + jnp.sum(onehot, axis=0, keepdims=True)
    packed = jnp.where(lane_i == 0, e0, jnp.where(lane_i == 1, e1, jnp.where(lane_i == 2, pos0, pos1)))
    ri_ref[...] = packed.T[:SUBLANES, :].astype(I32)
    rf_ref[...] = jnp.where(lane_i == 0, g0, g1)


def _router_layer(x, ys, ym, lw, mw, tm):
    n = x.shape[0]
    row = lambda w: pl.BlockSpec((tm, w), lambda i: (i, 0))
    return pl.pallas_call(
        functools.partial(_router_body, tm),
        grid=(n // tm,),
        in_specs=[row(D_MODEL), row(SSD_INNER), row(MLA_INNER), _const_spec((SSD_INNER + MLA_INNER, D_MODEL)),
                  _const_spec((1, D_MODEL)), _const_spec((D_MODEL, LANES))],
        out_specs=[row(D_MODEL), _row_tile_spec(tm), pl.BlockSpec((SUBLANES, tm), lambda i: (0, i)), row(LANES),
                   _const_spec((SUBLANES, LANES))],
        out_shape=[jax.ShapeDtypeStruct((n, D_MODEL), F32), jax.ShapeDtypeStruct((n,) + ROW_TILE, F32),
                   jax.ShapeDtypeStruct((SUBLANES, n), I32), jax.ShapeDtypeStruct((n, LANES), F32),
                   jax.ShapeDtypeStruct((SUBLANES, LANES), F32)],
        compiler_params=_params(1),
        name="mixout_router",
    )(x, ys, ym, lw["w_out"], lw["norm_ffn"], mw["w_router"])


def _dispatch_body(tm, first, bounds_ref, fill_ref, dest_ref, h_ref, *rest):
    xs_ref, dest_smem, zero_ref, sem = rest[-4:]

    def zero_padding():
        zero_ref[...] = jnp.zeros_like(zero_ref)

        def fill(start):
            copy = pltpu.make_async_copy(zero_ref, xs_ref.at[pl.ds(pl.multiple_of(start, MOE_TILE), MOE_TILE)], sem)
            copy.start()
            copy.wait()

        for e in range(N_EXPERTS):
            @pl.loop(fill_ref[e], bounds_ref[e + 1], step=MOE_TILE)
            def _(start):
                fill(start)

            unused = bounds_ref[N_EXPERTS] + e * MOE_TILE
            pl.when(unused < xs_ref.shape[0])(lambda unused=unused: fill(unused))

    if first:
        pl.when(pl.program_id(0) == 0)(zero_padding)

    pltpu.sync_copy(dest_ref.at[0], dest_smem)

    def start(j, c):
        for k in range(TOP_K):
            pltpu.make_async_copy(h_ref.at[j], xs_ref.at[dest_smem[k, j]], sem).start(priority=k)
        return c

    lax.fori_loop(0, tm, start, 0, unroll=8)
    for k in range(TOP_K):
        pltpu.make_async_copy(h_ref, xs_ref.at[pl.ds(0, tm)], sem).wait()


def _dispatch(h, dest, bounds, fill_from, rows, tm, xs_so_far):
    n = h.shape[0]
    first = xs_so_far is None
    in_specs = [pl.BlockSpec((1, TOP_K, tm), lambda i, *_: (i, 0, 0)), _row_tile_spec(tm)]
    args = [bounds, fill_from, dest, h]
    if not first:
        in_specs.append(pl.BlockSpec(memory_space=pl.ANY))
        args.append(xs_so_far)
    return pl.pallas_call(
        functools.partial(_dispatch_body, tm, first),
        grid_spec=pltpu.PrefetchScalarGridSpec(
            num_scalar_prefetch=2,
            grid=(n // tm,),
            in_specs=in_specs,
            out_specs=pl.BlockSpec(memory_space=pl.ANY),
            scratch_shapes=[pltpu.SMEM((TOP_K, tm), I32), pltpu.VMEM((MOE_TILE,) + ROW_TILE, F32),
                            pltpu.SemaphoreType.DMA],
        ),
        out_shape=jax.ShapeDtypeStruct((rows,) + ROW_TILE, F32),
        input_output_aliases={} if first else {len(args) - 1: 0},
        compiler_params=_params(1),
        name="dispatch",
    )(*args)


def _experts_body(tm, te_ref, nv_ref, x_ref, wg_ref, wu_ref, wd_ref, y_ref):
    valid = nv_ref[pl.program_id(0)]

    @pl.when(valid > 0)
    def _():
        x = _load_row_tiles(x_ref).astype(BF16)
        act = (_silu(_dot(x, wg_ref[...])) * _dot(x, wu_ref[...])).astype(BF16)
        _store_row_tiles(y_ref, _dot(act, wd_ref[...]))

    @pl.when(valid == 0)
    def _():
        y_ref[...] = jnp.zeros_like(y_ref)


def _experts(xs, tile_expert, tile_valid, mw, tm):
    rows = xs.shape[0]
    d_ff = mw["w_gate"].shape[2]
    wspec = lambda a, b: pl.BlockSpec((None, a, b), lambda i, te, nv: (te[i], 0, 0),
                                      pipeline_mode=pl.Buffered(1))
    return pl.pallas_call(
        functools.partial(_experts_body, tm),
        grid_spec=pltpu.PrefetchScalarGridSpec(
            num_scalar_prefetch=2,
            grid=(rows // tm,),
            in_specs=[_row_tile_spec(tm), wspec(D_MODEL, d_ff), wspec(D_MODEL, d_ff), wspec(d_ff, D_MODEL)],
            out_specs=_row_tile_spec(tm),
        ),
        out_shape=jax.ShapeDtypeStruct((rows,) + ROW_TILE, F32),
        compiler_params=_params(1),
        name="experts",
    )(tile_expert, tile_valid, xs, mw["w_gate"], mw["w_up"], mw["w_down"])


def _combine_body(tm, final, dest_ref, dnext_ref, x1_ref, rf_ref, ys_ref, *rest):
    if final:
        fw_ref, o_ref, buf_ref, dest_smem, sems = rest
    else:
        o_ref, buf_ref, dest_smem, sems = rest
    i = pl.program_id(0)
    nsteps = pl.num_programs(0)
    slot = i % 2

    def start_row(j, s):
        for k in range(TOP_K):
            pltpu.make_async_copy(ys_ref.at[dest_smem[k, j]], buf_ref.at[s, k, j], sems.at[s]).start(priority=k)

    def wait_rows(s):
        for k in range(TOP_K):
            pltpu.make_async_copy(ys_ref.at[pl.ds(0, tm)], buf_ref.at[s, k], sems.at[s]).wait()

    def gather(src_ref, s):
        pltpu.sync_copy(src_ref.at[0], dest_smem)
        lax.fori_loop(0, tm, lambda j, c: (start_row(j, s), c)[1], 0, unroll=8)

    @pl.when(i == 0)
    def _():
        gather(dest_ref, 0)

    wait_rows(slot)

    @pl.when(i + 1 < nsteps)
    def _():
        gather(dnext_ref, 1 - slot)

    gates = rf_ref[...]
    out = x1_ref[...] + gates[:, 0:1] * _load_row_tiles(buf_ref, slot, 0) + \
        gates[:, 1:2] * _load_row_tiles(buf_ref, slot, 1)
    if final:
        out = _rms(out, fw_ref[...])
    o_ref[...] = out


def _combine(dest, x1, rf, ys, final_w, tm):
    n = x1.shape[0]
    nt = n // tm
    row = lambda w: pl.BlockSpec((tm, w), lambda i: (i, 0))
    in_specs = [pl.BlockSpec((1, TOP_K, tm), lambda i: (i, 0, 0)),
                pl.BlockSpec((1, TOP_K, tm), lambda i: (jnp.minimum(i + 1, nt - 1), 0, 0)),
                row(D_MODEL), row(LANES), pl.BlockSpec(memory_space=pl.ANY)]
    args = [dest, dest, x1, rf, ys]
    if final_w is not None:
        in_specs.append(_const_spec((1, D_MODEL)))
        args.append(final_w)
    return pl.pallas_call(
        functools.partial(_combine_body, tm, final_w is not None),
        grid=(nt,),
        in_specs=in_specs,
        out_specs=row(D_MODEL),
        out_shape=jax.ShapeDtypeStruct((n, D_MODEL), F32),
        scratch_shapes=[pltpu.VMEM((2, TOP_K, tm) + ROW_TILE, F32), pltpu.SMEM((TOP_K, tm), I32),
                        pltpu.SemaphoreType.DMA((2,))],
        compiler_params=_params(1),
        name="combine",
    )(*args)


def _moe_layer(streams, lw, mw, final_w):
    routed = [_router_layer(x, ys, ym, lw, mw, min(TOKEN_TILE, x.shape[0])) for x, ys, ym in streams]
    stream_counts = [r[4][0, :N_EXPERTS].astype(I32) for r in routed]
    counts = functools.reduce(lambda a, b: a + b, stream_counts)
    padded = (counts + MOE_TILE - 1) // MOE_TILE * MOE_TILE
    ends = jnp.cumsum(padded)
    offsets = ends - padded
    n_total = sum(x.shape[0] for x, _, _ in streams)
    rows = (n_total * TOP_K + N_EXPERTS * (MOE_TILE - 1)) // MOE_TILE * MOE_TILE
    tile_start = jnp.arange(rows // MOE_TILE, dtype=I32) * MOE_TILE
    tile_expert = jnp.minimum(jnp.sum(tile_start[:, None] >= ends[None, :], axis=1), N_EXPERTS - 1).astype(I32)
    group_end = (offsets + counts)[tile_expert]
    tile_valid = jnp.where(tile_start < ends[-1], jnp.clip(group_end - tile_start, 0, MOE_TILE), 0).astype(I32)
    bounds = jnp.concatenate([jnp.zeros((1,), I32), ends.astype(I32)])

    def tiles(dest, t):
        return dest.reshape(TOP_K, dest.shape[1] // t, t).transpose(1, 0, 2)

    xs, dests, start = None, [], offsets
    for (x1, h, ri, rf, _), own in zip(routed, stream_counts):
        n = x1.shape[0]
        dest = jnp.stack([start[ri[0]] + ri[2], start[ri[1]] + ri[3]])
        dests.append(dest)
        td = min(DISPATCH_TILE, n)
        start = start + own
        xs = _dispatch(h, tiles(dest, td), bounds, start // MOE_TILE * MOE_TILE, rows, td, xs)
    y_sorted = _experts(xs, tile_expert, tile_valid, mw, MOE_TILE)
    outs = []
    for (x1, h, ri, rf, _), dest in zip(routed, dests):
        tc = min(COMBINE_TILE, x1.shape[0])
        outs.append(_combine(tiles(dest, tc), x1, rf, y_sorted, final_w, tc))
    return outs


def _prep_layer(layer, norm_mix, w_in, conv_w, conv_b, dt_bias, a_log, d_skip, ssd_norm, q_norm, w_q_up,
                kv_norm, w_kv_up, w_out, norm_ffn):
    w = w_in[layer]
    splits = np.cumsum([SSD_INNER, CONV_DIM, SSD_HEADS, Q_LORA, KV_LORA, ROPE_DIM])[:-1].tolist()
    w_z, w_xbc, w_dt, w_cq, w_ckv, w_kr = jnp.split(w, splits, axis=1)
    half = ROPE_DIM // 2
    w_krs = jnp.concatenate([w_kr[:, half:], w_kr[:, :half]], axis=1)
    pad = jnp.zeros((D_MODEL, LANES - SSD_HEADS), F32)
    w_cat = jnp.concatenate([w_z, w_xbc, w_cq, w_ckv, w_kr, w_krs, w_dt, pad], axis=1).astype(BF16)
    wq = w_q_up[layer].reshape(Q_LORA, MLA_HEADS, QK_DIM)
    wq = jnp.concatenate([wq, wq[:, :, NOPE_DIM + half:], wq[:, :, NOPE_DIM:NOPE_DIM + half]], axis=2)
    row = lambda v: v.reshape(1, -1).astype(F32)
    return {
        "norm_mix": row(norm_mix[layer]), "w_in": w_cat,
        "q_norm": row(q_norm[layer]), "w_q": wq.reshape(Q_LORA, MLA_HEADS * Q_HEAD_COLS).astype(BF16),
        "kv_norm": row(kv_norm[layer]), "w_kv": w_kv_up[layer].astype(BF16),
        "dt_bias": jnp.pad(row(dt_bias[layer]), ((0, 0), (0, LANES - SSD_HEADS))),
        "conv_w": conv_w[layer], "conv_b": row(conv_b[layer]), "a_log": row(a_log[layer]),
        "d_skip": row(jnp.repeat(d_skip[layer], SSD_HEADDIM)), "ssd_norm": row(ssd_norm[layer]),
        "w_out": w_out[layer].astype(BF16), "norm_ffn": row(norm_ffn[layer]),
    }


def _rope_tables(pos, tm):
    half = ROPE_DIM // 2
    inv_freq = ROPE_THETA ** (-jnp.arange(half, dtype=F32) / half)
    ang = pos.astype(F32)[:, None] * inv_freq[None, :]
    cos, sin = jnp.cos(ang), jnp.sin(ang)
    cos_tab = jnp.concatenate([cos, cos], axis=1)
    sin_tab = jnp.concatenate([-sin, sin], axis=1)
    reps = max(1, tm // pos.shape[0])
    return jnp.tile(cos_tab, (reps, 1)), jnp.tile(sin_tab, (reps, 1))


def _mixing(x, bsz, length, lw, tables, layer, conv0, ssm0, past_lat, past_kr):
    n = bsz * length
    tm = min(TOKEN_TILE, n)
    z, xbc, dt, c_kv, k_rope, q, k, v = _inproj(x, lw, tables[0], tables[1], tm)
    ys, new_ssm, new_conv = _ssd(xbc, dt, z, lw, bsz, length, layer, conv0, ssm0)
    if past_lat is None:
        ym = _attention_prompt(q, k, v, bsz, length)
    else:
        past = past_lat.shape[2]
        ym = _attention_cache(q, k, v, past_lat.reshape(-1, KV_LORA), past_kr.reshape(-1, ROPE_DIM),
                              lw["w_kv"], bsz, length, past, layer)
    return ys, ym, c_kv, k_rope, new_ssm, new_conv


def kernel(x_prompt, x_sample, cache_kv_latent, cache_k_rope, state_ssm, state_conv, norm_mix, w_in, conv_w,
           conv_b, dt_bias, a_log, d_skip, ssd_norm, q_norm, w_q_up, kv_norm, w_kv_up, w_out, norm_ffn,
           ffn_w_gate, ffn_w_up, ffn_w_down, moe_router, moe_w_gate, moe_w_up, moe_w_down, final_norm):
    bp, sp = x_prompt.shape[:2]
    bs, ss = x_sample.shape[:2]
    depth = w_in.shape[0]
    past = cache_kv_latent.shape[2]
    tabs_p = _rope_tables(jnp.arange(sp, dtype=I32), min(TOKEN_TILE, bp * sp))
    tabs_s = _rope_tables(past + jnp.arange(ss, dtype=I32), min(TOKEN_TILE, bs * ss))
    final_w = final_norm.reshape(1, D_MODEL)
    hp = x_prompt.reshape(bp * sp, D_MODEL)
    hs = x_sample.reshape(bs * ss, D_MODEL)
    outs_p, outs_s = [], []
    for layer in range(depth):
        lw = _prep_layer(layer, norm_mix, w_in, conv_w, conv_b, dt_bias, a_log, d_skip, ssd_norm, q_norm,
                         w_q_up, kv_norm, w_kv_up, w_out, norm_ffn)
        last = final_w if layer == depth - 1 else None
        mix_p = _mixing(hp, bp, sp, lw, tabs_p, layer, None, None, None, None)
        mix_s = _mixing(hs, bs, ss, lw, tabs_s, layer, state_conv, state_ssm, cache_kv_latent, cache_k_rope)
        outs_p.append(mix_p[2:])
        outs_s.append(mix_s[2:])
        idx = layer // 2
        if layer % 2 == 0:
            fw = {"w_gate": ffn_w_gate[idx].astype(BF16), "w_up": ffn_w_up[idx].astype(BF16),
                  "w_down": ffn_w_down[idx].astype(BF16)}
            hp = _dense_layer(hp, mix_p[0], mix_p[1], lw, fw, last, min(TOKEN_TILE, bp * sp))
            hs = _dense_layer(hs, mix_s[0], mix_s[1], lw, fw, last, min(TOKEN_TILE, bs * ss))
        else:
            wr_hi = moe_router[idx].astype(BF16)
            wr_lo = (moe_router[idx] - wr_hi.astype(F32)).astype(BF16)
            mw = {"w_router": jnp.pad(jnp.concatenate([wr_hi, wr_lo], axis=1),
                                      ((0, 0), (0, LANES - 2 * N_EXPERTS))),
                  "w_gate": moe_w_gate[idx].astype(BF16), "w_up": moe_w_up[idx].astype(BF16),
                  "w_down": moe_w_down[idx].astype(BF16)}
            hp, hs = _moe_layer([(hp, mix_p[0], mix_p[1]), (hs, mix_s[0], mix_s[1])], lw, mw, last)

    def stack(outs, j, shape):
        return jnp.stack([o[j].reshape(shape) for o in outs])

    return (hp.reshape(bp, sp, D_MODEL), hs.reshape(bs, ss, D_MODEL),
            stack(outs_p, 0, (bp, sp, KV_LORA)), stack(outs_p, 1, (bp, sp, ROPE_DIM)),
            stack(outs_p, 2, (bp, SSD_HEADS, SSD_HEADDIM, SSD_STATE)),
            stack(outs_p, 3, (bp, CONV_WIDTH - 1, CONV_DIM)),
            stack(outs_s, 0, (bs, ss, KV_LORA)), stack(outs_s, 1, (bs, ss, ROPE_DIM)),
            stack(outs_s, 2, (bs, SSD_HEADS, SSD_HEADDIM, SSD_STATE)),
            stack(outs_s, 3, (bs, CONV_WIDTH - 1, CONV_DIM)))
```

```python
import functools
import math

import jax
import jax.numpy as jnp
import numpy as np
from jax import lax
from jax.experimental import pallas as pl
from jax.experimental.pallas import tpu as pltpu

F32 = jnp.float32
BF16 = jnp.bfloat16
I32 = jnp.int32

D_MODEL = 1024
CHUNK = 64
LOG2_CHUNK = 6
SSD_HEADS = 16
SSD_HEADDIM = 64
SSD_INNER = SSD_HEADS * SSD_HEADDIM
SSD_GROUPS = 2
HEADS_PER_GROUP = SSD_HEADS // SSD_GROUPS
SSD_STATE = 128
CONV_WIDTH = 4
CONV_DIM = SSD_INNER + 2 * SSD_GROUPS * SSD_STATE
MLA_HEADS = 8
Q_LORA = 384
KV_LORA = 256
NOPE_DIM = 128
ROPE_DIM = 64
V_DIM = 128
QK_DIM = NOPE_DIM + ROPE_DIM
ROPE_THETA = 10000.0
MLA_INNER = MLA_HEADS * V_DIM
N_EXPERTS = 8
TOP_K = 2
EPS = 1e-6

LANES = 128
SUBLANES = 8
MXU_DIM = 256
VMEM_LIMIT = 56 * 1024 * 1024

OFF_Z = 0
OFF_XBC = OFF_Z + SSD_INNER
OFF_CQ = OFF_XBC + CONV_DIM
OFF_CKV = OFF_CQ + Q_LORA
OFF_KR = OFF_CKV + KV_LORA
OFF_KRS = OFF_KR + ROPE_DIM
OFF_DT = OFF_KRS + ROPE_DIM
IN_COLS = OFF_DT + LANES
Q_HEAD_COLS = NOPE_DIM + 2 * ROPE_DIM

TOKEN_TILE = 512
ATT_TILE = 512
ATT_HEADS_PER_STEP = 4
SSD_CHUNKS_PER_STEP = 8
MOE_TILE = 256
DISPATCH_TILE = 2048
COMBINE_TILE = 1024


def _params(n_axes):
    return pltpu.CompilerParams(dimension_semantics=("arbitrary",) * n_axes,
                                vmem_limit_bytes=VMEM_LIMIT)


def _const_spec(shape):
    zeros = (0,) * len(shape)
    return pl.BlockSpec(shape, lambda *_: zeros)


def _rms(x, w):
    return x * lax.rsqrt(jnp.mean(x * x, axis=-1, keepdims=True) + EPS) * w


def _silu(x):
    return x * (1.0 / (1.0 + jnp.exp(-x)))


def _dot(a, b):
    return jnp.dot(a, b, preferred_element_type=F32)


def _dot_nt(a, b):
    return lax.dot_general(a, b, (((1,), (1,)), ((), ())), preferred_element_type=F32)


def _dot_tn(a, b):
    return lax.dot_general(a, b, (((0,), (0,)), ((), ())), preferred_element_type=F32)


def _split3(x):
    hi = x.astype(BF16)
    rest = x - hi.astype(F32)
    mid = rest.astype(BF16)
    lo = (rest - mid.astype(F32)).astype(BF16)
    return jnp.concatenate([hi, mid, lo], axis=1)


ROW_TILE = (D_MODEL // LANES, LANES)


def _row_tile_spec(tm):
    return pl.BlockSpec((tm,) + ROW_TILE, lambda i, *_: (i, 0, 0))


def _store_row_tiles(ref, x):
    for s in range(ROW_TILE[0]):
        ref[:, s, :] = x[:, s * LANES:(s + 1) * LANES]


def _load_row_tiles(ref, *lead):
    return jnp.concatenate([ref[lead + (slice(None), s, slice(None))] for s in range(ROW_TILE[0])], axis=1)


def _inproj_body(x_ref, nw_ref, win_ref, qnw_ref, wq_ref, kvnw_ref, wkv_ref, cos_ref, sin_ref, dtb_ref,
                 z_ref, xbc_ref, dt_ref, ckv_ref, kr_ref, q_ref, k_ref, v_ref):
    h = _rms(x_ref[...], nw_ref[...]).astype(BF16)
    proj = _dot(h, win_ref[...])
    z_ref[...] = proj[:, OFF_Z:OFF_XBC]
    xbc_ref[...] = proj[:, OFF_XBC:OFF_CQ]
    pre = proj[:, OFF_DT:IN_COLS] + dtb_ref[...]
    dt_ref[...] = jnp.maximum(pre, 0.0) + jnp.log(1.0 + jnp.exp(-jnp.abs(pre)))
    cos = cos_ref[...]
    sin = sin_ref[...]
    k_rope = proj[:, OFF_KR:OFF_KRS] * cos + proj[:, OFF_KRS:OFF_DT] * sin
    kr_ref[...] = k_rope
    c_kv = _rms(proj[:, OFF_CKV:OFF_KR], kvnw_ref[...])
    ckv_ref[...] = c_kv
    qn = _rms(proj[:, OFF_CQ:OFF_CKV], qnw_ref[...]).astype(BF16)
    q_all = _dot(qn, wq_ref[...])
    kv_all = _dot(c_kv.astype(BF16), wkv_ref[...])
    k_rope_b = k_rope.astype(BF16)
    for hd in range(MLA_HEADS):
        qb = q_all[:, hd * Q_HEAD_COLS:(hd + 1) * Q_HEAD_COLS]
        q_rope = qb[:, NOPE_DIM:QK_DIM] * cos + qb[:, QK_DIM:Q_HEAD_COLS] * sin
        q_ref[hd] = jnp.concatenate([qb[:, :NOPE_DIM], q_rope], axis=1).astype(BF16)
        kvb = kv_all[:, hd * (NOPE_DIM + V_DIM):(hd + 1) * (NOPE_DIM + V_DIM)]
        k_ref[hd] = jnp.concatenate([kvb[:, :NOPE_DIM].astype(BF16), k_rope_b], axis=1)
        v_ref[hd] = kvb[:, NOPE_DIM:].astype(BF16)


def _inproj(x, lw, cos_tab, sin_tab, tm):
    n = x.shape[0]
    nt = n // tm
    tab_blocks = cos_tab.shape[0] // tm
    row = lambda w: pl.BlockSpec((tm, w), lambda i: (i, 0))
    head = lambda w: pl.BlockSpec((MLA_HEADS, tm, w), lambda i: (0, i, 0))
    tab = pl.BlockSpec((tm, ROPE_DIM), lambda i: (i % tab_blocks, 0))
    return pl.pallas_call(
        _inproj_body,
        grid=(nt,),
        in_specs=[row(D_MODEL), _const_spec((1, D_MODEL)), _const_spec((D_MODEL, IN_COLS)),
                  _const_spec((1, Q_LORA)), _const_spec((Q_LORA, MLA_HEADS * Q_HEAD_COLS)),
                  _const_spec((1, KV_LORA)), _const_spec((KV_LORA, MLA_HEADS * (NOPE_DIM + V_DIM))),
                  tab, tab, _const_spec((1, LANES))],
        out_specs=[row(SSD_INNER), row(CONV_DIM), row(LANES), row(KV_LORA), row(ROPE_DIM),
                   head(QK_DIM), head(QK_DIM), head(V_DIM)],
        out_shape=[jax.ShapeDtypeStruct((n, SSD_INNER), F32), jax.ShapeDtypeStruct((n, CONV_DIM), F32),
                   jax.ShapeDtypeStruct((n, LANES), F32), jax.ShapeDtypeStruct((n, KV_LORA), F32),
                   jax.ShapeDtypeStruct((n, ROPE_DIM), F32),
                   jax.ShapeDtypeStruct((MLA_HEADS, n, QK_DIM), BF16),
                   jax.ShapeDtypeStruct((MLA_HEADS, n, QK_DIM), BF16),
                   jax.ShapeDtypeStruct((MLA_HEADS, n, V_DIM), BF16)],
        compiler_params=_params(1),
        name="inproj",
    )(x, lw["norm_mix"], lw["w_in"], lw["q_norm"], lw["w_q"], lw["kv_norm"], lw["w_kv"],
      cos_tab, sin_tab, lw["dt_bias"])


def _ssd_body(t, cps, has_init, *refs):
    if has_init:
        (xbc_ref, dt_ref, z_ref, cw_ref, cb_ref, alog_ref, dskip_ref, nw_ref, conv0_ref, ssm0_ref,
         y_ref, ssm_ref, conv_ref, hist_ref) = refs
    else:
        (xbc_ref, dt_ref, z_ref, cw_ref, cb_ref, alog_ref, dskip_ref, nw_ref,
         y_ref, ssm_ref, conv_ref, hist_ref) = refs
    rows = cps * t
    hist_rows = SUBLANES
    first = hist_rows - (CONV_WIDTH - 1)
    group_lanes = HEADS_PER_GROUP * t
    group_cols = HEADS_PER_GROUP * SSD_HEADDIM
    heads_per_block = MXU_DIM // t
    blocks_per_group = group_lanes // MXU_DIM
    log2_t = int(math.log2(t))
    log2_p = int(math.log2(SSD_HEADDIM))

    @pl.when(pl.program_id(1) == 0)
    def _():
        hist_ref[...] = jnp.zeros_like(hist_ref)
        if has_init:
            hist_ref[first:hist_rows, :] = conv0_ref[...]
            ssm_ref[...] = ssm0_ref[...]
        else:
            ssm_ref[...] = jnp.zeros_like(ssm_ref)

    x_raw = xbc_ref[...]
    hist_ref[hist_rows:, :] = x_raw
    padded = hist_ref[...]
    conv = cb_ref[...]
    for k in range(CONV_WIDTH):
        back = CONV_WIDTH - 1 - k
        tap = x_raw if back == 0 else pltpu.roll(padded, back, axis=0)[hist_rows:, :]
        conv = conv + tap * cw_ref[k:k + 1, :]
    hist_ref[first:hist_rows, :] = x_raw[rows - (CONV_WIDTH - 1):rows]
    conv_ref[...] = x_raw[rows - (CONV_WIDTH - 1):rows]
    act = _silu(conv)
    xs = act[:, :SSD_INNER]
    b_mat = act[:, SSD_INNER:SSD_INNER + SSD_GROUPS * SSD_STATE].astype(BF16)
    c_mat = act[:, SSD_INNER + SSD_GROUPS * SSD_STATE:].astype(BF16)

    dt = dt_ref[:, :SSD_HEADS]
    da = dt * (-jnp.exp(alog_ref[...]))
    ri = lax.broadcasted_iota(I32, (rows, rows), 0)
    ci = lax.broadcasted_iota(I32, (rows, rows), 1)
    tril = ((ri >= ci) & ((ri >> log2_t) == (ci >> log2_t))).astype(BF16)
    cs3 = _dot(tril, _split3(da))
    cs = cs3[:, :SSD_HEADS] + cs3[:, SSD_HEADS:2 * SSD_HEADS] + cs3[:, 2 * SSD_HEADS:]
    cs = cs * math.log2(math.e)

    def expander(width, shift):
        e_rows = lax.broadcasted_iota(I32, (3 * SSD_HEADS, width), 0)
        return ((lax.broadcasted_iota(I32, (3 * SSD_HEADS, width), 1) >> shift) == (e_rows & (SSD_HEADS - 1))
                ).astype(BF16)

    per_p = _dot(_split3(jnp.concatenate([cs, dt], axis=0)), expander(SSD_INNER, log2_p))
    cs_p, dt_p = per_p[:rows], per_p[rows:]
    cs_t = cs_p if t == SSD_HEADDIM else _dot(_split3(cs), expander(SSD_HEADS * t, log2_t))
    xdt = xs * dt_p
    lane = lax.broadcasted_iota(I32, (t, SSD_HEADS * t), 1) & (t - 1)
    rowi = lax.broadcasted_iota(I32, (t, SSD_HEADS * t), 0)
    block_mask = (lax.broadcasted_iota(I32, (MXU_DIM, heads_per_block * SSD_HEADDIM), 0) >> log2_t) == \
        (lax.broadcasted_iota(I32, (MXU_DIM, heads_per_block * SSD_HEADDIM), 1) >> log2_p)

    y_chunks = []
    for ch in range(cps):
        r0, r1 = ch * t, (ch + 1) * t
        cs_t_c, cs_p_c = cs_t[r0:r1], cs_p[r0:r1]
        cs_src = jnp.sum(jnp.where(lane == rowi, cs_t_c, 0.0), axis=0, keepdims=True)
        decay = jnp.exp2(jnp.where(rowi >= lane, cs_t_c - cs_src, -jnp.inf))
        exp_cs = jnp.exp2(cs_p_c)
        x_end = (xdt[r0:r1] * jnp.exp2(cs_p_c[t - 1:t, :] - cs_p_c)).astype(BF16)
        chunk_decay = jnp.exp2(cs[r1 - 1:r1, :])
        y_blocks = []
        for g in range(SSD_GROUPS):
            bg = b_mat[r0:r1, g * SSD_STATE:(g + 1) * SSD_STATE]
            cg = c_mat[r0:r1, g * SSD_STATE:(g + 1) * SSD_STATE]
            cb = _dot_nt(cg, jnp.concatenate([bg] * HEADS_PER_GROUP, axis=0))
            m = (cb * decay[:, g * group_lanes:(g + 1) * group_lanes]).astype(BF16)
            h0, h1 = g * HEADS_PER_GROUP, (g + 1) * HEADS_PER_GROUP
            state = ssm_ref[h0:h1].reshape(group_cols, SSD_STATE)
            y_off = _dot_nt(cg, state.astype(BF16)) * exp_cs[:, g * group_cols:(g + 1) * group_cols]
            for kb in range(blocks_per_group):
                c0 = g * group_cols + kb * heads_per_block * SSD_HEADDIM
                c1 = c0 + heads_per_block * SSD_HEADDIM
                xb = xdt[r0:r1, c0:c1]
                bd = jnp.where(block_mask, jnp.concatenate([xb] * heads_per_block, axis=0), 0.0).astype(BF16)
                y_diag = _dot(m[:, kb * MXU_DIM:(kb + 1) * MXU_DIM], bd)
                y_blocks.append(y_diag + y_off[:, c0 - g * group_cols:c1 - g * group_cols])
            local = _dot_tn(x_end[:, g * group_cols:(g + 1) * group_cols], bg)
            for r in range(HEADS_PER_GROUP):
                hd = h0 + r
                ssm_ref[hd] = ssm_ref[hd] * chunk_decay[:, hd:hd + 1] + \
                    local[r * SSD_HEADDIM:(r + 1) * SSD_HEADDIM]
        y_chunks.append(jnp.concatenate(y_blocks, axis=1))
    y = jnp.concatenate(y_chunks, axis=0) + xs * dskip_ref[...]
    y_ref[...] = _rms(y * _silu(z_ref[...]), nw_ref[...]).astype(BF16)


def _ssd(xbc, dt, z, lw, bsz, length, layer, conv0, ssm0):
    t = CHUNK if length % CHUNK == 0 else length
    nc = length // t
    cps = max(c for c in (1, 2, SSD_CHUNKS_PER_STEP) if nc % c == 0)
    steps = nc // cps
    n = bsz * length
    has_init = conv0 is not None
    row = lambda w: pl.BlockSpec((cps * t, w), lambda b, c: (b * steps + c, 0))
    conv_spec = pl.BlockSpec((None, CONV_WIDTH - 1, CONV_DIM), lambda b, c: (b, 0, 0))
    ssm_spec = pl.BlockSpec((None, SSD_HEADS, SSD_HEADDIM, SSD_STATE), lambda b, c: (b, 0, 0, 0))
    in_specs = [row(CONV_DIM), row(LANES), row(SSD_INNER), _const_spec((CONV_WIDTH, CONV_DIM)),
                _const_spec((1, CONV_DIM)), _const_spec((1, SSD_HEADS)), _const_spec((1, SSD_INNER)),
                _const_spec((1, SSD_INNER))]
    args = [xbc, dt, z, lw["conv_w"], lw["conv_b"], lw["a_log"], lw["d_skip"], lw["ssd_norm"]]
    if has_init:
        in_specs += [pl.BlockSpec((None, None, CONV_WIDTH - 1, CONV_DIM), lambda b, c: (layer, b, 0, 0)),
                     pl.BlockSpec((None, None, SSD_HEADS, SSD_HEADDIM, SSD_STATE),
                                  lambda b, c: (layer, b, 0, 0, 0))]
        args += [conv0, ssm0]
    return pl.pallas_call(
        functools.partial(_ssd_body, t, cps, has_init),
        grid=(bsz, steps),
        in_specs=in_specs,
        out_specs=[row(SSD_INNER), ssm_spec, conv_spec],
        out_shape=[jax.ShapeDtypeStruct((n, SSD_INNER), BF16),
                   jax.ShapeDtypeStruct((bsz, SSD_HEADS, SSD_HEADDIM, SSD_STATE), F32),
                   jax.ShapeDtypeStruct((bsz, CONV_WIDTH - 1, CONV_DIM), F32)],
        scratch_shapes=[pltpu.VMEM((SUBLANES + cps * t, CONV_DIM), F32)],
        compiler_params=_params(2),
        name="ssd",
    )(*args)


def _lane_groups(x):
    return [x[:, g * LANES:(g + 1) * LANES] for g in range(x.shape[1] // LANES)]


def _attn_body(tq, nh, q_ref, k_ref, v_ref, o_ref, s_ref, m_ref, l_ref, acc_ref):
    qi = pl.program_id(2)
    heads = range(nh)
    q = [q_ref[h] for h in heads]
    c = (QK_DIM ** -0.5) * math.log2(math.e)

    def keys(ref, h, j):
        return ref[h, pl.ds(pl.multiple_of(j * tq, tq), tq), :]

    def lane_max(m, s):
        for part in _lane_groups(s):
            m = jnp.maximum(m, part)
        return m

    m_ref[...] = jnp.full_like(m_ref, -jnp.inf)

    @pl.loop(0, qi)
    def _(j):
        for h in heads:
            s = _dot_nt(q[h], keys(k_ref, h, j)) * c
            s_ref[h, j] = s
            m_ref[h] = lane_max(m_ref[h], s)

    rows = lax.broadcasted_iota(I32, (tq, tq), 0) >> LOG2_CHUNK
    cols = lax.broadcasted_iota(I32, (tq, tq), 1) >> LOG2_CHUNK
    for h in heads:
        s = jnp.where(cols <= rows, _dot_nt(q[h], keys(k_ref, h, qi)) * c, -jnp.inf)
        s_ref[h, qi] = s
        row_max = jnp.max(lane_max(m_ref[h], s), axis=-1, keepdims=True)
        m_ref[h] = jnp.broadcast_to(row_max, (tq, LANES))
    l_ref[...] = jnp.zeros_like(l_ref)
    acc_ref[...] = jnp.zeros_like(acc_ref)

    @pl.loop(0, qi + 1)
    def _(j):
        for h in heads:
            mc = m_ref[h]
            parts = [jnp.exp2(part - mc) for part in _lane_groups(s_ref[h, j])]
            l_ref[h] = l_ref[h] + functools.reduce(lambda a, b: a + b, parts)
            p = jnp.concatenate(parts, axis=1).astype(BF16)
            acc_ref[h] = acc_ref[h] + _dot(p, keys(v_ref, h, j))

    o_ref[...] = jnp.concatenate(
        [acc_ref[h] / jnp.sum(l_ref[h], axis=-1, keepdims=True) for h in heads], axis=1).astype(BF16)


def _attention_prompt(q, k, v, bsz, length):
    tq = min(ATT_TILE, length)
    nh = ATT_HEADS_PER_STEP
    nq = length // tq
    n = bsz * length
    return pl.pallas_call(
        functools.partial(_attn_body, tq, nh),
        grid=(bsz, MLA_HEADS // nh, nq),
        in_specs=[pl.BlockSpec((nh, tq, QK_DIM), lambda b, h, i: (h, b * nq + i, 0)),
                  pl.BlockSpec((nh, length, QK_DIM), lambda b, h, i: (h, b, 0), pipeline_mode=pl.Buffered(1)),
                  pl.BlockSpec((nh, length, V_DIM), lambda b, h, i: (h, b, 0), pipeline_mode=pl.Buffered(1))],
        out_specs=pl.BlockSpec((tq, nh * V_DIM), lambda b, h, i: (b * nq + i, h)),
        out_shape=jax.ShapeDtypeStruct((n, MLA_INNER), BF16),
        scratch_shapes=[pltpu.VMEM((nh, nq, tq, tq), F32), pltpu.VMEM((nh, tq, LANES), F32),
                        pltpu.VMEM((nh, tq, LANES), F32), pltpu.VMEM((nh, tq, V_DIM), F32)],
        compiler_params=_params(3),
        name="attn_prompt",
    )(q, k, v)


def _attn_cache_body(past, length, q_ref, kn_ref, vn_ref, lat_ref, kr_ref, wkv_ref, o_ref):
    scale = QK_DIM ** -0.5
    q_chunk = (past + np.arange(length)) // CHUNK
    past_visible = (np.arange(past) // CHUNK)[None, :] <= q_chunk[:, None]
    new_visible = q_chunk[None, :] <= q_chunk[:, None]
    kv_past = _dot(lat_ref[...].astype(BF16), wkv_ref[...])
    kr_past = kr_ref[...].astype(BF16)
    outs = []
    for hd in range(MLA_HEADS):
        q = q_ref[hd]
        kvb = kv_past[:, hd * (NOPE_DIM + V_DIM):(hd + 1) * (NOPE_DIM + V_DIM)]
        k_past = jnp.concatenate([kvb[:, :NOPE_DIM].astype(BF16), kr_past], axis=1)
        v_past = kvb[:, NOPE_DIM:].astype(BF16)
        s_p = _dot_nt(q, k_past) * scale
        s_n = _dot_nt(q, kn_ref[hd]) * scale
        if not past_visible.all():
            rows = (past + lax.broadcasted_iota(I32, (length, past), 0)) >> LOG2_CHUNK
            cols = lax.broadcasted_iota(I32, (length, past), 1) >> LOG2_CHUNK
            s_p = jnp.where(cols <= rows, s_p, -jnp.inf)
        if not new_visible.all():
            rows = (past + lax.broadcasted_iota(I32, (length, length), 0)) >> LOG2_CHUNK
            cols = (past + lax.broadcasted_iota(I32, (length, length), 1)) >> LOG2_CHUNK
            s_n = jnp.where(cols <= rows, s_n, -jnp.inf)
        m = jnp.maximum(jnp.max(s_p, axis=-1, keepdims=True), jnp.max(s_n, axis=-1, keepdims=True))
        p_p = jnp.exp(s_p - m)
        p_n = jnp.exp(s_n - m)
        denom = jnp.sum(p_p, axis=-1, keepdims=True) + jnp.sum(p_n, axis=-1, keepdims=True)
        o = _dot(p_p.astype(BF16), v_past) + _dot(p_n.astype(BF16), vn_ref[hd])
        outs.append(o / denom)
    o_ref[...] = jnp.concatenate(outs, axis=1).astype(BF16)


def _attention_cache(q, k_new, v_new, lat_past, kr_past, w_kv, bsz, length, past, layer):
    n = bsz * length
    new = lambda w: pl.BlockSpec((MLA_HEADS, length, w), lambda b: (0, b, 0))
    old = lambda w: pl.BlockSpec((past, w), lambda b: (layer * bsz + b, 0))
    return pl.pallas_call(
        functools.partial(_attn_cache_body, past, length),
        grid=(bsz,),
        in_specs=[new(QK_DIM), new(QK_DIM), new(V_DIM), old(KV_LORA), old(ROPE_DIM), _const_spec(w_kv.shape)],
        out_specs=pl.BlockSpec((length, MLA_INNER), lambda b: (b, 0)),
        out_shape=jax.ShapeDtypeStruct((n, MLA_INNER), BF16),
        compiler_params=_params(1),
        name="attn_cache",
    )(q, k_new, v_new, lat_past, kr_past, w_kv)


def _mix_out(x_ref, ys_ref, ym_ref, wo_ref):
    return x_ref[...] + _dot(ys_ref[...], wo_ref[:SSD_INNER, :]) + _dot(ym_ref[...], wo_ref[SSD_INNER:, :])


def _dense_body(final, x_ref, ys_ref, ym_ref, wo_ref, nw_ref, wg_ref, wu_ref, wd_ref, *rest):
    x1 = _mix_out(x_ref, ys_ref, ym_ref, wo_ref)
    h = _rms(x1, nw_ref[...]).astype(BF16)
    act = (_silu(_dot(h, wg_ref[...])) * _dot(h, wu_ref[...])).astype(BF16)
    x2 = x1 + _dot(act, wd_ref[...])
    if final:
        fw_ref, o_ref = rest
        o_ref[...] = _rms(x2, fw_ref[...])
    else:
        (o_ref,) = rest
        o_ref[...] = x2


def _resident(shape):
    zeros = (0,) * len(shape)
    return pl.BlockSpec(shape, lambda *_: zeros, pipeline_mode=pl.Buffered(1))


def _dense_layer(x, ys, ym, lw, fw, final_w, tm):
    n = x.shape[0]
    row = lambda w: pl.BlockSpec((tm, w), lambda i: (i, 0))
    d_ff = fw["w_gate"].shape[1]
    in_specs = [row(D_MODEL), row(SSD_INNER), row(MLA_INNER), _resident((SSD_INNER + MLA_INNER, D_MODEL)),
                _const_spec((1, D_MODEL)), _resident((D_MODEL, d_ff)), _resident((D_MODEL, d_ff)),
                _resident((d_ff, D_MODEL))]
    args = [x, ys, ym, lw["w_out"], lw["norm_ffn"], fw["w_gate"], fw["w_up"], fw["w_down"]]
    if final_w is not None:
        in_specs.append(_const_spec((1, D_MODEL)))
        args.append(final_w)
    return pl.pallas_call(
        functools.partial(_dense_body, final_w is not None),
        grid=(n // tm,),
        in_specs=in_specs,
        out_specs=row(D_MODEL),
        out_shape=jax.ShapeDtypeStruct((n, D_MODEL), F32),
        compiler_params=_params(1),
        name="mixout_dense",
    )(*args)


def _router_body(tm, x_ref, ys_ref, ym_ref, wo_ref, nw_ref, wr_ref,
                 x1_ref, h_ref, ri_ref, rf_ref, cnt_ref):
    @pl.when(pl.program_id(0) == 0)
    def _():
        cnt_ref[...] = jnp.zeros_like(cnt_ref)

    x1 = _mix_out(x_ref, ys_ref, ym_ref, wo_ref)
    x1_ref[...] = x1
    h = _rms(x1, nw_ref[...])
    _store_row_tiles(h_ref, h)
    lane_i = lax.broadcasted_iota(I32, (tm, LANES), 1)
    lane = lane_i.astype(F32)
    h_hi = h.astype(BF16)
    h_lo = (h - h_hi.astype(F32)).astype(BF16)
    hi_terms = _dot(h_hi, wr_ref[...])
    logits = hi_terms + pltpu.roll(hi_terms, LANES - N_EXPERTS, axis=1) + _dot(h_lo, wr_ref[...])
    logits = jnp.where(lane_i < N_EXPERTS, logits, -jnp.inf)
    top0 = jnp.max(logits, axis=-1, keepdims=True)
    e0 = jnp.min(jnp.where(logits == top0, lane, float(LANES)), axis=-1, keepdims=True)
    rest = jnp.where(lane == e0, -jnp.inf, logits)
    top1 = jnp.max(rest, axis=-1, keepdims=True)
    e1 = jnp.min(jnp.where(rest == top1, lane, float(LANES)), axis=-1, keepdims=True)
    w1 = jnp.exp(top1 - top0)
    g0 = 1.0 / (1.0 + w1)
    g1 = w1 / (1.0 + w1)
    onehot = jnp.where((lane == e0) | (lane == e1), 1.0, 0.0)
    strict = (lax.broadcasted_iota(I32, (tm, tm), 0) > lax.broadcasted_iota(I32, (tm, tm), 1)).astype(BF16)
    before = _dot(strict, onehot.astype(BF16)) + cnt_ref[0:1, :]
    pos0 = jnp.sum(jnp.where(lane == e0, before, 0.0), axis=-1, keepdims=True)
    pos1 = jnp.sum(jnp.where(lane == e1, before, 0.0), axis=-1, keepdims=True)
    cnt_ref[...] = cnt_ref[...] + jnp.sum(onehot, axis=0, keepdims=True)
    packed = jnp.where(lane_i == 0, e0, jnp.where(lane_i == 1, e1, jnp.where(lane_i == 2, pos0, pos1)))
    ri_ref[...] = packed.T[:SUBLANES, :].astype(I32)
    rf_ref[...] = jnp.where(lane_i == 0, g0, g1)


def _router_layer(x, ys, ym, lw, mw, tm):
    n = x.shape[0]
    row = lambda w: pl.BlockSpec((tm, w), lambda i: (i, 0))
    return pl.pallas_call(
        functools.partial(_router_body, tm),
        grid=(n // tm,),
        in_specs=[row(D_MODEL), row(SSD_INNER), row(MLA_INNER), _const_spec((SSD_INNER + MLA_INNER, D_MODEL)),
                  _const_spec((1, D_MODEL)), _const_spec((D_MODEL, LANES))],
        out_specs=[row(D_MODEL), _row_tile_spec(tm), pl.BlockSpec((SUBLANES, tm), lambda i: (0, i)), row(LANES),
                   _const_spec((SUBLANES, LANES))],
        out_shape=[jax.ShapeDtypeStruct((n, D_MODEL), F32), jax.ShapeDtypeStruct((n,) + ROW_TILE, F32),
                   jax.ShapeDtypeStruct((SUBLANES, n), I32), jax.ShapeDtypeStruct((n, LANES), F32),
                   jax.ShapeDtypeStruct((SUBLANES, LANES), F32)],
        compiler_params=_params(1),
        name="mixout_router",
    )(x, ys, ym, lw["w_out"], lw["norm_ffn"], mw["w_router"])


def _dispatch_body(tm, first, bounds_ref, fill_ref, dest_ref, h_ref, *rest):
    xs_ref, dest_smem, zero_ref, sem = rest[-4:]

    def zero_padding():
        zero_ref[...] = jnp.zeros_like(zero_ref)

        def fill(start):
            copy = pltpu.make_async_copy(zero_ref, xs_ref.at[pl.ds(pl.multiple_of(start, MOE_TILE), MOE_TILE)], sem)
            copy.start()
            copy.wait()

        for e in range(N_EXPERTS):
            @pl.loop(fill_ref[e], bounds_ref[e + 1], step=MOE_TILE)
            def _(start):
                fill(start)

            unused = bounds_ref[N_EXPERTS] + e * MOE_TILE
            pl.when(unused < xs_ref.shape[0])(lambda unused=unused: fill(unused))

    if first:
        pl.when(pl.program_id(0) == 0)(zero_padding)

    pltpu.sync_copy(dest_ref.at[0], dest_smem)

    def start(j, c):
        for k in range(TOP_K):
            pltpu.make_async_copy(h_ref.at[j], xs_ref.at[dest_smem[k, j]], sem).start(priority=k)
        return c

    lax.fori_loop(0, tm, start, 0, unroll=8)
    for k in range(TOP_K):
        pltpu.make_async_copy(h_ref, xs_ref.at[pl.ds(0, tm)], sem).wait()


def _dispatch(h, dest, bounds, fill_from, rows, tm, xs_so_far):
    n = h.shape[0]
    first = xs_so_far is None
    in_specs = [pl.BlockSpec((1, TOP_K, tm), lambda i, *_: (i, 0, 0)), _row_tile_spec(tm)]
    args = [bounds, fill_from, dest, h]
    if not first:
        in_specs.append(pl.BlockSpec(memory_space=pl.ANY))
        args.append(xs_so_far)
    return pl.pallas_call(
        functools.partial(_dispatch_body, tm, first),
        grid_spec=pltpu.PrefetchScalarGridSpec(
            num_scalar_prefetch=2,
            grid=(n // tm,),
            in_specs=in_specs,
            out_specs=pl.BlockSpec(memory_space=pl.ANY),
            scratch_shapes=[pltpu.SMEM((TOP_K, tm), I32), pltpu.VMEM((MOE_TILE,) + ROW_TILE, F32),
                            pltpu.SemaphoreType.DMA],
        ),
        out_shape=jax.ShapeDtypeStruct((rows,) + ROW_TILE, F32),
        input_output_aliases={} if first else {len(args) - 1: 0},
        compiler_params=_params(1),
        name="dispatch",
    )(*args)


def _experts_body(tm, te_ref, nv_ref, x_ref, wg_ref, wu_ref, wd_ref, y_ref):
    valid = nv_ref[pl.program_id(0)]

    @pl.when(valid > 0)
    def _():
        x = _load_row_tiles(x_ref).astype(BF16)
        act = (_silu(_dot(x, wg_ref[...])) * _dot(x, wu_ref[...])).astype(BF16)
        _store_row_tiles(y_ref, _dot(act, wd_ref[...]))

    @pl.when(valid == 0)
    def _():
        y_ref[...] = jnp.zeros_like(y_ref)


def _experts(xs, tile_expert, tile_valid, mw, tm):
    rows = xs.shape[0]
    d_ff = mw["w_gate"].shape[2]
    wspec = lambda a, b: pl.BlockSpec((None, a, b), lambda i, te, nv: (te[i], 0, 0),
                                      pipeline_mode=pl.Buffered(1))
    return pl.pallas_call(
        functools.partial(_experts_body, tm),
        grid_spec=pltpu.PrefetchScalarGridSpec(
            num_scalar_prefetch=2,
            grid=(rows // tm,),
            in_specs=[_row_tile_spec(tm), wspec(D_MODEL, d_ff), wspec(D_MODEL, d_ff), wspec(d_ff, D_MODEL)],
            out_specs=_row_tile_spec(tm),
        ),
        out_shape=jax.ShapeDtypeStruct((rows,) + ROW_TILE, F32),
        compiler_params=_params(1),
        name="experts",
    )(tile_expert, tile_valid, xs, mw["w_gate"], mw["w_up"], mw["w_down"])


def _combine_body(tm, final, dest_ref, dnext_ref, x1_ref, rf_ref, ys_ref, *rest):
    if final:
        fw_ref, o_ref, buf_ref, dest_smem, sems = rest
    else:
        o_ref, buf_ref, dest_smem, sems = rest
    i = pl.program_id(0)
    nsteps = pl.num_programs(0)
    slot = i % 2

    def start_row(j, s):
        for k in range(TOP_K):
            pltpu.make_async_copy(ys_ref.at[dest_smem[k, j]], buf_ref.at[s, k, j], sems.at[s]).start(priority=k)

    def wait_rows(s):
        for k in range(TOP_K):
            pltpu.make_async_copy(ys_ref.at[pl.ds(0, tm)], buf_ref.at[s, k], sems.at[s]).wait()

    def gather(src_ref, s):
        pltpu.sync_copy(src_ref.at[0], dest_smem)
        lax.fori_loop(0, tm, lambda j, c: (start_row(j, s), c)[1], 0, unroll=8)

    @pl.when(i == 0)
    def _():
        gather(dest_ref, 0)

    wait_rows(slot)

    @pl.when(i + 1 < nsteps)
    def _():
        gather(dnext_ref, 1 - slot)

    gates = rf_ref[...]
    out = x1_ref[...] + gates[:, 0:1] * _load_row_tiles(buf_ref, slot, 0) + \
        gates[:, 1:2] * _load_row_tiles(buf_ref, slot, 1)
    if final:
        out = _rms(out, fw_ref[...])
    o_ref[...] = out


def _combine(dest, x1, rf, ys, final_w, tm):
    n = x1.shape[0]
    nt = n // tm
    row = lambda w: pl.BlockSpec((tm, w), lambda i: (i, 0))
    in_specs = [pl.BlockSpec((1, TOP_K, tm), lambda i: (i, 0, 0)),
                pl.BlockSpec((1, TOP_K, tm), lambda i: (jnp.minimum(i + 1, nt - 1), 0, 0)),
                row(D_MODEL), row(LANES), pl.BlockSpec(memory_space=pl.ANY)]
    args = [dest, dest, x1, rf, ys]
    if final_w is not None:
        in_specs.append(_const_spec((1, D_MODEL)))
        args.append(final_w)
    return pl.pallas_call(
        functools.partial(_combine_body, tm, final_w is not None),
        grid=(nt,),
        in_specs=in_specs,
        out_specs=row(D_MODEL),
        out_shape=jax.ShapeDtypeStruct((n, D_MODEL), F32),
        scratch_shapes=[pltpu.VMEM((2, TOP_K, tm) + ROW_TILE, F32), pltpu.SMEM((TOP_K, tm), I32),
                        pltpu.SemaphoreType.DMA((2,))],
        compiler_params=_params(1),
        name="combine",
    )(*args)


def _moe_layer(streams, lw, mw, final_w):
    routed = [_router_layer(x, ys, ym, lw, mw, min(TOKEN_TILE, x.shape[0])) for x, ys, ym in streams]
    stream_counts = [r[4][0, :N_EXPERTS].astype(I32) for r in routed]
    counts = functools.reduce(lambda a, b: a + b, stream_counts)
    padded = (counts + MOE_TILE - 1) // MOE_TILE * MOE_TILE
    ends = jnp.cumsum(padded)
    offsets = ends - padded
    n_total = sum(x.shape[0] for x, _, _ in streams)
    rows = (n_total * TOP_K + N_EXPERTS * (MOE_TILE - 1)) // MOE_TILE * MOE_TILE
    tile_start = jnp.arange(rows // MOE_TILE, dtype=I32) * MOE_TILE
    tile_expert = jnp.minimum(jnp.sum(tile_start[:, None] >= ends[None, :], axis=1), N_EXPERTS - 1).astype(I32)
    group_end = (offsets + counts)[tile_expert]
    tile_valid = jnp.where(tile_start < ends[-1], jnp.clip(group_end - tile_start, 0, MOE_TILE), 0).astype(I32)
    bounds = jnp.concatenate([jnp.zeros((1,), I32), ends.astype(I32)])

    def tiles(dest, t):
        return dest.reshape(TOP_K, dest.shape[1] // t, t).transpose(1, 0, 2)

    xs, dests, start = None, [], offsets
    for (x1, h, ri, rf, _), own in zip(routed, stream_counts):
        n = x1.shape[0]
        dest = jnp.stack([start[ri[0]] + ri[2], start[ri[1]] + ri[3]])
        dests.append(dest)
        td = min(DISPATCH_TILE, n)
        start = start + own
        xs = _dispatch(h, tiles(dest, td), bounds, start // MOE_TILE * MOE_TILE, rows, td, xs)
    y_sorted = _experts(xs, tile_expert, tile_valid, mw, MOE_TILE)
    outs = []
    for (x1, h, ri, rf, _), dest in zip(routed, dests):
        tc = min(COMBINE_TILE, x1.shape[0])
        outs.append(_combine(tiles(dest, tc), x1, rf, y_sorted, final_w, tc))
    return outs


def _prep_layer(layer, norm_mix, w_in, conv_w, conv_b, dt_bias, a_log, d_skip, ssd_norm, q_norm, w_q_up,
                kv_norm, w_kv_up, w_out, norm_ffn):
    w = w_in[layer]
    splits = np.cumsum([SSD_INNER, CONV_DIM, SSD_HEADS, Q_LORA, KV_LORA, ROPE_DIM])[:-1].tolist()
    w_z, w_xbc, w_dt, w_cq, w_ckv, w_kr = jnp.split(w, splits, axis=1)
    half = ROPE_DIM // 2
    w_krs = jnp.concatenate([w_kr[:, half:], w_kr[:, :half]], axis=1)
    pad = jnp.zeros((D_MODEL, LANES - SSD_HEADS), F32)
    w_cat = jnp.concatenate([w_z, w_xbc, w_cq, w_ckv, w_kr, w_krs, w_dt, pad], axis=1).astype(BF16)
    wq = w_q_up[layer].reshape(Q_LORA, MLA_HEADS, QK_DIM)
    wq = jnp.concatenate([wq, wq[:, :, NOPE_DIM + half:], wq[:, :, NOPE_DIM:NOPE_DIM + half]], axis=2)
    row = lambda v: v.reshape(1, -1).astype(F32)
    return {
        "norm_mix": row(norm_mix[layer]), "w_in": w_cat,
        "q_norm": row(q_norm[layer]), "w_q": wq.reshape(Q_LORA, MLA_HEADS * Q_HEAD_COLS).astype(BF16),
        "kv_norm": row(kv_norm[layer]), "w_kv": w_kv_up[layer].astype(BF16),
        "dt_bias": jnp.pad(row(dt_bias[layer]), ((0, 0), (0, LANES - SSD_HEADS))),
        "conv_w": conv_w[layer], "conv_b": row(conv_b[layer]), "a_log": row(a_log[layer]),
        "d_skip": row(jnp.repeat(d_skip[layer], SSD_HEADDIM)), "ssd_norm": row(ssd_norm[layer]),
        "w_out": w_out[layer].astype(BF16), "norm_ffn": row(norm_ffn[layer]),
    }


def _rope_tables(pos, tm):
    half = ROPE_DIM // 2
    inv_freq = ROPE_THETA ** (-jnp.arange(half, dtype=F32) / half)
    ang = pos.astype(F32)[:, None] * inv_freq[None, :]
    cos, sin = jnp.cos(ang), jnp.sin(ang)
    cos_tab = jnp.concatenate([cos, cos], axis=1)
    sin_tab = jnp.concatenate([-sin, sin], axis=1)
    reps = max(1, tm // pos.shape[0])
    return jnp.tile(cos_tab, (reps, 1)), jnp.tile(sin_tab, (reps, 1))


def _mixing(x, bsz, length, lw, tables, layer, conv0, ssm0, past_lat, past_kr):
    n = bsz * length
    tm = min(TOKEN_TILE, n)
    z, xbc, dt, c_kv, k_rope, q, k, v = _inproj(x, lw, tables[0], tables[1], tm)
    ys, new_ssm, new_conv = _ssd(xbc, dt, z, lw, bsz, length, layer, conv0, ssm0)
    if past_lat is None:
        ym = _attention_prompt(q, k, v, bsz, length)
    else:
        past = past_lat.shape[2]
        ym = _attention_cache(q, k, v, past_lat.reshape(-1, KV_LORA), past_kr.reshape(-1, ROPE_DIM),
                              lw["w_kv"], bsz, length, past, layer)
    return ys, ym, c_kv, k_rope, new_ssm, new_conv


def kernel(x_prompt, x_sample, cache_kv_latent, cache_k_rope, state_ssm, state_conv, norm_mix, w_in, conv_w,
           conv_b, dt_bias, a_log, d_skip, ssd_norm, q_norm, w_q_up, kv_norm, w_kv_up, w_out, norm_ffn,
           ffn_w_gate, ffn_w_up, ffn_w_down, moe_router, moe_w_gate, moe_w_up, moe_w_down, final_norm):
    bp, sp = x_prompt.shape[:2]
    bs, ss = x_sample.shape[:2]
    depth = w_in.shape[0]
    past = cache_kv_latent.shape[2]
    tabs_p = _rope_tables(jnp.arange(sp, dtype=I32), min(TOKEN_TILE, bp * sp))
    tabs_s = _rope_tables(past + jnp.arange(ss, dtype=I32), min(TOKEN_TILE, bs * ss))
    final_w = final_norm.reshape(1, D_MODEL)
    hp = x_prompt.reshape(bp * sp, D_MODEL)
    hs = x_sample.reshape(bs * ss, D_MODEL)
    outs_p, outs_s = [], []
    for layer in range(depth):
        lw = _prep_layer(layer, norm_mix, w_in, conv_w, conv_b, dt_bias, a_log, d_skip, ssd_norm, q_norm,
                         w_q_up, kv_norm, w_kv_up, w_out, norm_ffn)
        last = final_w if layer == depth - 1 else None
        mix_p = _mixing(hp, bp, sp, lw, tabs_p, layer, None, None, None, None)
        mix_s = _mixing(hs, bs, ss, lw, tabs_s, layer, state_conv, state_ssm, cache_kv_latent, cache_k_rope)
        outs_p.append(mix_p[2:])
        outs_s.append(mix_s[2:])
        idx = layer // 2
        if layer % 2 == 0:
            fw = {"w_gate": ffn_w_gate[idx].astype(BF16), "w_up": ffn_w_up[idx].astype(BF16),
                  "w_down": ffn_w_down[idx].astype(BF16)}
            hp = _dense_layer(hp, mix_p[0], mix_p[1], lw, fw, last, min(TOKEN_TILE, bp * sp))
            hs = _dense_layer(hs, mix_s[0], mix_s[1], lw, fw, last, min(TOKEN_TILE, bs * ss))
        else:
            wr_hi = moe_router[idx].astype(BF16)
            wr_lo = (moe_router[idx] - wr_hi.astype(F32)).astype(BF16)
            mw = {"w_router": jnp.pad(jnp.concatenate([wr_hi, wr_lo], axis=1),
                                      ((0, 0), (0, LANES - 2 * N_EXPERTS))),
                  "w_gate": moe_w_gate[idx].astype(BF16), "w_up": moe_w_up[idx].astype(BF16),
                  "w_down": moe_w_down[idx].astype(BF16)}
            hp, hs = _moe_layer([(hp, mix_p[0], mix_p[1]), (hs, mix_s[0], mix_s[1])], lw, mw, last)

    def stack(outs, j, shape):
        return jnp.stack([o[j].reshape(shape) for o in outs])

    return (hp.reshape(bp, sp, D_MODEL), hs.reshape(bs, ss, D_MODEL),
            stack(outs_p, 0, (bp, sp, KV_LORA)), stack(outs_p, 1, (bp, sp, ROPE_DIM)),
            stack(outs_p, 2, (bp, SSD_HEADS, SSD_HEADDIM, SSD_STATE)),
            stack(outs_p, 3, (bp, CONV_WIDTH - 1, CONV_DIM)),
            stack(outs_s, 0, (bs, ss, KV_LORA)), stack(outs_s, 1, (bs, ss, ROPE_DIM)),
            stack(outs_s, 2, (bs, SSD_HEADS, SSD_HEADDIM, SSD_STATE)),
            stack(outs_s, 3, (bs, CONV_WIDTH - 1, CONV_DIM)))
```
